```python
import math
import jax
import jax.numpy as jnp
from jax import lax
import numpy as np

D_MODEL = 1024
BATCH = 8
SEQ = 8192
DEPTH = 2
DEC_BATCH = 8
DEC_SEQ = 4096
PAST_LEN = 128

D_MIX = D_MODEL
D_FF = 2816
EPS = 1e-6
MLA_HEADS = 8
MLA_Q_LORA = 384
MLA_KV_LORA = 256
MLA_NOPE = 64
MLA_ROPE = 32
MLA_QK = MLA_NOPE + MLA_ROPE
MLA_V = 64
ROPE_BASE = 10000.0
Q_BLOCK = 128
GLA_HEADS = 4
GLA_DK = 32
GLA_DV = 64
GLA_GATE_RANK = 16
GLA_GATE_NORM = 16.0
GLA_CHUNK = 16
SSD_HEADS = 4
SSD_HEADDIM = 64
SSD_INNER = SSD_HEADS * SSD_HEADDIM
SSD_GROUPS = 2
SSD_STATE = 128
SSD_CONV = 5
SSD_CHUNK = 128
SSD_CONV_DIM = SSD_INNER + 2 * SSD_GROUPS * SSD_STATE
MLA_IN = MLA_Q_LORA + MLA_KV_LORA + MLA_ROPE
GLA_IN = 2 * GLA_HEADS * GLA_DK + 2 * GLA_HEADS * GLA_DV + 2 * GLA_GATE_RANK
SSD_IN = SSD_INNER + SSD_CONV_DIM + 2 * SSD_HEADS
D_IN = MLA_IN + GLA_IN + SSD_IN
MLA_OUT = MLA_HEADS * MLA_V
GLA_OUT = GLA_HEADS * GLA_DV
SSD_OUT = SSD_INNER

kernel_name = 'hybrid_mla_gla_ssd_macaron_encoder'


def rms_norm(x, gain):
    xf = x.astype(jnp.float32)
    y = xf * lax.rsqrt(jnp.mean(xf * xf, axis=-1, keepdims=True) + EPS)
    return (y * gain.astype(jnp.float32)).astype(x.dtype)


def split_cols(x, sizes):
    return jnp.split(x, np.cumsum(sizes)[:-1].tolist(), axis=-1)


def swiglu(h, w_gu, w_down):
    g, u = jnp.split(h @ w_gu, 2, axis=-1)
    return (jax.nn.silu(g) * u) @ w_down


def flip_t(t):
    return jnp.flip(t, axis=1)


def rope_tables(length):
    pos = jnp.arange(length, dtype=jnp.float32)
    inv_freq = 1.0 / (ROPE_BASE ** (jnp.arange(0, MLA_ROPE, 2, dtype=jnp.float32) / MLA_ROPE))
    ang = pos[:, None] * inv_freq[None, :]
    return jnp.cos(ang)[:, None, :], jnp.sin(ang)[:, None, :]


def apply_rope(x, cos, sin):
    xf = x.astype(jnp.float32)
    x1, x2 = jnp.split(xf, 2, axis=-1)
    return jnp.concatenate([x1 * cos - x2 * sin, x1 * sin + x2 * cos], axis=-1).astype(x.dtype)


def block_attention(q, k, v):
    bsz, length, heads, dqk = q.shape
    n_blk = length // Q_BLOCK
    scale = dqk ** -0.5
    q_blocks = jnp.moveaxis(q.reshape(bsz, n_blk, Q_BLOCK, heads, dqk), 1, 0)

    def one_block(qb):
        s = jnp.einsum('bqhd,bkhd->bhqk', qb, k).astype(jnp.float32) * scale
        p = jax.nn.softmax(s, axis=-1).astype(v.dtype)
        return jnp.einsum('bhqk,bkhd->bqhd', p, v)

    o = lax.map(one_block, q_blocks)
    return jnp.moveaxis(o, 0, 1).reshape(bsz, length, heads * v.shape[-1])


def mla_mixer(p, q_norm, w_uq, kv_norm, w_ukv, q_gain, k_gain):
    bsz, length, _ = p.shape
    c_q, c_kv, k_rope = split_cols(p, [MLA_Q_LORA, MLA_KV_LORA, MLA_ROPE])
    q = (rms_norm(c_q, q_norm) @ w_uq).reshape(bsz, length, MLA_HEADS, MLA_QK)
    kv = (rms_norm(c_kv, kv_norm) @ w_ukv).reshape(bsz, length, MLA_HEADS, MLA_NOPE + MLA_V)
    k_nope, v = kv[..., :MLA_NOPE], kv[..., MLA_NOPE:]
    k_rope = jnp.broadcast_to(k_rope[:, :, None, :], (bsz, length, MLA_HEADS, MLA_ROPE))
    k = jnp.concatenate([k_nope, k_rope], axis=-1)
    q = rms_norm(q, q_gain)
    k = rms_norm(k, k_gain)
    cos, sin = rope_tables(length)
    q = jnp.concatenate([q[..., :MLA_NOPE], apply_rope(q[..., MLA_NOPE:], cos, sin)], axis=-1)
    k = jnp.concatenate([k[..., :MLA_NOPE], apply_rope(k[..., MLA_NOPE:], cos, sin)], axis=-1)
    return block_attention(q, k, v)


def gla_scan(q, k, v, log_a):
    bsz, length, heads, dk = q.shape
    dv = v.shape[-1]
    n = length // GLA_CHUNK

    def chunks(t):
        return t.reshape(bsz, n, GLA_CHUNK, heads, t.shape[-1]).transpose(0, 3, 1, 2, 4)

    qc, kc, vc, gc = chunks(q), chunks(k), chunks(v), chunks(log_a)
    b = jnp.cumsum(gc, axis=-2)
    tril = jnp.tril(jnp.ones((GLA_CHUNK, GLA_CHUNK), dtype=bool))
    diff = b[..., :, None, :] - b[..., None, :, :]
    decay = jnp.exp(jnp.where(tril[:, :, None], diff, -jnp.inf))
    attn = jnp.einsum('bhntsd,bhnsd->bhnts', qc[..., :, None, :] * decay, kc)
    o_intra = jnp.einsum('bhnts,bhnsv->bhntv', attn, vc)
    b_last = b[..., -1:, :]
    q_dec = qc * jnp.exp(b)
    k_dec = kc * jnp.exp(b_last - b)
    updates = jnp.einsum('bhnsd,bhnsv->bhndv', k_dec, vc)
    chunk_decay = jnp.exp(b_last[..., 0, :])

    def step(state, inp):
        upd, dec = inp
        return dec[..., None] * state + upd, state

    s0 = jnp.zeros((bsz, heads, dk, dv), jnp.float32)
    _, s_prev = lax.scan(step, s0, (jnp.moveaxis(updates, 2, 0), jnp.moveaxis(chunk_decay, 2, 0)))
    s_prev = jnp.moveaxis(s_prev, 0, 2)
    o_inter = jnp.einsum('bhntd,bhndv->bhntv', q_dec, s_prev)
    return (o_intra + o_inter).transpose(0, 2, 3, 1, 4).reshape(bsz, length, heads, dv)


def gla_mixer(p, w_gate, b_gate, o_norm):
    bsz, length, _ = p.shape
    q, k, v, g, lr = split_cols(p.astype(jnp.float32), [GLA_HEADS * GLA_DK, GLA_HEADS * GLA_DK,
                                                        GLA_HEADS * GLA_DV, GLA_HEADS * GLA_DV,
                                                        2 * GLA_GATE_RANK])
    q = q.reshape(bsz, length, GLA_HEADS, GLA_DK) * (GLA_DK ** -0.5)
    k = k.reshape(bsz, length, GLA_HEADS, GLA_DK)
    v = v.reshape(bsz, length, GLA_HEADS, GLA_DV)
    lr = lr.reshape(bsz, length, 2, GLA_GATE_RANK)
    log_a = jax.nn.log_sigmoid(jnp.einsum('blzr,zrd->blzd', lr, w_gate.astype(jnp.float32))
                               + b_gate.astype(jnp.float32)) / GLA_GATE_NORM
    log_a = log_a.reshape(bsz, length, 2, GLA_HEADS, GLA_DK)
    o = (gla_scan(q, k, v, log_a[:, :, 0])
         + flip_t(gla_scan(flip_t(q), flip_t(k), flip_t(v), flip_t(log_a[:, :, 1]))))
    o = rms_norm(o, o_norm).reshape(bsz, length, GLA_OUT) * jax.nn.silu(g)
    return o.astype(p.dtype)


def centred_depthwise_conv(x, w, b):
    ch = x.shape[-1]
    pad = SSD_CONV // 2
    y = lax.conv_general_dilated(x, w[:, None, :].astype(x.dtype), window_strides=(1,),
                                 padding=[(pad, pad)], dimension_numbers=('NWC', 'WIO', 'NWC'),
                                 feature_group_count=ch)
    return y + b.astype(x.dtype)


def ssd_scan(x, dt, a, bm, cm):
    bsz, length, heads, hd = x.shape
    nst = bm.shape[-1]
    n = length // SSD_CHUNK
    xc = (x * dt[..., None]).reshape(bsz, n, SSD_CHUNK, heads, hd)
    bc = bm.reshape(bsz, n, SSD_CHUNK, heads, nst)
    cc = cm.reshape(bsz, n, SSD_CHUNK, heads, nst)
    la = (dt * a).reshape(bsz, n, SSD_CHUNK, heads).transpose(0, 3, 1, 2)
    cum = jnp.cumsum(la, axis=-1)
    tril = jnp.tril(jnp.ones((SSD_CHUNK, SSD_CHUNK), dtype=bool))
    seg = jnp.exp(jnp.where(tril, cum[..., :, None] - cum[..., None, :], -jnp.inf))
    scores = jnp.einsum('bclhn,bcshn->bhcls', cc, bc) * seg
    y_diag = jnp.einsum('bhcls,bcshp->bclhp', scores, xc)
    decay_states = jnp.exp(cum[..., -1:] - cum)
    states = jnp.einsum('bclhn,bhcl,bclhp->bchpn', bc, decay_states, xc)
    chunk_decay = jnp.exp(cum[..., -1])

    def step(state, inp):
        st, dec = inp
        return dec[..., None, None] * state + st, state

    s0 = jnp.zeros((bsz, heads, hd, nst), jnp.float32)
    _, s_prev = lax.scan(step, s0, (jnp.moveaxis(states, 1, 0), jnp.moveaxis(chunk_decay, 2, 0)))
    s_prev = jnp.moveaxis(s_prev, 0, 1)
    y_off = jnp.einsum('bclhn,bchpn,bhcl->bclhp', cc, s_prev, jnp.exp(cum))
    return (y_diag + y_off).reshape(bsz, length, heads, hd)


def ssd_mixer(p, conv_w, conv_b, a_log, dt_bias, d_skip, norm_gain):
    bsz, length, _ = p.shape
    z, xbc, dt_raw = split_cols(p, [SSD_INNER, SSD_CONV_DIM, 2 * SSD_HEADS])
    xbc = jax.nn.silu(centred_depthwise_conv(xbc, conv_w, conv_b)).astype(jnp.float32)
    xs, bm, cm = split_cols(xbc, [SSD_INNER, SSD_GROUPS * SSD_STATE, SSD_GROUPS * SSD_STATE])
    rep = SSD_HEADS // SSD_GROUPS
    xs = xs.reshape(bsz, length, SSD_HEADS, SSD_HEADDIM)
    bm = jnp.repeat(bm.reshape(bsz, length, SSD_GROUPS, SSD_STATE), rep, axis=2)
    cm = jnp.repeat(cm.reshape(bsz, length, SSD_GROUPS, SSD_STATE), rep, axis=2)
    dt = jax.nn.softplus(dt_raw.astype(jnp.float32).reshape(bsz, length, 2, SSD_HEADS)
                         + dt_bias.astype(jnp.float32))
    a = -jnp.exp(a_log.astype(jnp.float32))
    y = (ssd_scan(xs, dt[:, :, 0], a[0], bm, cm)
         + flip_t(ssd_scan(flip_t(xs), flip_t(dt[:, :, 1]), a[1], flip_t(bm), flip_t(cm)))
         + d_skip.astype(jnp.float32)[:, None] * xs)
    gsz = SSD_INNER // SSD_GROUPS
    y = (y.reshape(bsz, length, SSD_GROUPS, gsz)
         * jax.nn.silu(z.astype(jnp.float32)).reshape(bsz, length, SSD_GROUPS, gsz))
    y = rms_norm(y, norm_gain.reshape(SSD_GROUPS, gsz))
    return y.reshape(bsz, length, SSD_OUT).astype(p.dtype)


def encoder_layer(x, ffn1_norm, ffn1_w_gu, ffn1_w_down, mix_norm, w_in,
                  mla_q_norm, mla_w_uq, mla_kv_norm, mla_w_ukv, mla_q_gain, mla_k_gain,
                  gla_w_gate, gla_b_gate, gla_o_norm,
                  ssd_conv_w, ssd_conv_b, ssd_a_log, ssd_dt_bias, ssd_d, ssd_norm,
                  w_out, ffn2_norm, ffn2_w_gu, ffn2_w_down, final_norm):
    x = x + 0.5 * swiglu(rms_norm(x, ffn1_norm), ffn1_w_gu, ffn1_w_down)
    proj = rms_norm(x, mix_norm) @ w_in
    p_mla, p_gla, p_ssd = split_cols(proj, [MLA_IN, GLA_IN, SSD_IN])
    o_mla = mla_mixer(p_mla, mla_q_norm, mla_w_uq, mla_kv_norm, mla_w_ukv, mla_q_gain, mla_k_gain)
    o_gla = gla_mixer(p_gla, gla_w_gate, gla_b_gate, gla_o_norm)
    o_ssd = ssd_mixer(p_ssd, ssd_conv_w, ssd_conv_b, ssd_a_log, ssd_dt_bias, ssd_d, ssd_norm)
    x = x + jnp.concatenate([o_mla, o_gla, o_ssd], axis=-1) @ w_out
    x = x + 0.5 * swiglu(rms_norm(x, ffn2_norm), ffn2_w_gu, ffn2_w_down)
    return rms_norm(x, final_norm)


def setup_inputs(seed: int = 0) -> dict:
    key = jax.random.key(seed)
    ks = jax.random.split(key, 27)

    def nrm(k, shape, scale):
        return jax.random.normal(k, shape, jnp.float32) * scale

    def gain(k, shape):
        return 1.0 + 0.01 * jax.random.normal(k, shape, jnp.float32)

    a_log = jnp.log(jax.random.uniform(ks[18], (DEPTH, 2, SSD_HEADS), jnp.float32, 1.0, 16.0))
    dt0 = jnp.exp(jax.random.uniform(ks[19], (DEPTH, 2, SSD_HEADS), jnp.float32,
                                     math.log(1e-3), math.log(1e-1)))
    dt_bias = dt0 + jnp.log(-jnp.expm1(-dt0))
    return {
        'x_prompt': nrm(ks[0], (BATCH, SEQ, D_MODEL), 1.0),
        'x_sample': nrm(ks[1], (DEC_BATCH, DEC_SEQ, D_MODEL), 1.0),
        'ffn1_norm': gain(ks[2], (DEPTH, D_MODEL)),
        'ffn1_w_gu': nrm(ks[3], (DEPTH, D_MODEL, 2 * D_FF), D_MODEL ** -0.5),
        'ffn1_w_down': nrm(ks[4], (DEPTH, D_FF, D_MODEL), D_FF ** -0.5),
        'mix_norm': gain(ks[5], (DEPTH, D_MODEL)),
        'w_in': nrm(ks[6], (DEPTH, D_MODEL, D_IN), D_MODEL ** -0.5),
        'mla_q_norm': gain(ks[7], (DEPTH, MLA_Q_LORA)),
        'mla_w_uq': nrm(ks[8], (DEPTH, MLA_Q_LORA, MLA_HEADS * MLA_QK), MLA_Q_LORA ** -0.5),
        'mla_kv_norm': gain(ks[9], (DEPTH, MLA_KV_LORA)),
        'mla_w_ukv': nrm(ks[10], (DEPTH, MLA_KV_LORA, MLA_HEADS * (MLA_NOPE + MLA_V)), MLA_KV_LORA ** -0.5),
        'mla_q_gain': gain(ks[11], (DEPTH, MLA_QK)),
        'mla_k_gain': gain(ks[12], (DEPTH, MLA_QK)),
        'gla_w_gate': nrm(ks[13], (DEPTH, 2, GLA_GATE_RANK, GLA_HEADS * GLA_DK), GLA_GATE_RANK ** -0.5),
        'gla_b_gate': nrm(ks[14], (DEPTH, 2, GLA_HEADS * GLA_DK), 0.1),
        'gla_o_norm': gain(ks[15], (DEPTH, GLA_DV)),
        'ssd_conv_w': nrm(ks[16], (DEPTH, SSD_CONV, SSD_CONV_DIM), SSD_CONV ** -0.5),
        'ssd_conv_b': nrm(ks[17], (DEPTH, SSD_CONV_DIM), 0.01),
        'ssd_a_log': a_log,
        'ssd_dt_bias': dt_bias,
        'ssd_d': gain(ks[20], (DEPTH, SSD_HEADS)),
        'ssd_norm': gain(ks[21], (DEPTH, SSD_INNER)),
        'w_out': nrm(ks[22], (DEPTH, D_MIX, D_MODEL), D_MIX ** -0.5),
        'ffn2_norm': gain(ks[23], (DEPTH, D_MODEL)),
        'ffn2_w_gu': nrm(ks[24], (DEPTH, D_MODEL, 2 * D_FF), D_MODEL ** -0.5),
        'ffn2_w_down': nrm(ks[25], (DEPTH, D_FF, D_MODEL), D_FF ** -0.5),
        'final_norm': gain(ks[26], (DEPTH, D_MODEL)),
    }


def reference(x_prompt, x_sample, ffn1_norm, ffn1_w_gu, ffn1_w_down, mix_norm, w_in,
              mla_q_norm, mla_w_uq, mla_kv_norm, mla_w_ukv, mla_q_gain, mla_k_gain,
              gla_w_gate, gla_b_gate, gla_o_norm,
              ssd_conv_w, ssd_conv_b, ssd_a_log, ssd_dt_bias, ssd_d, ssd_norm,
              w_out, ffn2_norm, ffn2_w_gu, ffn2_w_down, final_norm):
    y_prompt = x_prompt
    y_sample = x_sample
    for i in range(DEPTH):
        lw = (ffn1_norm[i], ffn1_w_gu[i], ffn1_w_down[i], mix_norm[i], w_in[i],
              mla_q_norm[i], mla_w_uq[i], mla_kv_norm[i], mla_w_ukv[i], mla_q_gain[i], mla_k_gain[i],
              gla_w_gate[i], gla_b_gate[i], gla_o_norm[i],
              ssd_conv_w[i], ssd_conv_b[i], ssd_a_log[i], ssd_dt_bias[i], ssd_d[i], ssd_norm[i],
              w_out[i], ffn2_norm[i], ffn2_w_gu[i], ffn2_w_down[i], final_norm[i])
        y_prompt = encoder_layer(y_prompt, *lw)
        y_sample = encoder_layer(y_sample, *lw)
    return (y_prompt, y_sample)
```

```python
import functools
import math

import jax
import jax.numpy as jnp
import numpy as np
from jax import lax
from jax.experimental import pallas as pl
from jax.experimental.pallas import tpu as pltpu

F32 = jnp.float32
BF16 = jnp.bfloat16

D_MODEL = 1024
D_FF = 2816
EPS = 1e-6
MLA_HEADS = 8
MLA_Q_LORA = 384
MLA_KV_LORA = 256
MLA_NOPE = 64
MLA_ROPE = 32
MLA_QK = MLA_NOPE + MLA_ROPE
MLA_V = 64
ROPE_BASE = 10000.0
GLA_HEADS = 4
GLA_DK = 32
GLA_DV = 64
GLA_GATE_RANK = 16
GLA_GATE_NORM = 16.0
GLA_CHUNK = 16
SSD_HEADS = 4
SSD_HEADDIM = 64
SSD_INNER = SSD_HEADS * SSD_HEADDIM
SSD_GROUPS = 2
SSD_STATE = 128
SSD_CONV = 5
SSD_CHUNK = 128
SSD_CONV_DIM = SSD_INNER + 2 * SSD_GROUPS * SSD_STATE
MLA_IN = MLA_Q_LORA + MLA_KV_LORA + MLA_ROPE
GLA_IN = 2 * GLA_HEADS * GLA_DK + 2 * GLA_HEADS * GLA_DV + 2 * GLA_GATE_RANK
SSD_IN = SSD_INNER + SSD_CONV_DIM + 2 * SSD_HEADS

LANES = 128
SUBLANES = 8
MXU_DIM = 256
VMEM_LIMIT = 56 * 1024 * 1024

HEAD_PAD = LANES
GLA_QK = GLA_HEADS * GLA_DK
GLA_V = GLA_HEADS * GLA_DV
FF_CHUNK = MXU_DIM
N_FF_CHUNKS = D_FF // FF_CHUNK

P_MLA = MLA_Q_LORA + MLA_KV_LORA + HEAD_PAD
P_QKV = 2 * GLA_QK + GLA_V
P_LR = LANES
P_G = GLA_V
P_Z = SSD_INNER
P_XBC = SSD_CONV_DIM
P_DT = LANES
P_TOTAL = P_MLA + P_QKV + P_LR + P_G + P_Z + P_XBC + P_DT

NEG_BIG = -1e30
LOG2E = 1.4426950408889634


def _rms(x, gain):
    return x * lax.rsqrt(jnp.mean(x * x, axis=-1, keepdims=True) + EPS) * gain


def _split_hi_lo(x):
    hi = x.astype(BF16)
    lo = (x - hi.astype(F32)).astype(BF16)
    return hi, lo


def _dot(a, b):
    return jnp.dot(a, b, preferred_element_type=F32)


def _dot_nt(a, b):
    return lax.dot_general(a, b, (((1,), (1,)), ((), ())), preferred_element_type=F32)


def _dot_tn(a, b):
    return lax.dot_general(a, b, (((0,), (0,)), ((), ())), preferred_element_type=F32)


def _cparams(sem):
    return pltpu.CompilerParams(dimension_semantics=sem, vmem_limit_bytes=VMEM_LIMIT)


def _const_spec(shape):
    nd = len(shape)
    return pl.BlockSpec(shape, lambda *_: (0,) * nd, pipeline_mode=pl.Buffered(1))


def _swiglu_into(h_ref, wg_ref, wu_ref, wd_ref, acc_ref):
    acc_ref[...] = jnp.zeros_like(acc_ref)

    def body(c, carry):
        h = h_ref[...]
        g = _dot(h, wg_ref[c])
        u = _dot(h, wu_ref[c])
        a = (g * jax.nn.sigmoid(g) * u).astype(BF16)
        acc_ref[...] += _dot(a, wd_ref[c])
        return carry

    lax.fori_loop(0, N_FF_CHUNKS, body, 0)


def _ffn_in_kernel(x_ref, n1_ref, wg_ref, wu_ref, wd_ref, n2_ref, win_ref,
                   x1_ref, mla_ref, qkv_ref, lr_ref, g_ref, z_ref, xbc_ref, dt_ref,
                   h_ref, acc_ref):
    x = x_ref[...]
    h_ref[...] = _rms(x, n1_ref[...]).astype(BF16)
    _swiglu_into(h_ref, wg_ref, wu_ref, wd_ref, acc_ref)
    x1 = x + 0.5 * acc_ref[...]
    x1_ref[...] = x1
    h2 = _rms(x1, n2_ref[...]).astype(BF16)
    off = 0
    for ref, width in ((mla_ref, P_MLA), (qkv_ref, P_QKV), (lr_ref, P_LR), (g_ref, P_G),
                       (z_ref, P_Z), (xbc_ref, P_XBC), (dt_ref, P_DT)):
        ref[...] = _dot(h2, win_ref[:, off:off + width])
        off += width


def _ffn_in(x, n1, wg, wu, wd, n2, win, tm):
    t = x.shape[0]
    widths = (D_MODEL, P_MLA, P_QKV, P_LR, P_G, P_Z, P_XBC, P_DT)
    tok = lambda w: pl.BlockSpec((tm, w), lambda i: (i, 0))
    return pl.pallas_call(
        _ffn_in_kernel,
        grid=(t // tm,),
        in_specs=[tok(D_MODEL), _const_spec(n1.shape), _const_spec(wg.shape), _const_spec(wu.shape),
                  _const_spec(wd.shape), _const_spec(n2.shape), _const_spec(win.shape)],
        out_specs=[tok(w) for w in widths],
        out_shape=[jax.ShapeDtypeStruct((t, w), F32) for w in widths],
        scratch_shapes=[pltpu.VMEM((tm, D_MODEL), BF16), pltpu.VMEM((tm, D_MODEL), F32)],
        compiler_params=_cparams(("parallel",)),
        name="ffn_in",
    )(x, n1, wg, wu, wd, n2, win)


def _ffn_out_kernel(x1_ref, omla_ref, ogla_ref, g_ref, yssd_ref, z_ref,
                    onorm_ref, snorm_ref, blk64_ref, blk128_ref, wout_ref,
                    n_ref, wg_ref, wu_ref, wd_ref, fn_ref,
                    y_ref, h_ref, acc_ref):
    og = ogla_ref[0] + ogla_ref[1]
    ss = _dot((og * og).astype(BF16), blk64_ref[...])
    g = g_ref[...]
    m_gla = og * lax.rsqrt(ss * (1.0 / GLA_DV) + EPS) * onorm_ref[...] * (g * jax.nn.sigmoid(g))
    z = z_ref[...]
    ys = (yssd_ref[0] + yssd_ref[1]) * (z * jax.nn.sigmoid(z))
    ss2 = _dot((ys * ys).astype(BF16), blk128_ref[...])
    m_ssd = ys * lax.rsqrt(ss2 * (1.0 / (SSD_INNER // SSD_GROUPS)) + EPS) * snorm_ref[...]
    m = jnp.concatenate([omla_ref[...], m_gla.astype(BF16), m_ssd.astype(BF16)], axis=-1)
    x2 = x1_ref[...] + _dot(m, wout_ref[...])
    h_ref[...] = _rms(x2, n_ref[...]).astype(BF16)
    _swiglu_into(h_ref, wg_ref, wu_ref, wd_ref, acc_ref)
    x3 = x2 + 0.5 * acc_ref[...]
    y_ref[...] = _rms(x3, fn_ref[...])


def _ffn_out(x1, omla, ogla, g, yssd, z, onorm, snorm, blk64, blk128, wout, n, wg, wu, wd, fn, tm):
    t = x1.shape[0]
    tok = lambda w: pl.BlockSpec((tm, w), lambda i: (i, 0))
    tok2 = lambda w: pl.BlockSpec((2, tm, w), lambda i: (0, i, 0))
    consts = (onorm, snorm, blk64, blk128, wout, n, wg, wu, wd, fn)
    return pl.pallas_call(
        _ffn_out_kernel,
        grid=(t // tm,),
        in_specs=[tok(D_MODEL), tok(MLA_HEADS * MLA_V), tok2(GLA_V), tok(GLA_V), tok2(SSD_INNER), tok(SSD_INNER)]
        + [_const_spec(c.shape) for c in consts],
        out_specs=tok(D_MODEL),
        out_shape=jax.ShapeDtypeStruct((t, D_MODEL), F32),
        scratch_shapes=[pltpu.VMEM((tm, D_MODEL), BF16), pltpu.VMEM((tm, D_MODEL), F32)],
        compiler_params=_cparams(("parallel",)),
        name="ffn_out",
    )(x1, omla, ogla, g, yssd, z, *consts)


def _mla_prep_kernel(p_ref, qn_ref, wuqt_ref, kvn_ref, wk_ref, wvt_ref, gq_ref, gk_ref,
                     cost_ref, sint_ref, cn_ref, s1_ref, s2_ref,
                     qt_ref, k_ref, vt_ref):
    p = p_ref[0]
    cq = p[:, :MLA_Q_LORA]
    ckv = p[:, MLA_Q_LORA:MLA_Q_LORA + MLA_KV_LORA]
    kr = p[:, MLA_Q_LORA + MLA_KV_LORA:]
    hq = _rms(cq, qn_ref[...]).astype(BF16)
    hkv = _rms(ckv, kvn_ref[...]).astype(BF16)

    qt = _dot_nt(wuqt_ref[...], hq)
    cos_t = cost_ref[...]
    sin_t = sint_ref[...]
    gq = gq_ref[...]
    half = MLA_ROPE // 2
    for h in range(MLA_HEADS):
        x = qt[h * HEAD_PAD:(h + 1) * HEAD_PAD]
        ss = jnp.sum(x * x, axis=0, keepdims=True)
        x = x * lax.rsqrt(ss * (1.0 / MLA_QK) + EPS) * gq
        x1 = x[MLA_NOPE:MLA_NOPE + half]
        x2 = x[MLA_NOPE + half:MLA_QK]
        qt_ref[0, h, 0:MLA_NOPE, :] = x[0:MLA_NOPE].astype(BF16)
        qt_ref[0, h, MLA_NOPE:MLA_NOPE + half, :] = (x1 * cos_t - x2 * sin_t).astype(BF16)
        qt_ref[0, h, MLA_NOPE + half:MLA_QK, :] = (x1 * sin_t + x2 * cos_t).astype(BF16)
        qt_ref[0, h, MLA_QK:HEAD_PAD, :] = x[MLA_QK:HEAD_PAD].astype(BF16)

    gk = gk_ref[...]
    krg = kr * gk
    k_rot = (krg * cn_ref[...] + pltpu.roll(krg, HEAD_PAD - half, axis=1) * s1_ref[...]
             + pltpu.roll(krg, half, axis=1) * s2_ref[...])
    ss_rope = jnp.sum(kr * kr, axis=-1, keepdims=True)
    kn = _dot(hkv, wk_ref[...])
    for h in range(MLA_HEADS):
        x = kn[:, h * HEAD_PAD:(h + 1) * HEAD_PAD]
        ss = jnp.sum(x * x, axis=-1, keepdims=True) + ss_rope
        k_ref[0, h] = ((x * gk + k_rot) * lax.rsqrt(ss * (1.0 / MLA_QK) + EPS)).astype(BF16)

    vt = _dot_nt(wvt_ref[...], hkv)
    for h in range(MLA_HEADS):
        vt_ref[0, h] = vt[h * MLA_V:(h + 1) * MLA_V].astype(BF16)


def _mla_prep(p_mla, qn, wuqt, kvn, wk, wvt, gq, gk, cos_t, sin_t, c_n, s1_n, s2_n, tm):
    b, l, _ = p_mla.shape
    consts = (qn, wuqt, kvn, wk, wvt, gq, gk)
    half = MLA_ROPE // 2
    return pl.pallas_call(
        _mla_prep_kernel,
        grid=(b, l // tm),
        in_specs=[pl.BlockSpec((1, tm, P_MLA), lambda i, j: (i, j, 0))]
        + [_const_spec(c.shape) for c in consts]
        + [pl.BlockSpec((half, tm), lambda i, j: (0, j)), pl.BlockSpec((half, tm), lambda i, j: (0, j)),
           pl.BlockSpec((tm, HEAD_PAD), lambda i, j: (j, 0)), pl.BlockSpec((tm, HEAD_PAD), lambda i, j: (j, 0)),
           pl.BlockSpec((tm, HEAD_PAD), lambda i, j: (j, 0))],
        out_specs=[pl.BlockSpec((1, MLA_HEADS, HEAD_PAD, tm), lambda i, j: (i, 0, 0, j)),
                   pl.BlockSpec((1, MLA_HEADS, tm, HEAD_PAD), lambda i, j: (i, 0, j, 0)),
                   pl.BlockSpec((1, MLA_HEADS, MLA_V, tm), lambda i, j: (i, 0, 0, j))],
        out_shape=[jax.ShapeDtypeStruct((b, MLA_HEADS, HEAD_PAD, l), BF16),
                   jax.ShapeDtypeStruct((b, MLA_HEADS, l, HEAD_PAD), BF16),
                   jax.ShapeDtypeStruct((b, MLA_HEADS, MLA_V, l), BF16)],
        compiler_params=_cparams(("parallel", "parallel")),
        name="mla_prep",
    )(p_mla, *consts, cos_t, sin_t, c_n, s1_n, s2_n)


ATTN_HEADS_PER_STEP = 2


def _mla_attn_kernel(qt_ref, k_ref, vt_ref, o_ref, ot_ref, *, tk):
    n_kb = k_ref.shape[2] // tk
    tq = qt_ref.shape[3]
    for hh in range(ATTN_HEADS_PER_STEP):
        qt = qt_ref[0, hh]

        def body(kb, carry):
            m, l, acc = carry
            start = pl.multiple_of(kb * tk, tk)
            s = _dot(k_ref[0, hh, pl.ds(start, tk), :], qt)
            m_new = jnp.maximum(m, jnp.max(s, axis=0, keepdims=True))
            alpha = jnp.exp2(m - m_new)
            p = jnp.exp2(s - m_new)
            l = alpha * l + jnp.sum(p, axis=0, keepdims=True)
            acc = alpha * acc + _dot(vt_ref[0, hh, :, pl.ds(start, tk)], p.astype(BF16))
            return m_new, l, acc

        m0 = jnp.full((1, tq), -jnp.inf, F32)
        l0 = jnp.zeros((1, tq), F32)
        a0 = jnp.zeros((MLA_V, tq), F32)
        _, l, acc = lax.fori_loop(0, n_kb, body, (m0, l0, a0))
        ot_ref[hh * MLA_V:(hh + 1) * MLA_V, :] = acc * (1.0 / l)
    o_ref[0] = ot_ref[...].T.astype(BF16)


def _mla_attn(qt, k, vt, tq, tk):
    b, nh, _, l = qt.shape
    hp = ATTN_HEADS_PER_STEP
    return pl.pallas_call(
        functools.partial(_mla_attn_kernel, tk=tk),
        grid=(b, nh // hp, l // tq),
        in_specs=[pl.BlockSpec((1, hp, HEAD_PAD, tq), lambda i, h, j: (i, h, 0, j)),
                  pl.BlockSpec((1, hp, l, HEAD_PAD), lambda i, h, j: (i, h, 0, 0)),
                  pl.BlockSpec((1, hp, MLA_V, l), lambda i, h, j: (i, h, 0, 0))],
        out_specs=pl.BlockSpec((1, tq, hp * MLA_V), lambda i, h, j: (i, j, h)),
        out_shape=jax.ShapeDtypeStruct((b, l, nh * MLA_V), BF16),
        scratch_shapes=[pltpu.VMEM((hp * MLA_V, tq), F32)],
        compiler_params=_cparams(("parallel", "parallel", "arbitrary")),
        name="mla_attn",
    )(qt, k, vt)


def _gla_kernel(qkv_ref, lr_ref, wgh_ref, wgl_ref, bg_ref, cum_ref, tot_ref, ind_ref, hmask_ref,
                o_ref, st_ref, qd_ref, kd_ref, v_ref, dec_ref, oi_ref, *, tg):
    d = pl.program_id(1)
    t = pl.program_id(2)
    n_chunks = tg // GLA_CHUNK

    @pl.when(t == 0)
    def _():
        st_ref[...] = jnp.zeros_like(st_ref)

    qkv = qkv_ref[0]
    q = qkv[:, :GLA_QK] * (GLA_DK ** -0.5)
    k = qkv[:, GLA_QK:2 * GLA_QK]
    v = qkv[:, 2 * GLA_QK:]

    lr_hi, lr_lo = _split_hi_lo(lr_ref[0])
    zg = _dot(lr_hi, wgh_ref[0]) + _dot(lr_lo, wgh_ref[0]) + _dot(lr_hi, wgl_ref[0]) + bg_ref[0]
    log_a = (jnp.minimum(zg, 0.0) - jnp.log1p(jnp.exp(-jnp.abs(zg)))) * (1.0 / GLA_GATE_NORM)

    la_hi, la_lo = _split_hi_lo(log_a)
    bcum = _dot(cum_ref[0], la_hi) + _dot(cum_ref[0], la_lo)
    btot = _dot(tot_ref[...], la_hi) + _dot(tot_ref[...], la_lo)

    qd_ref[...] = (q * jnp.exp(bcum)).astype(BF16)
    kd_ref[...] = (k * jnp.exp(btot - bcum)).astype(BF16)
    v_ref[...] = v.astype(BF16)
    dec_ref[...] = jnp.exp(btot)

    q3 = q.reshape(n_chunks, GLA_CHUNK, GLA_QK)
    k3 = k.reshape(n_chunks, GLA_CHUNK, GLA_QK)
    b3 = bcum.reshape(n_chunks, GLA_CHUNK, GLA_QK)
    v3 = v.reshape(n_chunks, GLA_CHUNK, GLA_V)
    fwd = d == 0
    tin = lax.broadcasted_iota(jnp.int32, (n_chunks, GLA_CHUNK, GLA_QK), 1)
    tin = jnp.where(fwd, tin, -tin)
    ind = ind_ref[...]
    o_intra = jnp.zeros((tg, GLA_V), F32)
    for j in range(GLA_CHUNK):
        valid = tin >= jnp.where(fwd, j, -j)
        e = jnp.exp(jnp.where(valid, b3 - b3[:, j:j + 1, :], NEG_BIG))
        pj = (q3 * k3[:, j:j + 1, :] * e).reshape(tg, GLA_QK).astype(BF16)
        aj = _dot(pj, ind).reshape(n_chunks, GLA_CHUNK, GLA_V)
        o_intra = o_intra + (aj * v3[:, j:j + 1, :]).reshape(tg, GLA_V)

    hmask = hmask_ref[...]

    def body(ci, carry):
        c = jnp.where(fwd, ci, n_chunks - 1 - ci)
        rows = pl.ds(pl.multiple_of(c * GLA_CHUNK, GLA_CHUNK), GLA_CHUNK)
        st = st_ref[...]
        oi_ref[rows, :] = _dot_nt(qd_ref[rows, :], st.astype(BF16))
        upd = _dot_tn(v_ref[rows, :], kd_ref[rows, :])
        st_ref[...] = st * dec_ref[pl.ds(c * GLA_CHUNK, 1), :] + upd * hmask
        return carry

    lax.fori_loop(0, n_chunks, body, 0)
    o_ref[0, 0] = o_intra + oi_ref[...]


def _gla_scan(qkv, lr, wgh, wgl, bg, cum, tot, ind, hmask, tg):
    b, l, _ = qkv.shape
    nt = l // tg
    tile = lambda d, t: t + d * (nt - 1 - 2 * t)
    return pl.pallas_call(
        functools.partial(_gla_kernel, tg=tg),
        grid=(b, 2, nt),
        in_specs=[pl.BlockSpec((1, tg, P_QKV), lambda i, d, t: (i, tile(d, t), 0)),
                  pl.BlockSpec((1, tg, P_LR), lambda i, d, t: (i, tile(d, t), 0)),
                  pl.BlockSpec((1, LANES, GLA_QK), lambda i, d, t: (d, 0, 0)),
                  pl.BlockSpec((1, LANES, GLA_QK), lambda i, d, t: (d, 0, 0)),
                  pl.BlockSpec((1, 1, GLA_QK), lambda i, d, t: (d, 0, 0)),
                  pl.BlockSpec((1, tg, tg), lambda i, d, t: (d, 0, 0)),
                  _const_spec(tot.shape), _const_spec(ind.shape), _const_spec(hmask.shape)],
        out_specs=pl.BlockSpec((1, 1, tg, GLA_V), lambda i, d, t: (d, i, tile(d, t), 0)),
        out_shape=jax.ShapeDtypeStruct((2, b, l, GLA_V), F32),
        scratch_shapes=[pltpu.VMEM((GLA_V, GLA_QK), F32),
                        pltpu.VMEM((tg, GLA_QK), BF16), pltpu.VMEM((tg, GLA_QK), BF16),
                        pltpu.VMEM((tg, GLA_V), BF16), pltpu.VMEM((tg, GLA_QK), F32),
                        pltpu.VMEM((tg, GLA_V), F32)],
        compiler_params=_cparams(("parallel", "parallel", "arbitrary")),
        name="gla_scan",
    )(qkv, lr, wgh, wgl, bg, cum, tot, ind, hmask)


CONV_HALO = SUBLANES


def _softplus(x):
    return jnp.maximum(x, 0.0) + jnp.log1p(jnp.exp(-jnp.abs(x)))


def _ssd_kernel(xc_ref, xp_ref, xn_ref, dt_ref, dtt_ref, cw_ref, cb_ref, bias_r_ref, bias_c_ref,
                alog_r_ref, alog_c_ref, dskip_ref, tri_ref,
                y_ref, xe_ref, st_ref, *, n_tiles):
    d = pl.program_id(1)
    t = pl.program_id(2)
    tt = t + d * (n_tiles - 1 - 2 * t)
    ts = SSD_CHUNK
    fwd = d == 0

    @pl.when(t == 0)
    def _():
        st_ref[...] = jnp.zeros_like(st_ref)

    xe_ref[0:CONV_HALO, :] = jnp.where(tt == 0, 0.0, xp_ref[0])
    xe_ref[CONV_HALO:CONV_HALO + ts, :] = xc_ref[0]
    xe_ref[CONV_HALO + ts:, :] = jnp.where(tt == n_tiles - 1, 0.0, xn_ref[0])
    pad = SSD_CONV // 2
    acc = jnp.zeros((ts, SSD_CONV_DIM), F32) + cb_ref[...]
    for kk in range(SSD_CONV):
        acc = acc + cw_ref[kk:kk + 1, :] * xe_ref[pl.ds(CONV_HALO - pad + kk, ts), :]
    act = acc * jax.nn.sigmoid(acc)
    xs = act[:, :SSD_INNER]
    bm = act[:, SSD_INNER:SSD_INNER + SSD_GROUPS * SSD_STATE]
    cm = act[:, SSD_INNER + SSD_GROUPS * SSD_STATE:]

    dt_c = _softplus(dt_ref[0] + bias_r_ref[...])
    la_c = dt_c * (-jnp.exp(alog_r_ref[...]))
    dt_r = _softplus(dtt_ref[0] + bias_c_ref[...])
    la_r = dt_r * (-jnp.exp(alog_c_ref[...]))
    tri_d = tri_ref[d]
    tri_o = tri_ref[1 - d]
    lc_hi, lc_lo = _split_hi_lo(la_c)
    cum_c = _dot(tri_d, lc_hi) + _dot(tri_d, lc_lo)
    lr_hi, lr_lo = _split_hi_lo(la_r)
    cum_r = _dot(lr_hi, tri_o) + _dot(lr_lo, tri_o)
    tot_c = jnp.sum(la_c, axis=0, keepdims=True)
    mask = tri_d.astype(F32) > 0.5

    ys = []
    for g in range(SSD_GROUPS):
        bg = bm[:, g * SSD_STATE:(g + 1) * SSD_STATE]
        cg = cm[:, g * SSD_STATE:(g + 1) * SSD_STATE]
        cb = _dot_nt(cg.astype(BF16), bg.astype(BF16))
        bgt = bg.T.astype(BF16)
        for hg in range(SSD_HEADS // SSD_GROUPS):
            h = g * (SSD_HEADS // SSD_GROUPS) + hg
            pick_c = lambda a: jnp.where(fwd, a[:, h:h + 1], a[:, SSD_HEADS + h:SSD_HEADS + h + 1])
            pick_r = lambda a: jnp.where(fwd, a[h:h + 1, :], a[SSD_HEADS + h:SSD_HEADS + h + 1, :])
            cc = pick_c(cum_c)
            cr = pick_r(cum_r)
            dth = pick_c(dt_c)
            tot = pick_c(tot_c)
            seg = jnp.exp(jnp.where(mask, cc - cr, NEG_BIG))
            xch = xs[:, h * SSD_HEADDIM:(h + 1) * SSD_HEADDIM] * dth
            st = st_ref[h]
            lhs = jnp.concatenate([(cb * seg).astype(BF16), (cg * jnp.exp(cc)).astype(BF16)], axis=1)
            rhs = jnp.concatenate([xch.astype(BF16), st.astype(BF16)], axis=0)
            ys.append(_dot(lhs, rhs))
            xdec = (xch * jnp.exp(tot - cc)).astype(BF16)
            st_ref[h] = st * jnp.exp(tot) + _dot(bgt, xdec)
    y = jnp.concatenate(ys, axis=1)
    y_ref[0, 0] = y + jnp.where(fwd, 1.0, 0.0) * (dskip_ref[...] * xs)


def _ssd_scan(xbc, dt, dtt, cw, cb, bias_r, bias_c, alog_r, alog_c, dskip, tri):
    b, l, _ = xbc.shape
    ts = SSD_CHUNK
    nt = l // ts
    hb = ts // CONV_HALO
    n_hb = l // CONV_HALO
    tile = lambda d, t: t + d * (nt - 1 - 2 * t)
    consts = (cw, cb, bias_r, bias_c, alog_r, alog_c, dskip, tri)
    return pl.pallas_call(
        functools.partial(_ssd_kernel, n_tiles=nt),
        grid=(b, 2, nt),
        in_specs=[pl.BlockSpec((1, ts, P_XBC), lambda i, d, t: (i, tile(d, t), 0)),
                  pl.BlockSpec((1, CONV_HALO, P_XBC), lambda i, d, t: (i, jnp.maximum(tile(d, t) * hb - 1, 0), 0)),
                  pl.BlockSpec((1, CONV_HALO, P_XBC),
                               lambda i, d, t: (i, jnp.minimum((tile(d, t) + 1) * hb, n_hb - 1), 0)),
                  pl.BlockSpec((1, ts, P_DT), lambda i, d, t: (i, tile(d, t), 0)),
                  pl.BlockSpec((1, 2 * SSD_HEADS, ts), lambda i, d, t: (i, 0, tile(d, t)))]
        + [_const_spec(c.shape) for c in consts],
        out_specs=pl.BlockSpec((1, 1, ts, SSD_INNER), lambda i, d, t: (d, i, tile(d, t), 0)),
        out_shape=jax.ShapeDtypeStruct((2, b, l, SSD_INNER), F32),
        scratch_shapes=[pltpu.VMEM((ts + 2 * CONV_HALO, P_XBC), F32),
                        pltpu.VMEM((SSD_HEADS, SSD_STATE, SSD_HEADDIM), F32)],
        compiler_params=_cparams(("parallel", "parallel", "arbitrary")),
        name="ssd_scan",
    )(xbc, xbc, xbc, dt, dtt, *consts)


def _pad_cols(w, width, at=0):
    out = jnp.zeros(w.shape[:-1] + (width,), w.dtype)
    return out.at[..., at:at + w.shape[-1]].set(w)


def _prep_layer(i, ffn1_norm, ffn1_w_gu, ffn1_w_down, mix_norm, w_in, mla_q_norm, mla_w_uq, mla_kv_norm,
                mla_w_ukv, mla_q_gain, mla_k_gain, gla_w_gate, gla_b_gate, gla_o_norm, ssd_conv_w, ssd_conv_b,
                ssd_a_log, ssd_dt_bias, ssd_d, ssd_norm, w_out, ffn2_norm, ffn2_w_gu, ffn2_w_down, final_norm):
    def ffn_w(w_gu, w_down):
        wg = w_gu[:, :D_FF].reshape(D_MODEL, N_FF_CHUNKS, FF_CHUNK).transpose(1, 0, 2).astype(BF16)
        wu = w_gu[:, D_FF:].reshape(D_MODEL, N_FF_CHUNKS, FF_CHUNK).transpose(1, 0, 2).astype(BF16)
        wd = w_down.reshape(N_FF_CHUNKS, FF_CHUNK, D_MODEL).astype(BF16)
        return wg, wu, wd

    w = {}
    w["n1"] = ffn1_norm[i][None, :]
    w["wg1"], w["wu1"], w["wd1"] = ffn_w(ffn1_w_gu[i], ffn1_w_down[i])
    w["nmix"] = mix_norm[i][None, :]

    wi = w_in[i]
    o_gla = MLA_IN
    o_ssd = MLA_IN + GLA_IN
    o_mla_kr = MLA_Q_LORA + MLA_KV_LORA
    gla_qkv_w = 2 * GLA_QK + GLA_V
    cols = [
        wi[:, :o_mla_kr],
        _pad_cols(wi[:, o_mla_kr:MLA_IN], HEAD_PAD, at=MLA_NOPE),
        wi[:, o_gla:o_gla + gla_qkv_w],
        _pad_cols(wi[:, o_gla + gla_qkv_w + GLA_V:o_ssd], P_LR),
        wi[:, o_gla + gla_qkv_w:o_gla + gla_qkv_w + GLA_V],
        wi[:, o_ssd:o_ssd + SSD_INNER],
        wi[:, o_ssd + SSD_INNER:o_ssd + SSD_INNER + SSD_CONV_DIM],
        _pad_cols(wi[:, o_ssd + SSD_INNER + SSD_CONV_DIM:], P_DT),
    ]
    w["win"] = jnp.concatenate(cols, axis=1).astype(BF16)

    w["qn"] = mla_q_norm[i][None, :]
    w["kvn"] = mla_kv_norm[i][None, :]
    wuq = _pad_cols(mla_w_uq[i].reshape(MLA_Q_LORA, MLA_HEADS, MLA_QK), HEAD_PAD)
    w["wuqt"] = wuq.reshape(MLA_Q_LORA, MLA_HEADS * HEAD_PAD).T.astype(BF16)
    wukv = mla_w_ukv[i].reshape(MLA_KV_LORA, MLA_HEADS, MLA_NOPE + MLA_V)
    w["wk"] = _pad_cols(wukv[..., :MLA_NOPE], HEAD_PAD).reshape(MLA_KV_LORA, MLA_HEADS * HEAD_PAD).astype(BF16)
    w["wvt"] = wukv[..., MLA_NOPE:].reshape(MLA_KV_LORA, MLA_HEADS * MLA_V).T.astype(BF16)
    w["gq"] = (_pad_cols(mla_q_gain[i], HEAD_PAD) * (MLA_QK ** -0.5 * LOG2E))[:, None]
    w["gk"] = _pad_cols(mla_k_gain[i], HEAD_PAD)[None, :]

    wgate = jnp.zeros((2, LANES, GLA_QK), F32)
    for zdir in range(2):
        wgate = wgate.at[zdir, zdir * GLA_GATE_RANK:(zdir + 1) * GLA_GATE_RANK, :].set(gla_w_gate[i, zdir])
    w["wgh"] = wgate.astype(BF16)
    w["wgl"] = (wgate - w["wgh"].astype(F32)).astype(BF16)
    w["bg"] = gla_b_gate[i][:, None, :]
    w["onorm"] = jnp.tile(gla_o_norm[i], GLA_HEADS)[None, :]

    w["cw"] = jnp.zeros((SUBLANES, SSD_CONV_DIM), F32).at[:SSD_CONV].set(ssd_conv_w[i])
    w["cb"] = ssd_conv_b[i][None, :]
    flat_bias = ssd_dt_bias[i].reshape(2 * SSD_HEADS)
    flat_alog = ssd_a_log[i].reshape(2 * SSD_HEADS)
    w["bias_r"] = _pad_cols(flat_bias, P_DT)[None, :]
    w["bias_c"] = flat_bias[:, None]
    w["alog_r"] = _pad_cols(flat_alog, P_DT)[None, :]
    w["alog_c"] = flat_alog[:, None]
    w["dskip"] = jnp.repeat(ssd_d[i], SSD_HEADDIM)[None, :]
    w["snorm"] = ssd_norm[i][None, :]

    w["wout"] = w_out[i].astype(BF16)
    w["n2"] = ffn2_norm[i][None, :]
    w["wg2"], w["wu2"], w["wd2"] = ffn_w(ffn2_w_gu[i], ffn2_w_down[i])
    w["fn"] = final_norm[i][None, :]
    return w


def _block_ones(n, blk):
    idx = np.arange(n) // blk
    return jnp.asarray(idx[:, None] == idx[None, :], BF16)


def _seq_consts(l, tg):
    half = MLA_ROPE // 2
    pos = jnp.arange(l, dtype=F32)
    inv_freq = 1.0 / (ROPE_BASE ** (jnp.arange(0, MLA_ROPE, 2, dtype=F32) / MLA_ROPE))
    ang = pos[:, None] * inv_freq[None, :]
    cos, sin = jnp.cos(ang), jnp.sin(ang)
    c = {}
    c["cos_t"], c["sin_t"] = cos.T, sin.T
    ones = jnp.ones((l, HEAD_PAD), F32)
    c["c_n"] = ones.at[:, MLA_NOPE:MLA_NOPE + half].set(cos).at[:, MLA_NOPE + half:MLA_QK].set(cos)
    zeros = jnp.zeros((l, HEAD_PAD), F32)
    c["s1_n"] = zeros.at[:, MLA_NOPE:MLA_NOPE + half].set(-sin)
    c["s2_n"] = zeros.at[:, MLA_NOPE + half:MLA_QK].set(sin)

    r = np.arange(tg)
    same = (r[:, None] // GLA_CHUNK) == (r[None, :] // GLA_CHUNK)
    c["gla_cum"] = jnp.asarray(np.stack([same & (r[None, :] <= r[:, None]), same & (r[None, :] >= r[:, None])]), BF16)
    c["gla_tot"] = jnp.asarray(same, BF16)
    qk_head = np.arange(GLA_QK) // GLA_DK
    v_head = np.arange(GLA_V) // GLA_DV
    c["gla_ind"] = jnp.asarray(qk_head[:, None] == v_head[None, :], BF16)
    c["gla_hmask"] = jnp.asarray(v_head[:, None] == qk_head[None, :], F32)

    q = np.arange(SSD_CHUNK)
    c["ssd_tri"] = jnp.asarray(np.stack([q[None, :] <= q[:, None], q[None, :] >= q[:, None]]), BF16)
    return c


def _pick_tile(n, pref):
    t = min(n, pref)
    while n % t:
        t //= 2
    return t


def _layer(x, w, c, tg):
    b, l, _ = x.shape
    t = b * l
    tm = _pick_tile(t, 512)
    x1, p_mla, p_qkv, p_lr, p_g, p_z, p_xbc, p_dt = _ffn_in(
        x.reshape(t, D_MODEL), w["n1"], w["wg1"], w["wu1"], w["wd1"], w["nmix"], w["win"], tm)

    qt, k, vt = _mla_prep(p_mla.reshape(b, l, P_MLA), w["qn"], w["wuqt"], w["kvn"], w["wk"], w["wvt"],
                          w["gq"], w["gk"], c["cos_t"], c["sin_t"], c["c_n"], c["s1_n"], c["s2_n"],
                          _pick_tile(l, 512))
    o_mla = _mla_attn(qt, k, vt, _pick_tile(l, 256), _pick_tile(l, 512))

    o_gla = _gla_scan(p_qkv.reshape(b, l, P_QKV), p_lr.reshape(b, l, P_LR), w["wgh"], w["wgl"], w["bg"],
                      c["gla_cum"], c["gla_tot"], c["gla_ind"], c["gla_hmask"], tg)

    dt3 = p_dt.reshape(b, l, P_DT)
    dtt = jnp.swapaxes(dt3[:, :, :2 * SSD_HEADS], 1, 2)
    y_ssd = _ssd_scan(p_xbc.reshape(b, l, P_XBC), dt3, dtt, w["cw"], w["cb"], w["bias_r"], w["bias_c"],
                      w["alog_r"], w["alog_c"], w["dskip"], c["ssd_tri"])

    y = _ffn_out(x1, o_mla.reshape(t, MLA_HEADS * MLA_V), o_gla.reshape(2, t, GLA_V), p_g,
                 y_ssd.reshape(2, t, SSD_INNER), p_z, w["onorm"], w["snorm"],
                 _block_ones(GLA_V, GLA_DV), _block_ones(SSD_INNER, SSD_INNER // SSD_GROUPS),
                 w["wout"], w["n2"], w["wg2"], w["wu2"], w["wd2"], w["fn"], tm)
    return y.reshape(b, l, D_MODEL)


def kernel(x_prompt, x_sample, ffn1_norm, ffn1_w_gu, ffn1_w_down, mix_norm, w_in, mla_q_norm, mla_w_uq,
           mla_kv_norm, mla_w_ukv, mla_q_gain, mla_k_gain, gla_w_gate, gla_b_gate, gla_o_norm, ssd_conv_w,
           ssd_conv_b, ssd_a_log, ssd_dt_bias, ssd_d, ssd_norm, w_out, ffn2_norm, ffn2_w_gu, ffn2_w_down,
           final_norm):
    params = (ffn1_norm, ffn1_w_gu, ffn1_w_down, mix_norm, w_in, mla_q_norm, mla_w_uq, mla_kv_norm,
              mla_w_ukv, mla_q_gain, mla_k_gain, gla_w_gate, gla_b_gate, gla_o_norm, ssd_conv_w, ssd_conv_b,
              ssd_a_log, ssd_dt_bias, ssd_d, ssd_norm, w_out, ffn2_norm, ffn2_w_gu, ffn2_w_down, final_norm)
    depth = ffn1_norm.shape[0]
    streams = [x_prompt, x_sample]
    tgs = [_pick_tile(s.shape[1], 256) for s in streams]
    consts = [_seq_consts(s.shape[1], tg) for s, tg in zip(streams, tgs)]
    for i in range(depth):
        w = _prep_layer(i, *params)
        streams = [_layer(s, w, c, tg) for s, c, tg in zip(streams, consts, tgs)]
    return tuple(streams)
```

```python
import functools
import math

import jax
import jax.numpy as jnp
import numpy as np
from jax import lax
from jax.experimental import pallas as pl
from jax.experimental.pallas import tpu as pltpu

F32 = jnp.float32
BF16 = jnp.bfloat16

D_MODEL = 1024
D_FF = 2816
EPS = 1e-6
MLA_HEADS = 8
MLA_Q_LORA = 384
MLA_KV_LORA = 256
MLA_NOPE = 64
MLA_ROPE = 32
MLA_QK = MLA_NOPE + MLA_ROPE
MLA_V = 64
ROPE_BASE = 10000.0
GLA_HEADS = 4
GLA_DK = 32
GLA_DV = 64
GLA_GATE_RANK = 16
GLA_GATE_NORM = 16.0
GLA_CHUNK = 16
SSD_HEADS = 4
SSD_HEADDIM = 64
SSD_INNER = SSD_HEADS * SSD_HEADDIM
SSD_GROUPS = 2
SSD_STATE = 128
SSD_CONV = 5
SSD_CHUNK = 128
SSD_CONV_DIM = SSD_INNER + 2 * SSD_GROUPS * SSD_STATE
MLA_IN = MLA_Q_LORA + MLA_KV_LORA + MLA_ROPE
GLA_IN = 2 * GLA_HEADS * GLA_DK + 2 * GLA_HEADS * GLA_DV + 2 * GLA_GATE_RANK
SSD_IN = SSD_INNER + SSD_CONV_DIM + 2 * SSD_HEADS

LANES = 128
SUBLANES = 8
MXU_DIM = 256
VMEM_LIMIT = 56 * 1024 * 1024

HEAD_PAD = LANES
V_ROWS = MLA_V + 16
GLA_QK = GLA_HEADS * GLA_DK
GLA_V = GLA_HEADS * GLA_DV
FF_CHUNK = MXU_DIM
N_FF_CHUNKS = D_FF // FF_CHUNK

P_MLA = MLA_Q_LORA + MLA_KV_LORA + HEAD_PAD
P_QKV = 2 * GLA_QK + GLA_V
P_LR = LANES
P_G = GLA_V
P_Z = SSD_INNER
P_XBC = SSD_CONV_DIM
P_DT = LANES
P_TOTAL = P_MLA + P_QKV + P_LR + P_G + P_Z + P_XBC + P_DT

NEG_BIG = -1e30
LOG2E = 1.4426950408889634


def _rms(x, gain):
    return x * lax.rsqrt(jnp.mean(x * x, axis=-1, keepdims=True) + EPS) * gain


def _split_hi_lo(x):
    hi = x.astype(BF16)
    lo = (x - hi.astype(F32)).astype(BF16)
    return hi, lo


def _dot(a, b):
    return jnp.dot(a, b, preferred_element_type=F32)


def _dot_nt(a, b):
    return lax.dot_general(a, b, (((1,), (1,)), ((), ())), preferred_element_type=F32)


def _dot_tn(a, b):
    return lax.dot_general(a, b, (((0,), (0,)), ((), ())), preferred_element_type=F32)


def _cparams(sem):
    return pltpu.CompilerParams(dimension_semantics=sem, vmem_limit_bytes=VMEM_LIMIT)


def _const_spec(shape):
    nd = len(shape)
    return pl.BlockSpec(shape, lambda *_: (0,) * nd, pipeline_mode=pl.Buffered(1))


def _swiglu_into(h_ref, wg_ref, wu_ref, wd_ref, acc_ref):
    acc_ref[...] = jnp.zeros_like(acc_ref)

    def body(c, carry):
        h = h_ref[...]
        g = _dot(h, wg_ref[c])
        u = _dot(h, wu_ref[c])
        a = (g * jax.nn.sigmoid(g) * u).astype(BF16)
        acc_ref[...] += _dot(a, wd_ref[c])
        return carry

    lax.fori_loop(0, N_FF_CHUNKS, body, 0)


def _ffn_in_kernel(x_ref, n1_ref, wg_ref, wu_ref, wd_ref, n2_ref, win_ref,
                   x1_ref, mla_ref, qkv_ref, lr_ref, g_ref, z_ref, xbc_ref, dt_ref,
                   h_ref, acc_ref):
    x = x_ref[...]
    h_ref[...] = _rms(x, n1_ref[...]).astype(BF16)
    _swiglu_into(h_ref, wg_ref, wu_ref, wd_ref, acc_ref)
    x1 = x + 0.5 * acc_ref[...]
    x1_ref[...] = x1
    h2 = _rms(x1, n2_ref[...]).astype(BF16)
    off = 0
    for ref, width in ((mla_ref, P_MLA), (qkv_ref, P_QKV), (lr_ref, P_LR), (g_ref, P_G),
                       (z_ref, P_Z), (xbc_ref, P_XBC), (dt_ref, P_DT)):
        ref[...] = _dot(h2, win_ref[:, off:off + width])
        off += width


def _ffn_in(x, n1, wg, wu, wd, n2, win, tm):
    t = x.shape[0]
    widths = (D_MODEL, P_MLA, P_QKV, P_LR, P_G, P_Z, P_XBC, P_DT)
    tok = lambda w: pl.BlockSpec((tm, w), lambda i: (i, 0))
    return pl.pallas_call(
        _ffn_in_kernel,
        grid=(t // tm,),
        in_specs=[tok(D_MODEL), _const_spec(n1.shape), _const_spec(wg.shape), _const_spec(wu.shape),
                  _const_spec(wd.shape), _const_spec(n2.shape), _const_spec(win.shape)],
        out_specs=[tok(w) for w in widths],
        out_shape=[jax.ShapeDtypeStruct((t, w), F32) for w in widths],
        scratch_shapes=[pltpu.VMEM((tm, D_MODEL), BF16), pltpu.VMEM((tm, D_MODEL), F32)],
        compiler_params=_cparams(("parallel",)),
        name="ffn_in",
    )(x, n1, wg, wu, wd, n2, win)


def _ffn_out_kernel(x1_ref, omla_ref, ogla_ref, g_ref, yssd_ref, z_ref,
                    onorm_ref, snorm_ref, blk64_ref, blk128_ref, wout_ref,
                    n_ref, wg_ref, wu_ref, wd_ref, fn_ref,
                    y_ref, h_ref, acc_ref):
    og = ogla_ref[0] + ogla_ref[1]
    ss = _dot((og * og).astype(BF16), blk64_ref[...])
    g = g_ref[...]
    m_gla = og * lax.rsqrt(ss * (1.0 / GLA_DV) + EPS) * onorm_ref[...] * (g * jax.nn.sigmoid(g))
    z = z_ref[...]
    ys = (yssd_ref[0] + yssd_ref[1]) * (z * jax.nn.sigmoid(z))
    ss2 = _dot((ys * ys).astype(BF16), blk128_ref[...])
    m_ssd = ys * lax.rsqrt(ss2 * (1.0 / (SSD_INNER // SSD_GROUPS)) + EPS) * snorm_ref[...]
    m = jnp.concatenate([omla_ref[...], m_gla.astype(BF16), m_ssd.astype(BF16)], axis=-1)
    x2 = x1_ref[...] + _dot(m, wout_ref[...])
    h_ref[...] = _rms(x2, n_ref[...]).astype(BF16)
    _swiglu_into(h_ref, wg_ref, wu_ref, wd_ref, acc_ref)
    x3 = x2 + 0.5 * acc_ref[...]
    y_ref[...] = _rms(x3, fn_ref[...])


def _ffn_out(x1, omla, ogla, g, yssd, z, onorm, snorm, blk64, blk128, wout, n, wg, wu, wd, fn, tm):
    t = x1.shape[0]
    tok = lambda w: pl.BlockSpec((tm, w), lambda i: (i, 0))
    tok2 = lambda w: pl.BlockSpec((2, tm, w), lambda i: (0, i, 0))
    consts = (onorm, snorm, blk64, blk128, wout, n, wg, wu, wd, fn)
    return pl.pallas_call(
        _ffn_out_kernel,
        grid=(t // tm,),
        in_specs=[tok(D_MODEL), tok(MLA_HEADS * MLA_V), tok2(GLA_V), tok(GLA_V), tok2(SSD_INNER), tok(SSD_INNER)]
        + [_const_spec(c.shape) for c in consts],
        out_specs=tok(D_MODEL),
        out_shape=jax.ShapeDtypeStruct((t, D_MODEL), F32),
        scratch_shapes=[pltpu.VMEM((tm, D_MODEL), BF16), pltpu.VMEM((tm, D_MODEL), F32)],
        compiler_params=_cparams(("parallel",)),
        name="ffn_out",
    )(x1, omla, ogla, g, yssd, z, *consts)


def _mla_prep_kernel(p_ref, qn_ref, wuqt_ref, kvn_ref, wk_ref, wvt_ref, gq_ref, gk_ref,
                     cost_ref, sint_ref, cn_ref, s1_ref, s2_ref,
                     qt_ref, k_ref, vt_ref):
    p = p_ref[0]
    cq = p[:, :MLA_Q_LORA]
    ckv = p[:, MLA_Q_LORA:MLA_Q_LORA + MLA_KV_LORA]
    kr = p[:, MLA_Q_LORA + MLA_KV_LORA:]
    hq = _rms(cq, qn_ref[...]).astype(BF16)
    hkv = _rms(ckv, kvn_ref[...]).astype(BF16)

    qt = _dot_nt(wuqt_ref[...], hq)
    cos_t = cost_ref[...]
    sin_t = sint_ref[...]
    gq = gq_ref[...]
    half = MLA_ROPE // 2
    for h in range(MLA_HEADS):
        x = qt[h * HEAD_PAD:(h + 1) * HEAD_PAD]
        ss = jnp.sum(x * x, axis=0, keepdims=True)
        x = x * lax.rsqrt(ss * (1.0 / MLA_QK) + EPS) * gq
        x1 = x[MLA_NOPE:MLA_NOPE + half]
        x2 = x[MLA_NOPE + half:MLA_QK]
        qt_ref[0, h, 0:MLA_NOPE, :] = x[0:MLA_NOPE].astype(BF16)
        qt_ref[0, h, MLA_NOPE:MLA_NOPE + half, :] = (x1 * cos_t - x2 * sin_t).astype(BF16)
        qt_ref[0, h, MLA_NOPE + half:MLA_QK, :] = (x1 * sin_t + x2 * cos_t).astype(BF16)
        qt_ref[0, h, MLA_QK:HEAD_PAD, :] = x[MLA_QK:HEAD_PAD].astype(BF16)

    gk = gk_ref[...]
    krg = kr * gk
    k_rot = (krg * cn_ref[...] + pltpu.roll(krg, HEAD_PAD - half, axis=1) * s1_ref[...]
             + pltpu.roll(krg, half, axis=1) * s2_ref[...])
    ss_rope = jnp.sum(kr * kr, axis=-1, keepdims=True)
    kn = _dot(hkv, wk_ref[...])
    for h in range(MLA_HEADS):
        x = kn[:, h * HEAD_PAD:(h + 1) * HEAD_PAD]
        ss = jnp.sum(x * x, axis=-1, keepdims=True) + ss_rope
        k_ref[0, h] = ((x * gk + k_rot) * lax.rsqrt(ss * (1.0 / MLA_QK) + EPS)).astype(BF16)

    vt = _dot_nt(wvt_ref[...], hkv)
    ones_tile = jnp.where(lax.broadcasted_iota(jnp.int32, (V_ROWS - MLA_V, vt.shape[1]), 0) == 0, 1.0, 0.0).astype(BF16)
    for h in range(MLA_HEADS):
        vt_ref[0, h, 0:MLA_V, :] = vt[h * MLA_V:(h + 1) * MLA_V].astype(BF16)
        vt_ref[0, h, MLA_V:V_ROWS, :] = ones_tile


def _mla_prep(p_mla, qn, wuqt, kvn, wk, wvt, gq, gk, cos_t, sin_t, c_n, s1_n, s2_n, tm):
    b, l, _ = p_mla.shape
    consts = (qn, wuqt, kvn, wk, wvt, gq, gk)
    half = MLA_ROPE // 2
    return pl.pallas_call(
        _mla_prep_kernel,
        grid=(b, l // tm),
        in_specs=[pl.BlockSpec((1, tm, P_MLA), lambda i, j: (i, j, 0))]
        + [_const_spec(c.shape) for c in consts]
        + [pl.BlockSpec((half, tm), lambda i, j: (0, j)), pl.BlockSpec((half, tm), lambda i, j: (0, j)),
           pl.BlockSpec((tm, HEAD_PAD), lambda i, j: (j, 0)), pl.BlockSpec((tm, HEAD_PAD), lambda i, j: (j, 0)),
           pl.BlockSpec((tm, HEAD_PAD), lambda i, j: (j, 0))],
        out_specs=[pl.BlockSpec((1, MLA_HEADS, HEAD_PAD, tm), lambda i, j: (i, 0, 0, j)),
                   pl.BlockSpec((1, MLA_HEADS, tm, HEAD_PAD), lambda i, j: (i, 0, j, 0)),
                   pl.BlockSpec((1, MLA_HEADS, V_ROWS, tm), lambda i, j: (i, 0, 0, j))],
        out_shape=[jax.ShapeDtypeStruct((b, MLA_HEADS, HEAD_PAD, l), BF16),
                   jax.ShapeDtypeStruct((b, MLA_HEADS, l, HEAD_PAD), BF16),
                   jax.ShapeDtypeStruct((b, MLA_HEADS, V_ROWS, l), BF16)],
        compiler_params=_cparams(("parallel", "parallel")),
        name="mla_prep",
    )(p_mla, *consts, cos_t, sin_t, c_n, s1_n, s2_n)


ATTN_HEADS_PER_STEP = 2
ATTN_BLOCKS_PER_TRIP = 16
ATTN_LOOKAHEAD = 3


def _mla_attn_kernel(qt_ref, k_ref, vt_ref, o_ref, ot_ref, s_ref, *, tk, unroll, lookahead):
    n_kb = k_ref.shape[2] // tk
    tq = qt_ref.shape[3]
    n_slots = lookahead + 1
    for hh in range(ATTN_HEADS_PER_STEP):
        qt = qt_ref[0, hh]

        def scores(kb):
            start = pl.multiple_of(kb * tk, tk)
            return _dot(k_ref[0, hh, pl.ds(start, tk), :], qt)

        def body(i, carry):
            m, acc = carry
            for u in range(unroll):
                kb = i * unroll + u
                s_ref[(u + lookahead) % n_slots] = scores(jnp.minimum(kb + lookahead, n_kb - 1))
                s = s_ref[u % n_slots]
                m_new = jnp.maximum(m, jnp.max(s, axis=0, keepdims=True))
                alpha = jnp.exp2(m - m_new)
                p = jnp.exp2(s - m_new)
                start = pl.multiple_of(kb * tk, tk)
                acc = alpha * acc + _dot(vt_ref[0, hh, :, pl.ds(start, tk)], p.astype(BF16))
                m = m_new
            return m, acc

        for kb in range(lookahead):
            s_ref[kb] = scores(kb)
        m0 = jnp.full((1, tq), -jnp.inf, F32)
        a0 = jnp.zeros((V_ROWS, tq), F32)
        _, acc = lax.fori_loop(0, n_kb // unroll, body, (m0, a0))
        ot_ref[hh * MLA_V:(hh + 1) * MLA_V, :] = acc[0:MLA_V] * (1.0 / acc[MLA_V:MLA_V + 1])
    o_ref[0] = ot_ref[...].T.astype(BF16)


def _mla_attn(qt, k, vt, tq, tk):
    b, nh, _, l = qt.shape
    hp = ATTN_HEADS_PER_STEP
    n_kb = l // tk
    unroll = min(ATTN_BLOCKS_PER_TRIP, n_kb)
    lookahead = min(ATTN_LOOKAHEAD, unroll - 1)
    assert n_kb % unroll == 0 and unroll % (lookahead + 1) == 0
    return pl.pallas_call(
        functools.partial(_mla_attn_kernel, tk=tk, unroll=unroll, lookahead=lookahead),
        grid=(b, nh // hp, l // tq),
        in_specs=[pl.BlockSpec((1, hp, HEAD_PAD, tq), lambda i, h, j: (i, h, 0, j)),
                  pl.BlockSpec((1, hp, l, HEAD_PAD), lambda i, h, j: (i, h, 0, 0)),
                  pl.BlockSpec((1, hp, V_ROWS, l), lambda i, h, j: (i, h, 0, 0))],
        out_specs=pl.BlockSpec((1, tq, hp * MLA_V), lambda i, h, j: (i, j, h)),
        out_shape=jax.ShapeDtypeStruct((b, l, nh * MLA_V), BF16),
        scratch_shapes=[pltpu.VMEM((hp * MLA_V, tq), F32),
                        pltpu.VMEM((lookahead + 1, tk, tq), F32)],
        compiler_params=_cparams(("parallel", "parallel", "arbitrary")),
        name="mla_attn",
    )(qt, k, vt)


def _gla_kernel(qkv_ref, lr_ref, wgh_ref, wgl_ref, bg_ref, cum_ref, tot_ref, ind_ref, hmask_ref,
                o_ref, st_ref, dec_ref, upd_ref, stb_ref, *, tg):
    d = pl.program_id(1)
    t = pl.program_id(2)
    n_chunks = tg // GLA_CHUNK

    @pl.when(t == 0)
    def _():
        st_ref[...] = jnp.zeros_like(st_ref)

    qkv = qkv_ref[0]
    q = qkv[:, :GLA_QK] * (GLA_DK ** -0.5)
    k = qkv[:, GLA_QK:2 * GLA_QK]
    v = qkv[:, 2 * GLA_QK:]

    lr_hi, lr_lo = _split_hi_lo(lr_ref[0])
    zg = _dot(lr_hi, wgh_ref[0]) + _dot(lr_lo, wgh_ref[0]) + _dot(lr_hi, wgl_ref[0]) + bg_ref[0]
    log_a = (jnp.minimum(zg, 0.0) - jnp.log1p(jnp.exp(-jnp.abs(zg)))) * (1.0 / GLA_GATE_NORM)

    la_hi, la_lo = _split_hi_lo(log_a)
    bcum = _dot(cum_ref[0], la_hi) + _dot(cum_ref[0], la_lo)
    btot = _dot(tot_ref[...], la_hi) + _dot(tot_ref[...], la_lo)

    qd = (q * jnp.exp(bcum)).astype(BF16)
    kd = (k * jnp.exp(btot - bcum)).astype(BF16)
    vb = v.astype(BF16)
    dec_ref[...] = jnp.exp(btot)

    q3 = q.reshape(n_chunks, GLA_CHUNK, GLA_QK)
    k3 = k.reshape(n_chunks, GLA_CHUNK, GLA_QK)
    b3 = bcum.reshape(n_chunks, GLA_CHUNK, GLA_QK)
    v3 = v.reshape(n_chunks, GLA_CHUNK, GLA_V)
    fwd = d == 0
    tin = lax.broadcasted_iota(jnp.int32, (n_chunks, GLA_CHUNK, GLA_QK), 1)
    tin = jnp.where(fwd, tin, -tin)
    ind = ind_ref[...]
    o_intra = jnp.zeros((tg, GLA_V), F32)
    for j in range(GLA_CHUNK):
        valid = tin >= jnp.where(fwd, j, -j)
        e = jnp.exp(jnp.where(valid, b3 - b3[:, j:j + 1, :], NEG_BIG))
        pj = (q3 * k3[:, j:j + 1, :] * e).reshape(tg, GLA_QK).astype(BF16)
        aj = _dot(pj, ind).reshape(n_chunks, GLA_CHUNK, GLA_V)
        o_intra = o_intra + (aj * v3[:, j:j + 1, :]).reshape(tg, GLA_V)

    hmask = hmask_ref[...]
    chunk_rows = [slice(c * GLA_CHUNK, (c + 1) * GLA_CHUNK) for c in range(n_chunks)]
    for c, rows in enumerate(chunk_rows):
        upd_ref[c] = _dot_tn(vb[rows], kd[rows]) * hmask

    def body(ci, st):
        c = jnp.where(fwd, ci, n_chunks - 1 - ci)
        stb_ref[c] = st.astype(BF16)
        return st * dec_ref[pl.ds(c * GLA_CHUNK, 1), :] + upd_ref[c]

    st_ref[...] = lax.fori_loop(0, n_chunks, body, st_ref[...], unroll=True)
    o_inter = jnp.concatenate([_dot_nt(qd[rows], stb_ref[c]) for c, rows in enumerate(chunk_rows)], axis=0)
    o_ref[0, 0] = o_intra + o_inter


def _gla_scan(qkv, lr, wgh, wgl, bg, cum, tot, ind, hmask, tg):
    b, l, _ = qkv.shape
    nt = l // tg
    tile = lambda d, t: t + d * (nt - 1 - 2 * t)
    return pl.pallas_call(
        functools.partial(_gla_kernel, tg=tg),
        grid=(b, 2, nt),
        in_specs=[pl.BlockSpec((1, tg, P_QKV), lambda i, d, t: (i, tile(d, t), 0)),
                  pl.BlockSpec((1, tg, P_LR), lambda i, d, t: (i, tile(d, t), 0)),
                  pl.BlockSpec((1, LANES, GLA_QK), lambda i, d, t: (d, 0, 0)),
                  pl.BlockSpec((1, LANES, GLA_QK), lambda i, d, t: (d, 0, 0)),
                  pl.BlockSpec((1, 1, GLA_QK), lambda i, d, t: (d, 0, 0)),
                  pl.BlockSpec((1, tg, tg), lambda i, d, t: (d, 0, 0)),
                  _const_spec(tot.shape), _const_spec(ind.shape), _const_spec(hmask.shape)],
        out_specs=pl.BlockSpec((1, 1, tg, GLA_V), lambda i, d, t: (d, i, tile(d, t), 0)),
        out_shape=jax.ShapeDtypeStruct((2, b, l, GLA_V), F32),
        scratch_shapes=[pltpu.VMEM((GLA_V, GLA_QK), F32), pltpu.VMEM((tg, GLA_QK), F32),
                        pltpu.VMEM((tg // GLA_CHUNK, GLA_V, GLA_QK), F32),
                        pltpu.VMEM((tg // GLA_CHUNK, GLA_V, GLA_QK), BF16)],
        compiler_params=_cparams(("parallel", "parallel", "arbitrary")),
        name="gla_scan",
    )(qkv, lr, wgh, wgl, bg, cum, tot, ind, hmask)


CONV_HALO = SUBLANES


def _softplus(x):
    return jnp.maximum(x, 0.0) + jnp.log1p(jnp.exp(-jnp.abs(x)))


def _ssd_kernel(xc_ref, xp_ref, xn_ref, dt_ref, dtt_ref, cw_ref, cb_ref, bias_r_ref, bias_c_ref,
                alog_r_ref, alog_c_ref, dskip_ref, tri_ref,
                y_ref, xe_ref, st_ref, *, n_tiles):
    d = pl.program_id(1)
    t = pl.program_id(2)
    tt = t + d * (n_tiles - 1 - 2 * t)
    ts = SSD_CHUNK
    fwd = d == 0

    @pl.when(t == 0)
    def _():
        st_ref[...] = jnp.zeros_like(st_ref)

    xe_ref[0:CONV_HALO, :] = jnp.where(tt == 0, 0.0, xp_ref[0])
    xe_ref[CONV_HALO:CONV_HALO + ts, :] = xc_ref[0]
    xe_ref[CONV_HALO + ts:, :] = jnp.where(tt == n_tiles - 1, 0.0, xn_ref[0])
    pad = SSD_CONV // 2
    acc = jnp.zeros((ts, SSD_CONV_DIM), F32) + cb_ref[...]
    for kk in range(SSD_CONV):
        acc = acc + cw_ref[kk:kk + 1, :] * xe_ref[pl.ds(CONV_HALO - pad + kk, ts), :]
    act = acc * jax.nn.sigmoid(acc)
    xs = act[:, :SSD_INNER]
    bm = act[:, SSD_INNER:SSD_INNER + SSD_GROUPS * SSD_STATE]
    cm = act[:, SSD_INNER + SSD_GROUPS * SSD_STATE:]

    dt_c = _softplus(dt_ref[0] + bias_r_ref[...])
    la_c = dt_c * (-jnp.exp(alog_r_ref[...]))
    dt_r = _softplus(dtt_ref[0] + bias_c_ref[...])
    la_r = dt_r * (-jnp.exp(alog_c_ref[...]))
    tri_d = tri_ref[d]
    tri_o = tri_ref[1 - d]
    lc_hi, lc_lo = _split_hi_lo(la_c)
    cum_c = _dot(tri_d, lc_hi) + _dot(tri_d, lc_lo)
    lr_hi, lr_lo = _split_hi_lo(la_r)
    cum_r = _dot(lr_hi, tri_o) + _dot(lr_lo, tri_o)
    tot_c = jnp.sum(la_c, axis=0, keepdims=True)
    mask = tri_d.astype(F32) > 0.5

    ys = []
    for g in range(SSD_GROUPS):
        bg = bm[:, g * SSD_STATE:(g + 1) * SSD_STATE]
        cg = cm[:, g * SSD_STATE:(g + 1) * SSD_STATE]
        cb = _dot_nt(cg.astype(BF16), bg.astype(BF16))
        bgt = bg.T.astype(BF16)
        for hg in range(SSD_HEADS // SSD_GROUPS):
            h = g * (SSD_HEADS // SSD_GROUPS) + hg
            pick_c = lambda a: jnp.where(fwd, a[:, h:h + 1], a[:, SSD_HEADS + h:SSD_HEADS + h + 1])
            pick_r = lambda a: jnp.where(fwd, a[h:h + 1, :], a[SSD_HEADS + h:SSD_HEADS + h + 1, :])
            cc = pick_c(cum_c)
            cr = pick_r(cum_r)
            dth = pick_c(dt_c)
            tot = pick_c(tot_c)
            seg = jnp.exp(jnp.where(mask, cc - cr, NEG_BIG))
            xch = xs[:, h * SSD_HEADDIM:(h + 1) * SSD_HEADDIM] * dth
            st = st_ref[h]
            lhs = jnp.concatenate([(cb * seg).astype(BF16), (cg * jnp.exp(cc)).astype(BF16)], axis=1)
            rhs = jnp.concatenate([xch.astype(BF16), st.astype(BF16)], axis=0)
            ys.append(_dot(lhs, rhs))
            xdec = (xch * jnp.exp(tot - cc)).astype(BF16)
            st_ref[h] = st * jnp.exp(tot) + _dot(bgt, xdec)
    y = jnp.concatenate(ys, axis=1)
    y_ref[0, 0] = y + jnp.where(fwd, 1.0, 0.0) * (dskip_ref[...] * xs)


def _ssd_scan(xbc, dt, dtt, cw, cb, bias_r, bias_c, alog_r, alog_c, dskip, tri):
    b, l, _ = xbc.shape
    ts = SSD_CHUNK
    nt = l // ts
    hb = ts // CONV_HALO
    n_hb = l // CONV_HALO
    tile = lambda d, t: t + d * (nt - 1 - 2 * t)
    consts = (cw, cb, bias_r, bias_c, alog_r, alog_c, dskip, tri)
    return pl.pallas_call(
        functools.partial(_ssd_kernel, n_tiles=nt),
        grid=(b, 2, nt),
        in_specs=[pl.BlockSpec((1, ts, P_XBC), lambda i, d, t: (i, tile(d, t), 0)),
                  pl.BlockSpec((1, CONV_HALO, P_XBC), lambda i, d, t: (i, jnp.maximum(tile(d, t) * hb - 1, 0), 0)),
                  pl.BlockSpec((1, CONV_HALO, P_XBC),
                               lambda i, d, t: (i, jnp.minimum((tile(d, t) + 1) * hb, n_hb - 1), 0)),
                  pl.BlockSpec((1, ts, P_DT), lambda i, d, t: (i, tile(d, t), 0)),
                  pl.BlockSpec((1, 2 * SSD_HEADS, ts), lambda i, d, t: (i, 0, tile(d, t)))]
        + [_const_spec(c.shape) for c in consts],
        out_specs=pl.BlockSpec((1, 1, ts, SSD_INNER), lambda i, d, t: (d, i, tile(d, t), 0)),
        out_shape=jax.ShapeDtypeStruct((2, b, l, SSD_INNER), F32),
        scratch_shapes=[pltpu.VMEM((ts + 2 * CONV_HALO, P_XBC), F32),
                        pltpu.VMEM((SSD_HEADS, SSD_STATE, SSD_HEADDIM), F32)],
        compiler_params=_cparams(("parallel", "parallel", "arbitrary")),
        name="ssd_scan",
    )(xbc, xbc, xbc, dt, dtt, *consts)


def _pad_cols(w, width, at=0):
    out = jnp.zeros(w.shape[:-1] + (width,), w.dtype)
    return out.at[..., at:at + w.shape[-1]].set(w)


def _prep_layer(i, ffn1_norm, ffn1_w_gu, ffn1_w_down, mix_norm, w_in, mla_q_norm, mla_w_uq, mla_kv_norm,
                mla_w_ukv, mla_q_gain, mla_k_gain, gla_w_gate, gla_b_gate, gla_o_norm, ssd_conv_w, ssd_conv_b,
                ssd_a_log, ssd_dt_bias, ssd_d, ssd_norm, w_out, ffn2_norm, ffn2_w_gu, ffn2_w_down, final_norm):
    def ffn_w(w_gu, w_down):
        wg = w_gu[:, :D_FF].reshape(D_MODEL, N_FF_CHUNKS, FF_CHUNK).transpose(1, 0, 2).astype(BF16)
        wu = w_gu[:, D_FF:].reshape(D_MODEL, N_FF_CHUNKS, FF_CHUNK).transpose(1, 0, 2).astype(BF16)
        wd = w_down.reshape(N_FF_CHUNKS, FF_CHUNK, D_MODEL).astype(BF16)
        return wg, wu, wd

    w = {}
    w["n1"] = ffn1_norm[i][None, :]
    w["wg1"], w["wu1"], w["wd1"] = ffn_w(ffn1_w_gu[i], ffn1_w_down[i])
    w["nmix"] = mix_norm[i][None, :]

    wi = w_in[i]
    o_gla = MLA_IN
    o_ssd = MLA_IN + GLA_IN
    o_mla_kr = MLA_Q_LORA + MLA_KV_LORA
    gla_qkv_w = 2 * GLA_QK + GLA_V
    cols = [
        wi[:, :o_mla_kr],
        _pad_cols(wi[:, o_mla_kr:MLA_IN], HEAD_PAD, at=MLA_NOPE),
        wi[:, o_gla:o_gla + gla_qkv_w],
        _pad_cols(wi[:, o_gla + gla_qkv_w + GLA_V:o_ssd], P_LR),
        wi[:, o_gla + gla_qkv_w:o_gla + gla_qkv_w + GLA_V],
        wi[:, o_ssd:o_ssd + SSD_INNER],
        wi[:, o_ssd + SSD_INNER:o_ssd + SSD_INNER + SSD_CONV_DIM],
        _pad_cols(wi[:, o_ssd + SSD_INNER + SSD_CONV_DIM:], P_DT),
    ]
    w["win"] = jnp.concatenate(cols, axis=1).astype(BF16)

    w["qn"] = mla_q_norm[i][None, :]
    w["kvn"] = mla_kv_norm[i][None, :]
    wuq = _pad_cols(mla_w_uq[i].reshape(MLA_Q_LORA, MLA_HEADS, MLA_QK), HEAD_PAD)
    w["wuqt"] = wuq.reshape(MLA_Q_LORA, MLA_HEADS * HEAD_PAD).T.astype(BF16)
    wukv = mla_w_ukv[i].reshape(MLA_KV_LORA, MLA_HEADS, MLA_NOPE + MLA_V)
    w["wk"] = _pad_cols(wukv[..., :MLA_NOPE], HEAD_PAD).reshape(MLA_KV_LORA, MLA_HEADS * HEAD_PAD).astype(BF16)
    w["wvt"] = wukv[..., MLA_NOPE:].reshape(MLA_KV_LORA, MLA_HEADS * MLA_V).T.astype(BF16)
    w["gq"] = (_pad_cols(mla_q_gain[i], HEAD_PAD) * (MLA_QK ** -0.5 * LOG2E))[:, None]
    w["gk"] = _pad_cols(mla_k_gain[i], HEAD_PAD)[None, :]

    wgate = jnp.zeros((2, LANES, GLA_QK), F32)
    for zdir in range(2):
        wgate = wgate.at[zdir, zdir * GLA_GATE_RANK:(zdir + 1) * GLA_GATE_RANK, :].set(gla_w_gate[i, zdir])
    w["wgh"] = wgate.astype(BF16)
    w["wgl"] = (wgate - w["wgh"].astype(F32)).astype(BF16)
    w["bg"] = gla_b_gate[i][:, None, :]
    w["onorm"] = jnp.tile(gla_o_norm[i], GLA_HEADS)[None, :]

    w["cw"] = jnp.zeros((SUBLANES, SSD_CONV_DIM), F32).at[:SSD_CONV].set(ssd_conv_w[i])
    w["cb"] = ssd_conv_b[i][None, :]
    flat_bias = ssd_dt_bias[i].reshape(2 * SSD_HEADS)
    flat_alog = ssd_a_log[i].reshape(2 * SSD_HEADS)
    w["bias_r"] = _pad_cols(flat_bias, P_DT)[None, :]
    w["bias_c"] = flat_bias[:, None]
    w["alog_r"] = _pad_cols(flat_alog, P_DT)[None, :]
    w["alog_c"] = flat_alog[:, None]
    w["dskip"] = jnp.repeat(ssd_d[i], SSD_HEADDIM)[None, :]
    w["snorm"] = ssd_norm[i][None, :]

    w["wout"] = w_out[i].astype(BF16)
    w["n2"] = ffn2_norm[i][None, :]
    w["wg2"], w["wu2"], w["wd2"] = ffn_w(ffn2_w_gu[i], ffn2_w_down[i])
    w["fn"] = final_norm[i][None, :]
    return w


def _block_ones(n, blk):
    idx = np.arange(n) // blk
    return jnp.asarray(idx[:, None] == idx[None, :], BF16)


def _seq_consts(l, tg):
    half = MLA_ROPE // 2
    pos = jnp.arange(l, dtype=F32)
    inv_freq = 1.0 / (ROPE_BASE ** (jnp.arange(0, MLA_ROPE, 2, dtype=F32) / MLA_ROPE))
    ang = pos[:, None] * inv_freq[None, :]
    cos, sin = jnp.cos(ang), jnp.sin(ang)
    c = {}
    c["cos_t"], c["sin_t"] = cos.T, sin.T
    ones = jnp.ones((l, HEAD_PAD), F32)
    c["c_n"] = ones.at[:, MLA_NOPE:MLA_NOPE + half].set(cos).at[:, MLA_NOPE + half:MLA_QK].set(cos)
    zeros = jnp.zeros((l, HEAD_PAD), F32)
    c["s1_n"] = zeros.at[:, MLA_NOPE:MLA_NOPE + half].set(-sin)
    c["s2_n"] = zeros.at[:, MLA_NOPE + half:MLA_QK].set(sin)

    r = np.arange(tg)
    same = (r[:, None] // GLA_CHUNK) == (r[None, :] // GLA_CHUNK)
    c["gla_cum"] = jnp.asarray(np.stack([same & (r[None, :] <= r[:, None]), same & (r[None, :] >= r[:, None])]), BF16)
    c["gla_tot"] = jnp.asarray(same, BF16)
    qk_head = np.arange(GLA_QK) // GLA_DK
    v_head = np.arange(GLA_V) // GLA_DV
    c["gla_ind"] = jnp.asarray(qk_head[:, None] == v_head[None, :], BF16)
    c["gla_hmask"] = jnp.asarray(v_head[:, None] == qk_head[None, :], F32)

    q = np.arange(SSD_CHUNK)
    c["ssd_tri"] = jnp.asarray(np.stack([q[None, :] <= q[:, None], q[None, :] >= q[:, None]]), BF16)
    return c


def _pick_tile(n, pref):
    t = min(n, pref)
    while n % t:
        t //= 2
    return t


def _layer(x, w, c, tg):
    b, l, _ = x.shape
    t = b * l
    tm = _pick_tile(t, 512)
    x1, p_mla, p_qkv, p_lr, p_g, p_z, p_xbc, p_dt = _ffn_in(
        x.reshape(t, D_MODEL), w["n1"], w["wg1"], w["wu1"], w["wd1"], w["nmix"], w["win"], tm)

    qt, k, vt = _mla_prep(p_mla.reshape(b, l, P_MLA), w["qn"], w["wuqt"], w["kvn"], w["wk"], w["wvt"],
                          w["gq"], w["gk"], c["cos_t"], c["sin_t"], c["c_n"], c["s1_n"], c["s2_n"],
                          _pick_tile(l, 512))
    o_mla = _mla_attn(qt, k, vt, _pick_tile(l, 512), _pick_tile(l, 256))

    o_gla = _gla_scan(p_qkv.reshape(b, l, P_QKV), p_lr.reshape(b, l, P_LR), w["wgh"], w["wgl"], w["bg"],
                      c["gla_cum"], c["gla_tot"], c["gla_ind"], c["gla_hmask"], tg)

    dt3 = p_dt.reshape(b, l, P_DT)
    dtt = jnp.swapaxes(dt3[:, :, :2 * SSD_HEADS], 1, 2)
    y_ssd = _ssd_scan(p_xbc.reshape(b, l, P_XBC), dt3, dtt, w["cw"], w["cb"], w["bias_r"], w["bias_c"],
                      w["alog_r"], w["alog_c"], w["dskip"], c["ssd_tri"])

    y = _ffn_out(x1, o_mla.reshape(t, MLA_HEADS * MLA_V), o_gla.reshape(2, t, GLA_V), p_g,
                 y_ssd.reshape(2, t, SSD_INNER), p_z, w["onorm"], w["snorm"],
                 _block_ones(GLA_V, GLA_DV), _block_ones(SSD_INNER, SSD_INNER // SSD_GROUPS),
                 w["wout"], w["n2"], w["wg2"], w["wu2"], w["wd2"], w["fn"], tm)
    return y.reshape(b, l, D_MODEL)


def kernel(x_prompt, x_sample, ffn1_norm, ffn1_w_gu, ffn1_w_down, mix_norm, w_in, mla_q_norm, mla_w_uq,
           mla_kv_norm, mla_w_ukv, mla_q_gain, mla_k_gain, gla_w_gate, gla_b_gate, gla_o_norm, ssd_conv_w,
           ssd_conv_b, ssd_a_log, ssd_dt_bias, ssd_d, ssd_norm, w_out, ffn2_norm, ffn2_w_gu, ffn2_w_down,
           final_norm):
    params = (ffn1_norm, ffn1_w_gu, ffn1_w_down, mix_norm, w_in, mla_q_norm, mla_w_uq, mla_kv_norm,
              mla_w_ukv, mla_q_gain, mla_k_gain, gla_w_gate, gla_b_gate, gla_o_norm, ssd_conv_w, ssd_conv_b,
              ssd_a_log, ssd_dt_bias, ssd_d, ssd_norm, w_out, ffn2_norm, ffn2_w_gu, ffn2_w_down, final_norm)
    depth = ffn1_norm.shape[0]
    streams = [x_prompt, x_sample]
    tgs = [_pick_tile(s.shape[1], 256) for s in streams]
    consts = [_seq_consts(s.shape[1], tg) for s, tg in zip(streams, tgs)]
    for i in range(depth):
        w = _prep_layer(i, *params)
        streams = [_layer(s, w, c, tg) for s, c, tg in zip(streams, consts, tgs)]
    return tuple(streams)
```

```python
import functools
import math

import jax
import jax.numpy as jnp
import numpy as np
from jax import lax
from jax.experimental import pallas as pl
from jax.experimental.pallas import tpu as pltpu

F32 = jnp.float32
BF16 = jnp.bfloat16

D_MODEL = 1024
D_FF = 2816
EPS = 1e-6
MLA_HEADS = 8
MLA_Q_LORA = 384
MLA_KV_LORA = 256
MLA_NOPE = 64
MLA_ROPE = 32
MLA_QK = MLA_NOPE + MLA_ROPE
MLA_V = 64
ROPE_BASE = 10000.0
GLA_HEADS = 4
GLA_DK = 32
GLA_DV = 64
GLA_GATE_RANK = 16
GLA_GATE_NORM = 16.0
GLA_CHUNK = 16
SSD_HEADS = 4
SSD_HEADDIM = 64
SSD_INNER = SSD_HEADS * SSD_HEADDIM
SSD_GROUPS = 2
SSD_STATE = 128
SSD_CONV = 5
SSD_CHUNK = 128
SSD_CONV_DIM = SSD_INNER + 2 * SSD_GROUPS * SSD_STATE
MLA_IN = MLA_Q_LORA + MLA_KV_LORA + MLA_ROPE
GLA_IN = 2 * GLA_HEADS * GLA_DK + 2 * GLA_HEADS * GLA_DV + 2 * GLA_GATE_RANK
SSD_IN = SSD_INNER + SSD_CONV_DIM + 2 * SSD_HEADS

LANES = 128
SUBLANES = 8
MXU_DIM = 256
VMEM_LIMIT = 56 * 1024 * 1024

HEAD_PAD = LANES
V_ROWS = MLA_V + 16
GLA_QK = GLA_HEADS * GLA_DK
GLA_V = GLA_HEADS * GLA_DV
FF_CHUNK = MXU_DIM
N_FF_CHUNKS = D_FF // FF_CHUNK

P_MLA = MLA_Q_LORA + MLA_KV_LORA + HEAD_PAD
P_QKV = 2 * GLA_QK + GLA_V
P_LR = LANES
P_G = GLA_V
P_Z = SSD_INNER
P_XBC = SSD_CONV_DIM
P_DT = LANES
P_TOTAL = P_MLA + P_QKV + P_LR + P_G + P_Z + P_XBC + P_DT

NEG_BIG = -1e30
LOG2E = 1.4426950408889634


def _rms(x, gain):
    return x * lax.rsqrt(jnp.mean(x * x, axis=-1, keepdims=True) + EPS) * gain


def _split_hi_lo(x):
    hi = x.astype(BF16)
    lo = (x - hi.astype(F32)).astype(BF16)
    return hi, lo


def _dot(a, b):
    return jnp.dot(a, b, preferred_element_type=F32)


def _dot_nt(a, b):
    return lax.dot_general(a, b, (((1,), (1,)), ((), ())), preferred_element_type=F32)


def _dot_tn(a, b):
    return lax.dot_general(a, b, (((0,), (0,)), ((), ())), preferred_element_type=F32)


def _cparams(sem):
    return pltpu.CompilerParams(dimension_semantics=sem, vmem_limit_bytes=VMEM_LIMIT)


def _const_spec(shape):
    nd = len(shape)
    return pl.BlockSpec(shape, lambda *_: (0,) * nd, pipeline_mode=pl.Buffered(1))


def _swiglu_into(h_ref, wg_ref, wu_ref, wd_ref, acc_ref):
    acc_ref[...] = jnp.zeros_like(acc_ref)

    def body(c, carry):
        h = h_ref[...]
        g = _dot(h, wg_ref[c])
        u = _dot(h, wu_ref[c])
        a = (g * jax.nn.sigmoid(g) * u).astype(BF16)
        acc_ref[...] += _dot(a, wd_ref[c])
        return carry

    lax.fori_loop(0, N_FF_CHUNKS, body, 0)


def _ffn_in_kernel(x_ref, n1_ref, wg_ref, wu_ref, wd_ref, n2_ref, win_ref,
                   x1_ref, mla_ref, qkv_ref, lr_ref, g_ref, z_ref, xbc_ref, dt_ref,
                   h_ref, acc_ref):
    x = x_ref[...]
    h_ref[...] = _rms(x, n1_ref[...]).astype(BF16)
    _swiglu_into(h_ref, wg_ref, wu_ref, wd_ref, acc_ref)
    x1 = x + 0.5 * acc_ref[...]
    x1_ref[...] = x1
    h2 = _rms(x1, n2_ref[...]).astype(BF16)
    off = 0
    for ref, width in ((mla_ref, P_MLA), (qkv_ref, P_QKV), (lr_ref, P_LR), (g_ref, P_G),
                       (z_ref, P_Z), (xbc_ref, P_XBC), (dt_ref, P_DT)):
        ref[...] = _dot(h2, win_ref[:, off:off + width])
        off += width


def _ffn_in(x, n1, wg, wu, wd, n2, win, tm):
    t = x.shape[0]
    widths = (D_MODEL, P_MLA, P_QKV, P_LR, P_G, P_Z, P_XBC, P_DT)
    tok = lambda w: pl.BlockSpec((tm, w), lambda i: (i, 0))
    return pl.pallas_call(
        _ffn_in_kernel,
        grid=(t // tm,),
        in_specs=[tok(D_MODEL), _const_spec(n1.shape), _const_spec(wg.shape), _const_spec(wu.shape),
                  _const_spec(wd.shape), _const_spec(n2.shape), _const_spec(win.shape)],
        out_specs=[tok(w) for w in widths],
        out_shape=[jax.ShapeDtypeStruct((t, w), F32) for w in widths],
        scratch_shapes=[pltpu.VMEM((tm, D_MODEL), BF16), pltpu.VMEM((tm, D_MODEL), F32)],
        compiler_params=_cparams(("parallel",)),
        name="ffn_in",
    )(x, n1, wg, wu, wd, n2, win)


def _ffn_out_kernel(x1_ref, omla_ref, ogla_ref, g_ref, yssd_ref, z_ref,
                    onorm_ref, snorm_ref, blk64_ref, blk128_ref, wout_ref,
                    n_ref, wg_ref, wu_ref, wd_ref, fn_ref,
                    y_ref, h_ref, acc_ref):
    og = ogla_ref[0] + ogla_ref[1]
    ss = _dot((og * og).astype(BF16), blk64_ref[...])
    g = g_ref[...]
    m_gla = og * lax.rsqrt(ss * (1.0 / GLA_DV) + EPS) * onorm_ref[...] * (g * jax.nn.sigmoid(g))
    z = z_ref[...]
    ys = (yssd_ref[0] + yssd_ref[1]) * (z * jax.nn.sigmoid(z))
    ss2 = _dot((ys * ys).astype(BF16), blk128_ref[...])
    m_ssd = ys * lax.rsqrt(ss2 * (1.0 / (SSD_INNER // SSD_GROUPS)) + EPS) * snorm_ref[...]
    m = jnp.concatenate([omla_ref[...], m_gla.astype(BF16), m_ssd.astype(BF16)], axis=-1)
    x2 = x1_ref[...] + _dot(m, wout_ref[...])
    h_ref[...] = _rms(x2, n_ref[...]).astype(BF16)
    _swiglu_into(h_ref, wg_ref, wu_ref, wd_ref, acc_ref)
    x3 = x2 + 0.5 * acc_ref[...]
    y_ref[...] = _rms(x3, fn_ref[...])


def _ffn_out(x1, omla, ogla, g, yssd, z, onorm, snorm, blk64, blk128, wout, n, wg, wu, wd, fn, tm):
    t = x1.shape[0]
    tok = lambda w: pl.BlockSpec((tm, w), lambda i: (i, 0))
    tok2 = lambda w: pl.BlockSpec((2, tm, w), lambda i: (0, i, 0))
    consts = (onorm, snorm, blk64, blk128, wout, n, wg, wu, wd, fn)
    return pl.pallas_call(
        _ffn_out_kernel,
        grid=(t // tm,),
        in_specs=[tok(D_MODEL), tok(MLA_HEADS * MLA_V), tok2(GLA_V), tok(GLA_V), tok2(SSD_INNER), tok(SSD_INNER)]
        + [_const_spec(c.shape) for c in consts],
        out_specs=tok(D_MODEL),
        out_shape=jax.ShapeDtypeStruct((t, D_MODEL), F32),
        scratch_shapes=[pltpu.VMEM((tm, D_MODEL), BF16), pltpu.VMEM((tm, D_MODEL), F32)],
        compiler_params=_cparams(("parallel",)),
        name="ffn_out",
    )(x1, omla, ogla, g, yssd, z, *consts)


def _mla_prep_kernel(p_ref, qn_ref, wuqt_ref, kvn_ref, wk_ref, wvt_ref, gq_ref, gk_ref,
                     cost_ref, sint_ref, cn_ref, s1_ref, s2_ref,
                     qt_ref, k_ref, vt_ref):
    p = p_ref[0]
    cq = p[:, :MLA_Q_LORA]
    ckv = p[:, MLA_Q_LORA:MLA_Q_LORA + MLA_KV_LORA]
    kr = p[:, MLA_Q_LORA + MLA_KV_LORA:]
    hq = _rms(cq, qn_ref[...]).astype(BF16)
    hkv = _rms(ckv, kvn_ref[...]).astype(BF16)

    qt = _dot_nt(wuqt_ref[...], hq)
    cos_t = cost_ref[...]
    sin_t = sint_ref[...]
    gq = gq_ref[...]
    half = MLA_ROPE // 2
    for h in range(MLA_HEADS):
        x = qt[h * HEAD_PAD:(h + 1) * HEAD_PAD]
        ss = jnp.sum(x * x, axis=0, keepdims=True)
        x = x * lax.rsqrt(ss * (1.0 / MLA_QK) + EPS) * gq
        x1 = x[MLA_NOPE:MLA_NOPE + half]
        x2 = x[MLA_NOPE + half:MLA_QK]
        qt_ref[0, h, 0:MLA_NOPE, :] = x[0:MLA_NOPE].astype(BF16)
        qt_ref[0, h, MLA_NOPE:MLA_NOPE + half, :] = (x1 * cos_t - x2 * sin_t).astype(BF16)
        qt_ref[0, h, MLA_NOPE + half:MLA_QK, :] = (x1 * sin_t + x2 * cos_t).astype(BF16)
        qt_ref[0, h, MLA_QK:HEAD_PAD, :] = x[MLA_QK:HEAD_PAD].astype(BF16)

    gk = gk_ref[...]
    krg = kr * gk
    k_rot = (krg * cn_ref[...] + pltpu.roll(krg, HEAD_PAD - half, axis=1) * s1_ref[...]
             + pltpu.roll(krg, half, axis=1) * s2_ref[...])
    ss_rope = jnp.sum(kr * kr, axis=-1, keepdims=True)
    kn = _dot(hkv, wk_ref[...])
    for h in range(MLA_HEADS):
        x = kn[:, h * HEAD_PAD:(h + 1) * HEAD_PAD]
        ss = jnp.sum(x * x, axis=-1, keepdims=True) + ss_rope
        k_ref[0, h] = ((x * gk + k_rot) * lax.rsqrt(ss * (1.0 / MLA_QK) + EPS)).astype(BF16)

    vt = _dot_nt(wvt_ref[...], hkv)
    ones_tile = jnp.where(lax.broadcasted_iota(jnp.int32, (V_ROWS - MLA_V, vt.shape[1]), 0) == 0, 1.0, 0.0).astype(BF16)
    for h in range(MLA_HEADS):
        vt_ref[0, h, 0:MLA_V, :] = vt[h * MLA_V:(h + 1) * MLA_V].astype(BF16)
        vt_ref[0, h, MLA_V:V_ROWS, :] = ones_tile


def _mla_prep(p_mla, qn, wuqt, kvn, wk, wvt, gq, gk, cos_t, sin_t, c_n, s1_n, s2_n, tm):
    b, l, _ = p_mla.shape
    consts = (qn, wuqt, kvn, wk, wvt, gq, gk)
    half = MLA_ROPE // 2
    return pl.pallas_call(
        _mla_prep_kernel,
        grid=(b, l // tm),
        in_specs=[pl.BlockSpec((1, tm, P_MLA), lambda i, j: (i, j, 0))]
        + [_const_spec(c.shape) for c in consts]
        + [pl.BlockSpec((half, tm), lambda i, j: (0, j)), pl.BlockSpec((half, tm), lambda i, j: (0, j)),
           pl.BlockSpec((tm, HEAD_PAD), lambda i, j: (j, 0)), pl.BlockSpec((tm, HEAD_PAD), lambda i, j: (j, 0)),
           pl.BlockSpec((tm, HEAD_PAD), lambda i, j: (j, 0))],
        out_specs=[pl.BlockSpec((1, MLA_HEADS, HEAD_PAD, tm), lambda i, j: (i, 0, 0, j)),
                   pl.BlockSpec((1, MLA_HEADS, tm, HEAD_PAD), lambda i, j: (i, 0, j, 0)),
                   pl.BlockSpec((1, MLA_HEADS, V_ROWS, tm), lambda i, j: (i, 0, 0, j))],
        out_shape=[jax.ShapeDtypeStruct((b, MLA_HEADS, HEAD_PAD, l), BF16),
                   jax.ShapeDtypeStruct((b, MLA_HEADS, l, HEAD_PAD), BF16),
                   jax.ShapeDtypeStruct((b, MLA_HEADS, V_ROWS, l), BF16)],
        compiler_params=_cparams(("parallel", "parallel")),
        name="mla_prep",
    )(p_mla, *consts, cos_t, sin_t, c_n, s1_n, s2_n)


ATTN_HEADS_PER_STEP = 2
ATTN_BLOCKS_PER_TRIP = 16
ATTN_LOOKAHEAD = 3


def _mla_attn_kernel(qt_ref, k_ref, vt_ref, o_ref, ot_ref, s_ref, *, tk, unroll, lookahead):
    n_kb = k_ref.shape[2] // tk
    tq = qt_ref.shape[3]
    n_slots = lookahead + 1
    for hh in range(ATTN_HEADS_PER_STEP):
        qt = qt_ref[0, hh]

        def scores(kb):
            start = pl.multiple_of(kb * tk, tk)
            return _dot(k_ref[0, hh, pl.ds(start, tk), :], qt)

        def body(i, carry):
            m, acc = carry
            for u in range(unroll):
                kb = i * unroll + u
                s_ref[(u + lookahead) % n_slots] = scores(jnp.minimum(kb + lookahead, n_kb - 1))
                s = s_ref[u % n_slots]
                m_new = jnp.maximum(m, jnp.max(s, axis=0, keepdims=True))
                alpha = jnp.exp2(m - m_new)
                p = jnp.exp2(s - m_new)
                start = pl.multiple_of(kb * tk, tk)
                acc = alpha * acc + _dot(vt_ref[0, hh, :, pl.ds(start, tk)], p.astype(BF16))
                m = m_new
            return m, acc

        for kb in range(lookahead):
            s_ref[kb] = scores(kb)
        m0 = jnp.full((1, tq), -jnp.inf, F32)
        a0 = jnp.zeros((V_ROWS, tq), F32)
        _, acc = lax.fori_loop(0, n_kb // unroll, body, (m0, a0))
        ot_ref[hh * MLA_V:(hh + 1) * MLA_V, :] = acc[0:MLA_V] * (1.0 / acc[MLA_V:MLA_V + 1])
    o_ref[0] = ot_ref[...].T.astype(BF16)


def _mla_attn(qt, k, vt, tq, tk):
    b, nh, _, l = qt.shape
    hp = ATTN_HEADS_PER_STEP
    n_kb = l // tk
    unroll = min(ATTN_BLOCKS_PER_TRIP, n_kb)
    lookahead = min(ATTN_LOOKAHEAD, unroll - 1)
    assert n_kb % unroll == 0 and unroll % (lookahead + 1) == 0
    return pl.pallas_call(
        functools.partial(_mla_attn_kernel, tk=tk, unroll=unroll, lookahead=lookahead),
        grid=(b, nh // hp, l // tq),
        in_specs=[pl.BlockSpec((1, hp, HEAD_PAD, tq), lambda i, h, j: (i, h, 0, j)),
                  pl.BlockSpec((1, hp, l, HEAD_PAD), lambda i, h, j: (i, h, 0, 0)),
                  pl.BlockSpec((1, hp, V_ROWS, l), lambda i, h, j: (i, h, 0, 0))],
        out_specs=pl.BlockSpec((1, tq, hp * MLA_V), lambda i, h, j: (i, j, h)),
        out_shape=jax.ShapeDtypeStruct((b, l, nh * MLA_V), BF16),
        scratch_shapes=[pltpu.VMEM((hp * MLA_V, tq), F32),
                        pltpu.VMEM((lookahead + 1, tk, tq), F32)],
        compiler_params=_cparams(("parallel", "parallel", "arbitrary")),
        name="mla_attn",
    )(qt, k, vt)


def _gla_kernel(qkv_ref, lr_ref, wgh_ref, wgl_ref, bg_ref, cum_ref, tot_ref, ind_ref, hmask_ref,
                o_ref, st_ref, dec_ref, upd_ref, stb_ref, *, tg):
    d = pl.program_id(1)
    t = pl.program_id(2)
    n_chunks = tg // GLA_CHUNK

    @pl.when(t == 0)
    def _():
        st_ref[...] = jnp.zeros_like(st_ref)

    qkv = qkv_ref[0]
    q = qkv[:, :GLA_QK] * (GLA_DK ** -0.5)
    k = qkv[:, GLA_QK:2 * GLA_QK]
    v = qkv[:, 2 * GLA_QK:]

    lr_hi, lr_lo = _split_hi_lo(lr_ref[0])
    zg = _dot(lr_hi, wgh_ref[0]) + _dot(lr_lo, wgh_ref[0]) + _dot(lr_hi, wgl_ref[0]) + bg_ref[0]
    log_a = (jnp.minimum(zg, 0.0) - jnp.log1p(jnp.exp(-jnp.abs(zg)))) * (1.0 / GLA_GATE_NORM)

    la_hi, la_lo = _split_hi_lo(log_a)
    bcum = _dot(cum_ref[0], la_hi) + _dot(cum_ref[0], la_lo)
    btot = _dot(tot_ref[...], la_hi) + _dot(tot_ref[...], la_lo)

    qd = (q * jnp.exp(bcum)).astype(BF16)
    kd = (k * jnp.exp(btot - bcum)).astype(BF16)
    vb = v.astype(BF16)
    dec_ref[...] = jnp.exp(btot)

    q3 = q.reshape(n_chunks, GLA_CHUNK, GLA_QK)
    k3 = k.reshape(n_chunks, GLA_CHUNK, GLA_QK)
    b3 = bcum.reshape(n_chunks, GLA_CHUNK, GLA_QK)
    v3 = v.reshape(n_chunks, GLA_CHUNK, GLA_V)
    fwd = d == 0
    tin = lax.broadcasted_iota(jnp.int32, (n_chunks, GLA_CHUNK, GLA_QK), 1)
    tin = jnp.where(fwd, tin, -tin)
    ind = ind_ref[...]
    o_intra = jnp.zeros((tg, GLA_V), F32)
    for j in range(GLA_CHUNK):
        valid = tin >= jnp.where(fwd, j, -j)
        e = jnp.exp(jnp.where(valid, b3 - b3[:, j:j + 1, :], NEG_BIG))
        pj = (q3 * k3[:, j:j + 1, :] * e).reshape(tg, GLA_QK).astype(BF16)
        aj = _dot(pj, ind).reshape(n_chunks, GLA_CHUNK, GLA_V)
        o_intra = o_intra + (aj * v3[:, j:j + 1, :]).reshape(tg, GLA_V)

    hmask = hmask_ref[...]
    chunk_rows = [slice(c * GLA_CHUNK, (c + 1) * GLA_CHUNK) for c in range(n_chunks)]
    for c, rows in enumerate(chunk_rows):
        upd_ref[c] = _dot_tn(vb[rows], kd[rows]) * hmask

    def body(ci, st):
        c = jnp.where(fwd, ci, n_chunks - 1 - ci)
        stb_ref[c] = st.astype(BF16)
        return st * dec_ref[pl.ds(c * GLA_CHUNK, 1), :] + upd_ref[c]

    st_ref[...] = lax.fori_loop(0, n_chunks, body, st_ref[...], unroll=True)
    o_inter = jnp.concatenate([_dot_nt(qd[rows], stb_ref[c]) for c, rows in enumerate(chunk_rows)], axis=0)
    o_ref[0, 0] = o_intra + o_inter


def _gla_scan(qkv, lr, wgh, wgl, bg, cum, tot, ind, hmask, tg):
    b, l, _ = qkv.shape
    nt = l // tg
    tile = lambda d, t: t + d * (nt - 1 - 2 * t)
    return pl.pallas_call(
        functools.partial(_gla_kernel, tg=tg),
        grid=(b, 2, nt),
        in_specs=[pl.BlockSpec((1, tg, P_QKV), lambda i, d, t: (i, tile(d, t), 0)),
                  pl.BlockSpec((1, tg, P_LR), lambda i, d, t: (i, tile(d, t), 0)),
                  pl.BlockSpec((1, LANES, GLA_QK), lambda i, d, t: (d, 0, 0)),
                  pl.BlockSpec((1, LANES, GLA_QK), lambda i, d, t: (d, 0, 0)),
                  pl.BlockSpec((1, 1, GLA_QK), lambda i, d, t: (d, 0, 0)),
                  pl.BlockSpec((1, tg, tg), lambda i, d, t: (d, 0, 0)),
                  _const_spec(tot.shape), _const_spec(ind.shape), _const_spec(hmask.shape)],
        out_specs=pl.BlockSpec((1, 1, tg, GLA_V), lambda i, d, t: (d, i, tile(d, t), 0)),
        out_shape=jax.ShapeDtypeStruct((2, b, l, GLA_V), F32),
        scratch_shapes=[pltpu.VMEM((GLA_V, GLA_QK), F32), pltpu.VMEM((tg, GLA_QK), F32),
                        pltpu.VMEM((tg // GLA_CHUNK, GLA_V, GLA_QK), F32),
                        pltpu.VMEM((tg // GLA_CHUNK, GLA_V, GLA_QK), BF16)],
        compiler_params=_cparams(("parallel", "parallel", "arbitrary")),
        name="gla_scan",
    )(qkv, lr, wgh, wgl, bg, cum, tot, ind, hmask)


CONV_HALO = SUBLANES
SSD_TILE_CHUNKS = 4


def _softplus(x):
    return jnp.maximum(x, 0.0) + jnp.log1p(jnp.exp(-jnp.abs(x)))


def _ssd_kernel(xc_ref, xp_ref, xn_ref, dt_ref, dtt_ref, cw_ref, cb_ref, bias_r_ref, bias_c_ref,
                alog_r_ref, alog_c_ref, dskip_ref, tri_ref, shift_ref, rep_ref,
                y_ref, xe_ref, st_ref, upd_ref, dec_ref, stb_ref, cgb_ref, e64_ref, *, n_tiles, n_chunks):
    d = pl.program_id(1)
    t = pl.program_id(2)
    tt = t + d * (n_tiles - 1 - 2 * t)
    cs = SSD_CHUNK
    ts = n_chunks * cs
    fwd = d == 0
    hpg = SSD_HEADS // SSD_GROUPS
    gw = hpg * SSD_HEADDIM
    pad = SSD_CONV // 2

    @pl.when(t == 0)
    def _():
        st_ref[...] = jnp.zeros_like(st_ref)

    xe_ref[0:CONV_HALO, :] = jnp.where(tt == 0, 0.0, xp_ref[0])
    xe_ref[CONV_HALO:CONV_HALO + ts, :] = xc_ref[0]
    xe_ref[CONV_HALO + ts:, :] = jnp.where(tt == n_tiles - 1, 0.0, xn_ref[0])

    tri_d = tri_ref[d]
    tri_o = tri_ref[1 - d]
    mask = tri_d.astype(F32) > 0.5
    rep64 = rep_ref[0]
    neg_a_r = -jnp.exp(alog_r_ref[...])
    neg_a_c = -jnp.exp(alog_c_ref[...])
    lane_head = lax.broadcasted_iota(jnp.int32, (cs, gw), 1) // SSD_HEADDIM
    skip_gain = jnp.where(fwd, 1.0, 0.0) * dskip_ref[...]

    for c in range(n_chunks):
        rows = slice(c * cs, (c + 1) * cs)
        xe_b = xe_ref[c * cs:(c + 1) * cs + 2 * CONV_HALO, :].astype(BF16)
        acc = cb_ref[...] + cw_ref[pad:pad + 1, :] * xe_ref[CONV_HALO + c * cs:CONV_HALO + (c + 1) * cs, :]
        for i, kk in enumerate(k for k in range(SSD_CONV) if k != pad):
            acc = acc + cw_ref[kk:kk + 1, :] * _dot(shift_ref[i], xe_b)
        act = acc * jax.nn.sigmoid(acc)
        xs = act[:, :SSD_INNER]
        bm = act[:, SSD_INNER:SSD_INNER + SSD_GROUPS * SSD_STATE]
        cmb = act[:, SSD_INNER + SSD_GROUPS * SSD_STATE:].astype(BF16)

        dt_c = _softplus(dt_ref[0, rows, :] + bias_r_ref[...])
        la_c = dt_c * neg_a_r
        dt_r = _softplus(dtt_ref[0, :, rows] + bias_c_ref[...])
        la_r = dt_r * neg_a_c
        lc_hi, lc_lo = _split_hi_lo(la_c)
        cum_c = _dot(tri_d, lc_hi) + _dot(tri_d, lc_lo)
        lr_hi, lr_lo = _split_hi_lo(la_r)
        cum_r = _dot(lr_hi, tri_o) + _dot(lr_lo, tri_o)
        dt_hi, dt_lo = _split_hi_lo(dt_c)
        dt64 = _dot(dt_hi, rep64) + _dot(dt_lo, rep64)
        cu_hi, cu_lo = _split_hi_lo(cum_c)
        cum64 = _dot(cu_hi, rep64) + _dot(cu_lo, rep64)
        tot64 = jnp.where(fwd, cum64[cs - 1:cs, :], cum64[0:1, :])

        xc = xs * dt64
        xcb = xc.astype(BF16)
        xdec = (xc * jnp.exp(tot64 - cum64)).astype(BF16)
        e64_ref[rows, :] = jnp.exp(cum64)
        dec_ref[c] = jnp.exp(tot64)
        cgb_ref[rows, :] = cmb
        ys = []
        for g in range(SSD_GROUPS):
            gl = slice(g * gw, (g + 1) * gw)
            bg = bm[:, g * SSD_STATE:(g + 1) * SSD_STATE]
            cb = _dot_nt(cmb[:, g * SSD_STATE:(g + 1) * SSD_STATE], bg.astype(BF16))
            y_g = jnp.zeros((cs, gw), F32)
            cum_g = cum64[:, gl]
            cum_swapped = pltpu.roll(cum_g, SSD_HEADDIM, axis=1)
            for hg in range(hpg):
                h = g * hpg + hg
                cr = jnp.where(fwd, cum_r[h:h + 1, :], cum_r[SSD_HEADS + h:SSD_HEADS + h + 1, :])
                cum_h = jnp.where(lane_head == hg, cum_g, cum_swapped)
                seg = jnp.exp(jnp.where(mask, cum_h - cr, NEG_BIG))
                yd = _dot((cb * seg).astype(BF16), xcb[:, gl])
                y_g = y_g + jnp.where(lane_head == hg, yd, 0.0)
            ys.append(y_g)
            upd_ref[c, g] = _dot(bg.T.astype(BF16), xdec[:, gl])
        y_ref[0, 0, rows, :] = jnp.concatenate(ys, axis=1) + skip_gain * xs

    def body(ci, st):
        c = jnp.where(fwd, ci, n_chunks - 1 - ci)
        dec = dec_ref[c]
        new = []
        for g in range(SSD_GROUPS):
            stb_ref[c, g] = st[g].astype(BF16)
            new.append(st[g] * dec[:, g * gw:(g + 1) * gw] + upd_ref[c, g])
        return tuple(new)

    st = lax.fori_loop(0, n_chunks, body, tuple(st_ref[g] for g in range(SSD_GROUPS)), unroll=True)
    for g in range(SSD_GROUPS):
        st_ref[g] = st[g]

    for c in range(n_chunks):
        rows = slice(c * cs, (c + 1) * cs)
        y_off = jnp.concatenate(
            [_dot(cgb_ref[rows, g * SSD_STATE:(g + 1) * SSD_STATE], stb_ref[c, g]) for g in range(SSD_GROUPS)], axis=1)
        y_ref[0, 0, rows, :] = y_ref[0, 0, rows, :] + e64_ref[rows, :] * y_off


def _ssd_scan(xbc, dt, dtt, cw, cb, bias_r, bias_c, alog_r, alog_c, dskip, tri, shift, rep, n_chunks):
    b, l, _ = xbc.shape
    ts = n_chunks * SSD_CHUNK
    nt = l // ts
    hb = ts // CONV_HALO
    n_hb = l // CONV_HALO
    gw = SSD_INNER // SSD_GROUPS
    tile = lambda d, t: t + d * (nt - 1 - 2 * t)
    consts = (cw, cb, bias_r, bias_c, alog_r, alog_c, dskip, tri, shift)
    return pl.pallas_call(
        functools.partial(_ssd_kernel, n_tiles=nt, n_chunks=n_chunks),
        grid=(b, 2, nt),
        in_specs=[pl.BlockSpec((1, ts, P_XBC), lambda i, d, t: (i, tile(d, t), 0)),
                  pl.BlockSpec((1, CONV_HALO, P_XBC), lambda i, d, t: (i, jnp.maximum(tile(d, t) * hb - 1, 0), 0)),
                  pl.BlockSpec((1, CONV_HALO, P_XBC),
                               lambda i, d, t: (i, jnp.minimum((tile(d, t) + 1) * hb, n_hb - 1), 0)),
                  pl.BlockSpec((1, ts, P_DT), lambda i, d, t: (i, tile(d, t), 0)),
                  pl.BlockSpec((1, 2 * SSD_HEADS, ts), lambda i, d, t: (i, 0, tile(d, t)))]
        + [_const_spec(c.shape) for c in consts]
        + [pl.BlockSpec((1,) + rep.shape[1:], lambda i, d, t: (d, 0, 0))],
        out_specs=pl.BlockSpec((1, 1, ts, SSD_INNER), lambda i, d, t: (d, i, tile(d, t), 0)),
        out_shape=jax.ShapeDtypeStruct((2, b, l, SSD_INNER), F32),
        scratch_shapes=[pltpu.VMEM((ts + 2 * CONV_HALO, P_XBC), F32),
                        pltpu.VMEM((SSD_GROUPS, SSD_STATE, gw), F32),
                        pltpu.VMEM((n_chunks, SSD_GROUPS, SSD_STATE, gw), F32),
                        pltpu.VMEM((n_chunks, 1, SSD_INNER), F32),
                        pltpu.VMEM((n_chunks, SSD_GROUPS, SSD_STATE, gw), BF16),
                        pltpu.VMEM((ts, SSD_GROUPS * SSD_STATE), BF16),
                        pltpu.VMEM((ts, SSD_INNER), F32)],
        compiler_params=_cparams(("parallel", "parallel", "arbitrary")),
        name="ssd_scan",
    )(xbc, xbc, xbc, dt, dtt, *consts, rep)


def _pad_cols(w, width, at=0):
    out = jnp.zeros(w.shape[:-1] + (width,), w.dtype)
    return out.at[..., at:at + w.shape[-1]].set(w)


def _prep_layer(i, ffn1_norm, ffn1_w_gu, ffn1_w_down, mix_norm, w_in, mla_q_norm, mla_w_uq, mla_kv_norm,
                mla_w_ukv, mla_q_gain, mla_k_gain, gla_w_gate, gla_b_gate, gla_o_norm, ssd_conv_w, ssd_conv_b,
                ssd_a_log, ssd_dt_bias, ssd_d, ssd_norm, w_out, ffn2_norm, ffn2_w_gu, ffn2_w_down, final_norm):
    def ffn_w(w_gu, w_down):
        wg = w_gu[:, :D_FF].reshape(D_MODEL, N_FF_CHUNKS, FF_CHUNK).transpose(1, 0, 2).astype(BF16)
        wu = w_gu[:, D_FF:].reshape(D_MODEL, N_FF_CHUNKS, FF_CHUNK).transpose(1, 0, 2).astype(BF16)
        wd = w_down.reshape(N_FF_CHUNKS, FF_CHUNK, D_MODEL).astype(BF16)
        return wg, wu, wd

    w = {}
    w["n1"] = ffn1_norm[i][None, :]
    w["wg1"], w["wu1"], w["wd1"] = ffn_w(ffn1_w_gu[i], ffn1_w_down[i])
    w["nmix"] = mix_norm[i][None, :]

    wi = w_in[i]
    o_gla = MLA_IN
    o_ssd = MLA_IN + GLA_IN
    o_mla_kr = MLA_Q_LORA + MLA_KV_LORA
    gla_qkv_w = 2 * GLA_QK + GLA_V
    cols = [
        wi[:, :o_mla_kr],
        _pad_cols(wi[:, o_mla_kr:MLA_IN], HEAD_PAD, at=MLA_NOPE),
        wi[:, o_gla:o_gla + gla_qkv_w],
        _pad_cols(wi[:, o_gla + gla_qkv_w + GLA_V:o_ssd], P_LR),
        wi[:, o_gla + gla_qkv_w:o_gla + gla_qkv_w + GLA_V],
        wi[:, o_ssd:o_ssd + SSD_INNER],
        wi[:, o_ssd + SSD_INNER:o_ssd + SSD_INNER + SSD_CONV_DIM],
        _pad_cols(wi[:, o_ssd + SSD_INNER + SSD_CONV_DIM:], P_DT),
    ]
    w["win"] = jnp.concatenate(cols, axis=1).astype(BF16)

    w["qn"] = mla_q_norm[i][None, :]
    w["kvn"] = mla_kv_norm[i][None, :]
    wuq = _pad_cols(mla_w_uq[i].reshape(MLA_Q_LORA, MLA_HEADS, MLA_QK), HEAD_PAD)
    w["wuqt"] = wuq.reshape(MLA_Q_LORA, MLA_HEADS * HEAD_PAD).T.astype(BF16)
    wukv = mla_w_ukv[i].reshape(MLA_KV_LORA, MLA_HEADS, MLA_NOPE + MLA_V)
    w["wk"] = _pad_cols(wukv[..., :MLA_NOPE], HEAD_PAD).reshape(MLA_KV_LORA, MLA_HEADS * HEAD_PAD).astype(BF16)
    w["wvt"] = wukv[..., MLA_NOPE:].reshape(MLA_KV_LORA, MLA_HEADS * MLA_V).T.astype(BF16)
    w["gq"] = (_pad_cols(mla_q_gain[i], HEAD_PAD) * (MLA_QK ** -0.5 * LOG2E))[:, None]
    w["gk"] = _pad_cols(mla_k_gain[i], HEAD_PAD)[None, :]

    wgate = jnp.zeros((2, LANES, GLA_QK), F32)
    for zdir in range(2):
        wgate = wgate.at[zdir, zdir * GLA_GATE_RANK:(zdir + 1) * GLA_GATE_RANK, :].set(gla_w_gate[i, zdir])
    w["wgh"] = wgate.astype(BF16)
    w["wgl"] = (wgate - w["wgh"].astype(F32)).astype(BF16)
    w["bg"] = gla_b_gate[i][:, None, :]
    w["onorm"] = jnp.tile(gla_o_norm[i], GLA_HEADS)[None, :]

    w["cw"] = jnp.zeros((SUBLANES, SSD_CONV_DIM), F32).at[:SSD_CONV].set(ssd_conv_w[i])
    w["cb"] = ssd_conv_b[i][None, :]
    flat_bias = ssd_dt_bias[i].reshape(2 * SSD_HEADS)
    flat_alog = ssd_a_log[i].reshape(2 * SSD_HEADS)
    w["bias_r"] = _pad_cols(flat_bias, P_DT)[None, :]
    w["bias_c"] = flat_bias[:, None]
    w["alog_r"] = _pad_cols(flat_alog, P_DT)[None, :]
    w["alog_c"] = flat_alog[:, None]
    w["dskip"] = jnp.repeat(ssd_d[i], SSD_HEADDIM)[None, :]
    w["snorm"] = ssd_norm[i][None, :]

    w["wout"] = w_out[i].astype(BF16)
    w["n2"] = ffn2_norm[i][None, :]
    w["wg2"], w["wu2"], w["wd2"] = ffn_w(ffn2_w_gu[i], ffn2_w_down[i])
    w["fn"] = final_norm[i][None, :]
    return w


def _block_ones(n, blk):
    idx = np.arange(n) // blk
    return jnp.asarray(idx[:, None] == idx[None, :], BF16)


def _seq_consts(l, tg):
    half = MLA_ROPE // 2
    pos = jnp.arange(l, dtype=F32)
    inv_freq = 1.0 / (ROPE_BASE ** (jnp.arange(0, MLA_ROPE, 2, dtype=F32) / MLA_ROPE))
    ang = pos[:, None] * inv_freq[None, :]
    cos, sin = jnp.cos(ang), jnp.sin(ang)
    c = {}
    c["cos_t"], c["sin_t"] = cos.T, sin.T
    ones = jnp.ones((l, HEAD_PAD), F32)
    c["c_n"] = ones.at[:, MLA_NOPE:MLA_NOPE + half].set(cos).at[:, MLA_NOPE + half:MLA_QK].set(cos)
    zeros = jnp.zeros((l, HEAD_PAD), F32)
    c["s1_n"] = zeros.at[:, MLA_NOPE:MLA_NOPE + half].set(-sin)
    c["s2_n"] = zeros.at[:, MLA_NOPE + half:MLA_QK].set(sin)

    r = np.arange(tg)
    same = (r[:, None] // GLA_CHUNK) == (r[None, :] // GLA_CHUNK)
    c["gla_cum"] = jnp.asarray(np.stack([same & (r[None, :] <= r[:, None]), same & (r[None, :] >= r[:, None])]), BF16)
    c["gla_tot"] = jnp.asarray(same, BF16)
    qk_head = np.arange(GLA_QK) // GLA_DK
    v_head = np.arange(GLA_V) // GLA_DV
    c["gla_ind"] = jnp.asarray(qk_head[:, None] == v_head[None, :], BF16)
    c["gla_hmask"] = jnp.asarray(v_head[:, None] == qk_head[None, :], F32)

    q = np.arange(SSD_CHUNK)
    c["ssd_tri"] = jnp.asarray(np.stack([q[None, :] <= q[:, None], q[None, :] >= q[:, None]]), BF16)
    xe_row = np.arange(SSD_CHUNK + 2 * CONV_HALO)
    offs = [k - SSD_CONV // 2 for k in range(SSD_CONV) if k != SSD_CONV // 2]
    c["ssd_shift"] = jnp.asarray(np.stack([xe_row[None, :] == (q[:, None] + CONV_HALO + o) for o in offs]), BF16)
    src_lane = np.arange(P_DT)
    c["ssd_rep"] = jnp.asarray(np.stack(
        [src_lane[:, None] == (zdir * SSD_HEADS + np.arange(SSD_INNER) // SSD_HEADDIM)[None, :] for zdir in range(2)]), BF16)
    return c


def _pick_tile(n, pref):
    t = min(n, pref)
    while n % t:
        t //= 2
    return t


def _layer(x, w, c, tg):
    b, l, _ = x.shape
    t = b * l
    tm = _pick_tile(t, 512)
    x1, p_mla, p_qkv, p_lr, p_g, p_z, p_xbc, p_dt = _ffn_in(
        x.reshape(t, D_MODEL), w["n1"], w["wg1"], w["wu1"], w["wd1"], w["nmix"], w["win"], tm)

    qt, k, vt = _mla_prep(p_mla.reshape(b, l, P_MLA), w["qn"], w["wuqt"], w["kvn"], w["wk"], w["wvt"],
                          w["gq"], w["gk"], c["cos_t"], c["sin_t"], c["c_n"], c["s1_n"], c["s2_n"],
                          _pick_tile(l, 512))
    o_mla = _mla_attn(qt, k, vt, _pick_tile(l, 512), _pick_tile(l, 256))

    o_gla = _gla_scan(p_qkv.reshape(b, l, P_QKV), p_lr.reshape(b, l, P_LR), w["wgh"], w["wgl"], w["bg"],
                      c["gla_cum"], c["gla_tot"], c["gla_ind"], c["gla_hmask"], tg)

    dt3 = p_dt.reshape(b, l, P_DT)
    dtt = jnp.swapaxes(dt3[:, :, :2 * SSD_HEADS], 1, 2)
    y_ssd = _ssd_scan(p_xbc.reshape(b, l, P_XBC), dt3, dtt, w["cw"], w["cb"], w["bias_r"], w["bias_c"],
                      w["alog_r"], w["alog_c"], w["dskip"], c["ssd_tri"], c["ssd_shift"], c["ssd_rep"],
                      _pick_tile(l // SSD_CHUNK, SSD_TILE_CHUNKS))

    y = _ffn_out(x1, o_mla.reshape(t, MLA_HEADS * MLA_V), o_gla.reshape(2, t, GLA_V), p_g,
                 y_ssd.reshape(2, t, SSD_INNER), p_z, w["onorm"], w["snorm"],
                 _block_ones(GLA_V, GLA_DV), _block_ones(SSD_INNER, SSD_INNER // SSD_GROUPS),
                 w["wout"], w["n2"], w["wg2"], w["wu2"], w["wd2"], w["fn"], tm)
    return y.reshape(b, l, D_MODEL)


def kernel(x_prompt, x_sample, ffn1_norm, ffn1_w_gu, ffn1_w_down, mix_norm, w_in, mla_q_norm, mla_w_uq,
           mla_kv_norm, mla_w_ukv, mla_q_gain, mla_k_gain, gla_w_gate, gla_b_gate, gla_o_norm, ssd_conv_w,
           ssd_conv_b, ssd_a_log, ssd_dt_bias, ssd_d, ssd_norm, w_out, ffn2_norm, ffn2_w_gu, ffn2_w_down,
           final_norm):
    params = (ffn1_norm, ffn1_w_gu, ffn1_w_down, mix_norm, w_in, mla_q_norm, mla_w_uq, mla_kv_norm,
              mla_w_ukv, mla_q_gain, mla_k_gain, gla_w_gate, gla_b_gate, gla_o_norm, ssd_conv_w, ssd_conv_b,
              ssd_a_log, ssd_dt_bias, ssd_d, ssd_norm, w_out, ffn2_norm, ffn2_w_gu, ffn2_w_down, final_norm)
    depth = ffn1_norm.shape[0]
    streams = [x_prompt, x_sample]
    tgs = [_pick_tile(s.shape[1], 256) for s in streams]
    consts = [_seq_consts(s.shape[1], tg) for s, tg in zip(streams, tgs)]
    for i in range(depth):
        w = _prep_layer(i, *params)
        streams = [_layer(s, w, c, tg) for s, c, tg in zip(streams, consts, tgs)]
    return tuple(streams)
```

```python
import functools
import math

import jax
import jax.numpy as jnp
import numpy as np
from jax import lax
from jax.experimental import pallas as pl
from jax.experimental.pallas import tpu as pltpu

F32 = jnp.float32
BF16 = jnp.bfloat16

D_MODEL = 1024
D_FF = 2816
EPS = 1e-6
MLA_HEADS = 8
MLA_Q_LORA = 384
MLA_KV_LORA = 256
MLA_NOPE = 64
MLA_ROPE = 32
MLA_QK = MLA_NOPE + MLA_ROPE
MLA_V = 64
ROPE_BASE = 10000.0
GLA_HEADS = 4
GLA_DK = 32
GLA_DV = 64
GLA_GATE_RANK = 16
GLA_GATE_NORM = 16.0
GLA_CHUNK = 16
SSD_HEADS = 4
SSD_HEADDIM = 64
SSD_INNER = SSD_HEADS * SSD_HEADDIM
SSD_GROUPS = 2
SSD_STATE = 128
SSD_CONV = 5
SSD_CHUNK = 128
SSD_CONV_DIM = SSD_INNER + 2 * SSD_GROUPS * SSD_STATE
MLA_IN = MLA_Q_LORA + MLA_KV_LORA + MLA_ROPE
GLA_IN = 2 * GLA_HEADS * GLA_DK + 2 * GLA_HEADS * GLA_DV + 2 * GLA_GATE_RANK
SSD_IN = SSD_INNER + SSD_CONV_DIM + 2 * SSD_HEADS

LANES = 128
SUBLANES = 8
MXU_DIM = 256
VMEM_LIMIT = 56 * 1024 * 1024

HEAD_PAD = LANES
V_ROWS = MLA_V + 16
GLA_QK = GLA_HEADS * GLA_DK
GLA_V = GLA_HEADS * GLA_DV
FF_CHUNK = MXU_DIM
N_FF_CHUNKS = D_FF // FF_CHUNK

P_MLA = MLA_Q_LORA + MLA_KV_LORA + HEAD_PAD
P_QKV = 2 * GLA_QK + GLA_V
P_LR = LANES
P_G = GLA_V
P_Z = SSD_INNER
P_XBC = SSD_CONV_DIM
P_DT = LANES
P_TOTAL = P_MLA + P_QKV + P_LR + P_G + P_Z + P_XBC + P_DT

NEG_BIG = -1e30
LOG2E = 1.4426950408889634


def _rms(x, gain):
    return x * lax.rsqrt(jnp.mean(x * x, axis=-1, keepdims=True) + EPS) * gain


def _split_hi_lo(x):
    hi = x.astype(BF16)
    lo = (x - hi.astype(F32)).astype(BF16)
    return hi, lo


def _dot(a, b):
    return jnp.dot(a, b, preferred_element_type=F32)


def _dot_nt(a, b):
    return lax.dot_general(a, b, (((1,), (1,)), ((), ())), preferred_element_type=F32)


def _dot_tn(a, b):
    return lax.dot_general(a, b, (((0,), (0,)), ((), ())), preferred_element_type=F32)


def _cparams(sem):
    return pltpu.CompilerParams(dimension_semantics=sem, vmem_limit_bytes=VMEM_LIMIT)


def _const_spec(shape):
    nd = len(shape)
    return pl.BlockSpec(shape, lambda *_: (0,) * nd, pipeline_mode=pl.Buffered(1))


def _swiglu(h_ref, wg_ref, wu_ref, wd_ref, a_ref):
    for c in range(N_FF_CHUNKS):
        h = h_ref[...]
        g = _dot(h, wg_ref[c])
        u = _dot(h, wu_ref[c])
        a_ref[:, c * FF_CHUNK:(c + 1) * FF_CHUNK] = (g * jax.nn.sigmoid(g) * u).astype(BF16)
    return _dot(a_ref[...], wd_ref[...])


def _ffn_in_kernel(x_ref, n1_ref, wg_ref, wu_ref, wd_ref, n2_ref, win_ref,
                   x1_ref, mla_ref, qkv_ref, lr_ref, g_ref, z_ref, xbc_ref, dt_ref,
                   h_ref, a_ref):
    x = x_ref[...]
    h_ref[...] = _rms(x, n1_ref[...]).astype(BF16)
    x1 = x + 0.5 * _swiglu(h_ref, wg_ref, wu_ref, wd_ref, a_ref)
    x1_ref[...] = x1
    h2 = _rms(x1, n2_ref[...]).astype(BF16)
    off = 0
    for ref, width in ((mla_ref, P_MLA), (qkv_ref, P_QKV), (lr_ref, P_LR), (g_ref, P_G),
                       (z_ref, P_Z), (xbc_ref, P_XBC), (dt_ref, P_DT)):
        ref[...] = _dot(h2, win_ref[:, off:off + width])
        off += width


def _ffn_in(x, n1, wg, wu, wd, n2, win, tm):
    t = x.shape[0]
    widths = (D_MODEL, P_MLA, P_QKV, P_LR, P_G, P_Z, P_XBC, P_DT)
    tok = lambda w: pl.BlockSpec((tm, w), lambda i: (i, 0))
    return pl.pallas_call(
        _ffn_in_kernel,
        grid=(t // tm,),
        in_specs=[tok(D_MODEL), _const_spec(n1.shape), _const_spec(wg.shape), _const_spec(wu.shape),
                  _const_spec(wd.shape), _const_spec(n2.shape), _const_spec(win.shape)],
        out_specs=[tok(w) for w in widths],
        out_shape=[jax.ShapeDtypeStruct((t, w), F32) for w in widths],
        scratch_shapes=[pltpu.VMEM((tm, D_MODEL), BF16), pltpu.VMEM((tm, D_FF), BF16)],
        compiler_params=_cparams(("parallel",)),
        name="ffn_in",
    )(x, n1, wg, wu, wd, n2, win)


def _ffn_out_kernel(x1_ref, omla_ref, ogla_ref, g_ref, yssd_ref, z_ref,
                    onorm_ref, snorm_ref, blk64_ref, blk128_ref, wout_ref,
                    n_ref, wg_ref, wu_ref, wd_ref, fn_ref,
                    y_ref, h_ref, a_ref):
    og = ogla_ref[0] + ogla_ref[1]
    ss = _dot((og * og).astype(BF16), blk64_ref[...])
    g = g_ref[...]
    m_gla = og * lax.rsqrt(ss * (1.0 / GLA_DV) + EPS) * onorm_ref[...] * (g * jax.nn.sigmoid(g))
    z = z_ref[...]
    ys = (yssd_ref[0] + yssd_ref[1]) * (z * jax.nn.sigmoid(z))
    ss2 = _dot((ys * ys).astype(BF16), blk128_ref[...])
    m_ssd = ys * lax.rsqrt(ss2 * (1.0 / (SSD_INNER // SSD_GROUPS)) + EPS) * snorm_ref[...]
    m = jnp.concatenate([omla_ref[...], m_gla.astype(BF16), m_ssd.astype(BF16)], axis=-1)
    x2 = x1_ref[...] + _dot(m, wout_ref[...])
    h_ref[...] = _rms(x2, n_ref[...]).astype(BF16)
    x3 = x2 + 0.5 * _swiglu(h_ref, wg_ref, wu_ref, wd_ref, a_ref)
    y_ref[...] = _rms(x3, fn_ref[...])


def _ffn_out(x1, omla, ogla, g, yssd, z, onorm, snorm, blk64, blk128, wout, n, wg, wu, wd, fn, tm):
    t = x1.shape[0]
    tok = lambda w: pl.BlockSpec((tm, w), lambda i: (i, 0))
    tok2 = lambda w: pl.BlockSpec((2, tm, w), lambda i: (0, i, 0))
    consts = (onorm, snorm, blk64, blk128, wout, n, wg, wu, wd, fn)
    return pl.pallas_call(
        _ffn_out_kernel,
        grid=(t // tm,),
        in_specs=[tok(D_MODEL), tok(MLA_HEADS * MLA_V), tok2(GLA_V), tok(GLA_V), tok2(SSD_INNER), tok(SSD_INNER)]
        + [_const_spec(c.shape) for c in consts],
        out_specs=tok(D_MODEL),
        out_shape=jax.ShapeDtypeStruct((t, D_MODEL), F32),
        scratch_shapes=[pltpu.VMEM((tm, D_MODEL), BF16), pltpu.VMEM((tm, D_FF), BF16)],
        compiler_params=_cparams(("parallel",)),
        name="ffn_out",
    )(x1, omla, ogla, g, yssd, z, *consts)


def _mla_prep_kernel(p_ref, qn_ref, wuqt_ref, kvn_ref, wk_ref, wvt_ref, gq_ref, gk_ref,
                     cost_ref, sint_ref, cn_ref, s1_ref, s2_ref,
                     qt_ref, k_ref, vt_ref):
    p = p_ref[0]
    cq = p[:, :MLA_Q_LORA]
    ckv = p[:, MLA_Q_LORA:MLA_Q_LORA + MLA_KV_LORA]
    kr = p[:, MLA_Q_LORA + MLA_KV_LORA:]
    hq = _rms(cq, qn_ref[...]).astype(BF16)
    hkv = _rms(ckv, kvn_ref[...]).astype(BF16)

    qt = _dot_nt(wuqt_ref[...], hq)
    cos_t = cost_ref[...]
    sin_t = sint_ref[...]
    gq = gq_ref[...]
    half = MLA_ROPE // 2
    for h in range(MLA_HEADS):
        x = qt[h * HEAD_PAD:(h + 1) * HEAD_PAD]
        ss = jnp.sum(x * x, axis=0, keepdims=True)
        x = x * lax.rsqrt(ss * (1.0 / MLA_QK) + EPS) * gq
        x1 = x[MLA_NOPE:MLA_NOPE + half]
        x2 = x[MLA_NOPE + half:MLA_QK]
        qt_ref[0, h, 0:MLA_NOPE, :] = x[0:MLA_NOPE].astype(BF16)
        qt_ref[0, h, MLA_NOPE:MLA_NOPE + half, :] = (x1 * cos_t - x2 * sin_t).astype(BF16)
        qt_ref[0, h, MLA_NOPE + half:MLA_QK, :] = (x1 * sin_t + x2 * cos_t).astype(BF16)
        qt_ref[0, h, MLA_QK:HEAD_PAD, :] = x[MLA_QK:HEAD_PAD].astype(BF16)

    gk = gk_ref[...]
    krg = kr * gk
    k_rot = (krg * cn_ref[...] + pltpu.roll(krg, HEAD_PAD - half, axis=1) * s1_ref[...]
             + pltpu.roll(krg, half, axis=1) * s2_ref[...])
    ss_rope = jnp.sum(kr * kr, axis=-1, keepdims=True)
    kn = _dot(hkv, wk_ref[...])
    for h in range(MLA_HEADS):
        x = kn[:, h * HEAD_PAD:(h + 1) * HEAD_PAD]
        ss = jnp.sum(x * x, axis=-1, keepdims=True) + ss_rope
        k_ref[0, h] = ((x * gk + k_rot) * lax.rsqrt(ss * (1.0 / MLA_QK) + EPS)).astype(BF16)

    vt = _dot_nt(wvt_ref[...], hkv)
    ones_tile = jnp.where(lax.broadcasted_iota(jnp.int32, (V_ROWS - MLA_V, vt.shape[1]), 0) == 0, 1.0, 0.0).astype(BF16)
    for h in range(MLA_HEADS):
        vt_ref[0, h, 0:MLA_V, :] = vt[h * MLA_V:(h + 1) * MLA_V].astype(BF16)
        vt_ref[0, h, MLA_V:V_ROWS, :] = ones_tile


def _mla_prep(p_mla, qn, wuqt, kvn, wk, wvt, gq, gk, cos_t, sin_t, c_n, s1_n, s2_n, tm):
    b, l, _ = p_mla.shape
    consts = (qn, wuqt, kvn, wk, wvt, gq, gk)
    half = MLA_ROPE // 2
    return pl.pallas_call(
        _mla_prep_kernel,
        grid=(b, l // tm),
        in_specs=[pl.BlockSpec((1, tm, P_MLA), lambda i, j: (i, j, 0))]
        + [_const_spec(c.shape) for c in consts]
        + [pl.BlockSpec((half, tm), lambda i, j: (0, j)), pl.BlockSpec((half, tm), lambda i, j: (0, j)),
           pl.BlockSpec((tm, HEAD_PAD), lambda i, j: (j, 0)), pl.BlockSpec((tm, HEAD_PAD), lambda i, j: (j, 0)),
           pl.BlockSpec((tm, HEAD_PAD), lambda i, j: (j, 0))],
        out_specs=[pl.BlockSpec((1, MLA_HEADS, HEAD_PAD, tm), lambda i, j: (i, 0, 0, j)),
                   pl.BlockSpec((1, MLA_HEADS, tm, HEAD_PAD), lambda i, j: (i, 0, j, 0)),
                   pl.BlockSpec((1, MLA_HEADS, V_ROWS, tm), lambda i, j: (i, 0, 0, j))],
        out_shape=[jax.ShapeDtypeStruct((b, MLA_HEADS, HEAD_PAD, l), BF16),
                   jax.ShapeDtypeStruct((b, MLA_HEADS, l, HEAD_PAD), BF16),
                   jax.ShapeDtypeStruct((b, MLA_HEADS, V_ROWS, l), BF16)],
        compiler_params=_cparams(("parallel", "parallel")),
        name="mla_prep",
    )(p_mla, *consts, cos_t, sin_t, c_n, s1_n, s2_n)


ATTN_HEADS_PER_STEP = 2
ATTN_BLOCKS_PER_TRIP = 16
ATTN_LOOKAHEAD = 3


def _mla_attn_kernel(qt_ref, k_ref, vt_ref, o_ref, ot_ref, s_ref, *, tk, unroll, lookahead):
    n_kb = k_ref.shape[2] // tk
    tq = qt_ref.shape[3]
    n_slots = lookahead + 1
    for hh in range(ATTN_HEADS_PER_STEP):
        qt = qt_ref[0, hh]

        def scores(kb):
            start = pl.multiple_of(kb * tk, tk)
            return _dot(k_ref[0, hh, pl.ds(start, tk), :], qt)

        def body(i, carry):
            m, acc = carry
            for u in range(unroll):
                kb = i * unroll + u
                s_ref[(u + lookahead) % n_slots] = scores(jnp.minimum(kb + lookahead, n_kb - 1))
                s = s_ref[u % n_slots]
                m_new = jnp.maximum(m, jnp.max(s, axis=0, keepdims=True))
                alpha = jnp.exp2(m - m_new)
                p = jnp.exp2(s - m_new)
                start = pl.multiple_of(kb * tk, tk)
                acc = alpha * acc + _dot(vt_ref[0, hh, :, pl.ds(start, tk)], p.astype(BF16))
                m = m_new
            return m, acc

        for kb in range(lookahead):
            s_ref[kb] = scores(kb)
        m0 = jnp.full((1, tq), -jnp.inf, F32)
        a0 = jnp.zeros((V_ROWS, tq), F32)
        _, acc = lax.fori_loop(0, n_kb // unroll, body, (m0, a0))
        ot_ref[hh * MLA_V:(hh + 1) * MLA_V, :] = acc[0:MLA_V] * (1.0 / acc[MLA_V:MLA_V + 1])
    o_ref[0] = ot_ref[...].T.astype(BF16)


def _mla_attn(qt, k, vt, tq, tk):
    b, nh, _, l = qt.shape
    hp = ATTN_HEADS_PER_STEP
    n_kb = l // tk
    unroll = min(ATTN_BLOCKS_PER_TRIP, n_kb)
    lookahead = min(ATTN_LOOKAHEAD, unroll - 1)
    assert n_kb % unroll == 0 and unroll % (lookahead + 1) == 0
    return pl.pallas_call(
        functools.partial(_mla_attn_kernel, tk=tk, unroll=unroll, lookahead=lookahead),
        grid=(b, nh // hp, l // tq),
        in_specs=[pl.BlockSpec((1, hp, HEAD_PAD, tq), lambda i, h, j: (i, h, 0, j)),
                  pl.BlockSpec((1, hp, l, HEAD_PAD), lambda i, h, j: (i, h, 0, 0)),
                  pl.BlockSpec((1, hp, V_ROWS, l), lambda i, h, j: (i, h, 0, 0))],
        out_specs=pl.BlockSpec((1, tq, hp * MLA_V), lambda i, h, j: (i, j, h)),
        out_shape=jax.ShapeDtypeStruct((b, l, nh * MLA_V), BF16),
        scratch_shapes=[pltpu.VMEM((hp * MLA_V, tq), F32),
                        pltpu.VMEM((lookahead + 1, tk, tq), F32)],
        compiler_params=_cparams(("parallel", "parallel", "arbitrary")),
        name="mla_attn",
    )(qt, k, vt)


def _gla_kernel(qkv_ref, lr_ref, wgh_ref, wgl_ref, bg_ref, cum_ref, tot_ref, ind_ref, vmask_ref, hmask_ref,
                o_ref, st_ref, dec_ref, upd_ref, stb_ref, p_ref, *, tg):
    d = pl.program_id(1)
    t = pl.program_id(2)
    n_chunks = tg // GLA_CHUNK

    @pl.when(t == 0)
    def _():
        st_ref[...] = jnp.zeros_like(st_ref)

    qkv = qkv_ref[0]
    q = qkv[:, :GLA_QK] * (GLA_DK ** -0.5)
    k = qkv[:, GLA_QK:2 * GLA_QK]
    v = qkv[:, 2 * GLA_QK:]

    lr_hi, lr_lo = _split_hi_lo(lr_ref[0])
    zg = _dot(lr_hi, wgh_ref[0]) + _dot(lr_lo, wgh_ref[0]) + _dot(lr_hi, wgl_ref[0]) + bg_ref[0]
    log_a = (jnp.minimum(zg, 0.0) - jnp.log1p(jnp.exp(-jnp.abs(zg)))) * (1.0 / GLA_GATE_NORM)

    la_hi, la_lo = _split_hi_lo(log_a)
    bcum = _dot(cum_ref[0], la_hi) + _dot(cum_ref[0], la_lo)
    btot = _dot(tot_ref[...], la_hi) + _dot(tot_ref[...], la_lo)

    qd = (q * jnp.exp(bcum)).astype(BF16)
    kd = (k * jnp.exp(btot - bcum)).astype(BF16)
    vb = v.astype(BF16)
    dec_ref[...] = jnp.exp(btot)

    fwd = d == 0
    chunk_rows = [slice(c * GLA_CHUNK, (c + 1) * GLA_CHUNK) for c in range(n_chunks)]

    q3 = q.reshape(n_chunks, GLA_CHUNK, GLA_QK)
    k3 = k.reshape(n_chunks, GLA_CHUNK, GLA_QK)
    b3 = bcum.reshape(n_chunks, GLA_CHUNK, GLA_QK)
    tin = lax.broadcasted_iota(jnp.int32, (n_chunks, GLA_CHUNK, GLA_QK), 1)
    tin = jnp.where(fwd, tin, -tin)
    hmask = hmask_ref[...]
    pending = list(enumerate(chunk_rows))
    for j in range(GLA_CHUNK):
        for c, rows in pending[j::GLA_CHUNK]:
            upd_ref[c] = _dot_tn(vb[rows], kd[rows]) * hmask
        valid = tin >= jnp.where(fwd, j, -j)
        e = jnp.exp(jnp.where(valid, b3 - b3[:, j:j + 1, :], NEG_BIG))
        p_ref[:, j * GLA_QK:(j + 1) * GLA_QK] = (q3 * k3[:, j:j + 1, :] * e).reshape(tg, GLA_QK).astype(BF16)

    def body(ci, st):
        c = jnp.where(fwd, ci, n_chunks - 1 - ci)
        stb_ref[c] = st.T.astype(BF16)
        return st * dec_ref[pl.ds(c * GLA_CHUNK, 1), :] + upd_ref[c]

    st_ref[...] = lax.fori_loop(0, n_chunks, body, st_ref[...], unroll=True)
    a_intra = _dot(p_ref[...], ind_ref[...]).astype(BF16)
    vmask = vmask_ref[...]
    outs = []
    for c, rows in enumerate(chunk_rows):
        v_rows = jnp.concatenate([vb[rows]] * GLA_HEADS, axis=0) * vmask
        lhs = jnp.concatenate([qd[rows], a_intra[rows]], axis=1)
        outs.append(_dot(lhs, jnp.concatenate([stb_ref[c], v_rows], axis=0)))
    o_ref[0, 0] = jnp.concatenate(outs, axis=0)


def _gla_scan(qkv, lr, wgh, wgl, bg, cum, tot, ind, vmask, hmask, tg):
    b, l, _ = qkv.shape
    nt = l // tg
    tile = lambda d, t: t + d * (nt - 1 - 2 * t)
    return pl.pallas_call(
        functools.partial(_gla_kernel, tg=tg),
        grid=(b, 2, nt),
        in_specs=[pl.BlockSpec((1, tg, P_QKV), lambda i, d, t: (i, tile(d, t), 0)),
                  pl.BlockSpec((1, tg, P_LR), lambda i, d, t: (i, tile(d, t), 0)),
                  pl.BlockSpec((1, LANES, GLA_QK), lambda i, d, t: (d, 0, 0)),
                  pl.BlockSpec((1, LANES, GLA_QK), lambda i, d, t: (d, 0, 0)),
                  pl.BlockSpec((1, 1, GLA_QK), lambda i, d, t: (d, 0, 0)),
                  pl.BlockSpec((1, tg, tg), lambda i, d, t: (d, 0, 0)),
                  _const_spec(tot.shape), _const_spec(ind.shape), _const_spec(vmask.shape),
                  _const_spec(hmask.shape)],
        out_specs=pl.BlockSpec((1, 1, tg, GLA_V), lambda i, d, t: (d, i, tile(d, t), 0)),
        out_shape=jax.ShapeDtypeStruct((2, b, l, GLA_V), F32),
        scratch_shapes=[pltpu.VMEM((GLA_V, GLA_QK), F32), pltpu.VMEM((tg, GLA_QK), F32),
                        pltpu.VMEM((tg // GLA_CHUNK, GLA_V, GLA_QK), F32),
                        pltpu.VMEM((tg // GLA_CHUNK, GLA_QK, GLA_V), BF16),
                        pltpu.VMEM((tg, GLA_CHUNK * GLA_QK), BF16)],
        compiler_params=_cparams(("parallel", "parallel", "arbitrary")),
        name="gla_scan",
    )(qkv, lr, wgh, wgl, bg, cum, tot, ind, vmask, hmask)


CONV_HALO = SUBLANES
SSD_TILE_CHUNKS = 4


def _softplus(x):
    return jnp.maximum(x, 0.0) + jnp.log1p(jnp.exp(-jnp.abs(x)))


def _ssd_kernel(xc_ref, xp_ref, xn_ref, dt_ref, dtt_ref, cw_ref, cb_ref, bias_r_ref, bias_c_ref,
                alog_r_ref, alog_c_ref, dskip_ref, tri_ref, shift_ref, rep_ref,
                y_ref, xe_ref, st_ref, upd_ref, dec_ref, stb_ref, cgb_ref, e64_ref, *, n_tiles, n_chunks):
    d = pl.program_id(1)
    t = pl.program_id(2)
    tt = t + d * (n_tiles - 1 - 2 * t)
    cs = SSD_CHUNK
    ts = n_chunks * cs
    fwd = d == 0
    hpg = SSD_HEADS // SSD_GROUPS
    gw = hpg * SSD_HEADDIM
    pad = SSD_CONV // 2

    @pl.when(t == 0)
    def _():
        st_ref[...] = jnp.zeros_like(st_ref)

    xe_ref[0:CONV_HALO, :] = jnp.where(tt == 0, 0.0, xp_ref[0])
    xe_ref[CONV_HALO:CONV_HALO + ts, :] = xc_ref[0]
    xe_ref[CONV_HALO + ts:, :] = jnp.where(tt == n_tiles - 1, 0.0, xn_ref[0])

    tri_d = tri_ref[d]
    tri_o = tri_ref[1 - d]
    mask = tri_d.astype(F32) > 0.5
    rep64 = rep_ref[0]
    neg_a_r = -jnp.exp(alog_r_ref[...])
    neg_a_c = -jnp.exp(alog_c_ref[...])
    lane_head = lax.broadcasted_iota(jnp.int32, (cs, gw), 1) // SSD_HEADDIM
    skip_gain = jnp.where(fwd, 1.0, 0.0) * dskip_ref[...]

    for c in range(n_chunks):
        rows = slice(c * cs, (c + 1) * cs)
        xe_b = xe_ref[c * cs:(c + 1) * cs + 2 * CONV_HALO, :].astype(BF16)
        acc = cb_ref[...] + cw_ref[pad:pad + 1, :] * xe_ref[CONV_HALO + c * cs:CONV_HALO + (c + 1) * cs, :]
        for i, kk in enumerate(k for k in range(SSD_CONV) if k != pad):
            acc = acc + cw_ref[kk:kk + 1, :] * _dot(shift_ref[i], xe_b)
        act = acc * jax.nn.sigmoid(acc)
        xs = act[:, :SSD_INNER]
        bm = act[:, SSD_INNER:SSD_INNER + SSD_GROUPS * SSD_STATE]
        cmb = act[:, SSD_INNER + SSD_GROUPS * SSD_STATE:].astype(BF16)

        dt_c = _softplus(dt_ref[0, rows, :] + bias_r_ref[...])
        la_c = dt_c * neg_a_r
        dt_r = _softplus(dtt_ref[0, :, rows] + bias_c_ref[...])
        la_r = dt_r * neg_a_c
        lc_hi, lc_lo = _split_hi_lo(la_c)
        cum_c = _dot(tri_d, lc_hi) + _dot(tri_d, lc_lo)
        lr_hi, lr_lo = _split_hi_lo(la_r)
        cum_r = _dot(lr_hi, tri_o) + _dot(lr_lo, tri_o)
        dt_hi, dt_lo = _split_hi_lo(dt_c)
        dt64 = _dot(dt_hi, rep64) + _dot(dt_lo, rep64)
        cu_hi, cu_lo = _split_hi_lo(cum_c)
        cum64 = _dot(cu_hi, rep64) + _dot(cu_lo, rep64)
        tot64 = jnp.where(fwd, cum64[cs - 1:cs, :], cum64[0:1, :])

        xc = xs * dt64
        xcb = xc.astype(BF16)
        xdec = (xc * jnp.exp(tot64 - cum64)).astype(BF16)
        e64_ref[rows, :] = jnp.exp(cum64)
        dec_ref[c] = jnp.exp(tot64)
        cgb_ref[rows, :] = cmb
        ys = []
        for g in range(SSD_GROUPS):
            gl = slice(g * gw, (g + 1) * gw)
            bg = bm[:, g * SSD_STATE:(g + 1) * SSD_STATE]
            cb = _dot_nt(cmb[:, g * SSD_STATE:(g + 1) * SSD_STATE], bg.astype(BF16))
            y_g = jnp.zeros((cs, gw), F32)
            cum_g = cum64[:, gl]
            cum_swapped = pltpu.roll(cum_g, SSD_HEADDIM, axis=1)
            for hg in range(hpg):
                h = g * hpg + hg
                cr = jnp.where(fwd, cum_r[h:h + 1, :], cum_r[SSD_HEADS + h:SSD_HEADS + h + 1, :])
                cum_h = jnp.where(lane_head == hg, cum_g, cum_swapped)
                seg = jnp.exp(jnp.where(mask, cum_h - cr, NEG_BIG))
                yd = _dot((cb * seg).astype(BF16), xcb[:, gl])
                y_g = y_g + jnp.where(lane_head == hg, yd, 0.0)
            ys.append(y_g)
            upd_ref[c, g] = _dot(bg.T.astype(BF16), xdec[:, gl])
        y_ref[0, 0, rows, :] = jnp.concatenate(ys, axis=1) + skip_gain * xs

    def body(ci, st):
        c = jnp.where(fwd, ci, n_chunks - 1 - ci)
        dec = dec_ref[c]
        new = []
        for g in range(SSD_GROUPS):
            stb_ref[c, g] = st[g].astype(BF16)
            new.append(st[g] * dec[:, g * gw:(g + 1) * gw] + upd_ref[c, g])
        return tuple(new)

    st = lax.fori_loop(0, n_chunks, body, tuple(st_ref[g] for g in range(SSD_GROUPS)), unroll=True)
    for g in range(SSD_GROUPS):
        st_ref[g] = st[g]

    for c in range(n_chunks):
        rows = slice(c * cs, (c + 1) * cs)
        y_off = jnp.concatenate(
            [_dot(cgb_ref[rows, g * SSD_STATE:(g + 1) * SSD_STATE], stb_ref[c, g]) for g in range(SSD_GROUPS)], axis=1)
        y_ref[0, 0, rows, :] = y_ref[0, 0, rows, :] + e64_ref[rows, :] * y_off


def _ssd_scan(xbc, dt, dtt, cw, cb, bias_r, bias_c, alog_r, alog_c, dskip, tri, shift, rep, n_chunks):
    b, l, _ = xbc.shape
    ts = n_chunks * SSD_CHUNK
    nt = l // ts
    hb = ts // CONV_HALO
    n_hb = l // CONV_HALO
    gw = SSD_INNER // SSD_GROUPS
    tile = lambda d, t: t + d * (nt - 1 - 2 * t)
    consts = (cw, cb, bias_r, bias_c, alog_r, alog_c, dskip, tri, shift)
    return pl.pallas_call(
        functools.partial(_ssd_kernel, n_tiles=nt, n_chunks=n_chunks),
        grid=(b, 2, nt),
        in_specs=[pl.BlockSpec((1, ts, P_XBC), lambda i, d, t: (i, tile(d, t), 0)),
                  pl.BlockSpec((1, CONV_HALO, P_XBC), lambda i, d, t: (i, jnp.maximum(tile(d, t) * hb - 1, 0), 0)),
                  pl.BlockSpec((1, CONV_HALO, P_XBC),
                               lambda i, d, t: (i, jnp.minimum((tile(d, t) + 1) * hb, n_hb - 1), 0)),
                  pl.BlockSpec((1, ts, P_DT), lambda i, d, t: (i, tile(d, t), 0)),
                  pl.BlockSpec((1, 2 * SSD_HEADS, ts), lambda i, d, t: (i, 0, tile(d, t)))]
        + [_const_spec(c.shape) for c in consts]
        + [pl.BlockSpec((1,) + rep.shape[1:], lambda i, d, t: (d, 0, 0))],
        out_specs=pl.BlockSpec((1, 1, ts, SSD_INNER), lambda i, d, t: (d, i, tile(d, t), 0)),
        out_shape=jax.ShapeDtypeStruct((2, b, l, SSD_INNER), F32),
        scratch_shapes=[pltpu.VMEM((ts + 2 * CONV_HALO, P_XBC), F32),
                        pltpu.VMEM((SSD_GROUPS, SSD_STATE, gw), F32),
                        pltpu.VMEM((n_chunks, SSD_GROUPS, SSD_STATE, gw), F32),
                        pltpu.VMEM((n_chunks, 1, SSD_INNER), F32),
                        pltpu.VMEM((n_chunks, SSD_GROUPS, SSD_STATE, gw), BF16),
                        pltpu.VMEM((ts, SSD_GROUPS * SSD_STATE), BF16),
                        pltpu.VMEM((ts, SSD_INNER), F32)],
        compiler_params=_cparams(("parallel", "parallel", "arbitrary")),
        name="ssd_scan",
    )(xbc, xbc, xbc, dt, dtt, *consts, rep)


def _pad_cols(w, width, at=0):
    out = jnp.zeros(w.shape[:-1] + (width,), w.dtype)
    return out.at[..., at:at + w.shape[-1]].set(w)


def _prep_layer(i, ffn1_norm, ffn1_w_gu, ffn1_w_down, mix_norm, w_in, mla_q_norm, mla_w_uq, mla_kv_norm,
                mla_w_ukv, mla_q_gain, mla_k_gain, gla_w_gate, gla_b_gate, gla_o_norm, ssd_conv_w, ssd_conv_b,
                ssd_a_log, ssd_dt_bias, ssd_d, ssd_norm, w_out, ffn2_norm, ffn2_w_gu, ffn2_w_down, final_norm):
    def ffn_w(w_gu, w_down):
        wg = w_gu[:, :D_FF].reshape(D_MODEL, N_FF_CHUNKS, FF_CHUNK).transpose(1, 0, 2).astype(BF16)
        wu = w_gu[:, D_FF:].reshape(D_MODEL, N_FF_CHUNKS, FF_CHUNK).transpose(1, 0, 2).astype(BF16)
        wd = w_down.astype(BF16)
        return wg, wu, wd

    w = {}
    w["n1"] = ffn1_norm[i][None, :]
    w["wg1"], w["wu1"], w["wd1"] = ffn_w(ffn1_w_gu[i], ffn1_w_down[i])
    w["nmix"] = mix_norm[i][None, :]

    wi = w_in[i]
    o_gla = MLA_IN
    o_ssd = MLA_IN + GLA_IN
    o_mla_kr = MLA_Q_LORA + MLA_KV_LORA
    gla_qkv_w = 2 * GLA_QK + GLA_V
    cols = [
        wi[:, :o_mla_kr],
        _pad_cols(wi[:, o_mla_kr:MLA_IN], HEAD_PAD, at=MLA_NOPE),
        wi[:, o_gla:o_gla + gla_qkv_w],
        _pad_cols(wi[:, o_gla + gla_qkv_w + GLA_V:o_ssd], P_LR),
        wi[:, o_gla + gla_qkv_w:o_gla + gla_qkv_w + GLA_V],
        wi[:, o_ssd:o_ssd + SSD_INNER],
        wi[:, o_ssd + SSD_INNER:o_ssd + SSD_INNER + SSD_CONV_DIM],
        _pad_cols(wi[:, o_ssd + SSD_INNER + SSD_CONV_DIM:], P_DT),
    ]
    w["win"] = jnp.concatenate(cols, axis=1).astype(BF16)

    w["qn"] = mla_q_norm[i][None, :]
    w["kvn"] = mla_kv_norm[i][None, :]
    wuq = _pad_cols(mla_w_uq[i].reshape(MLA_Q_LORA, MLA_HEADS, MLA_QK), HEAD_PAD)
    w["wuqt"] = wuq.reshape(MLA_Q_LORA, MLA_HEADS * HEAD_PAD).T.astype(BF16)
    wukv = mla_w_ukv[i].reshape(MLA_KV_LORA, MLA_HEADS, MLA_NOPE + MLA_V)
    w["wk"] = _pad_cols(wukv[..., :MLA_NOPE], HEAD_PAD).reshape(MLA_KV_LORA, MLA_HEADS * HEAD_PAD).astype(BF16)
    w["wvt"] = wukv[..., MLA_NOPE:].reshape(MLA_KV_LORA, MLA_HEADS * MLA_V).T.astype(BF16)
    w["gq"] = (_pad_cols(mla_q_gain[i], HEAD_PAD) * (MLA_QK ** -0.5 * LOG2E))[:, None]
    w["gk"] = _pad_cols(mla_k_gain[i], HEAD_PAD)[None, :]

    wgate = jnp.zeros((2, LANES, GLA_QK), F32)
    for zdir in range(2):
        wgate = wgate.at[zdir, zdir * GLA_GATE_RANK:(zdir + 1) * GLA_GATE_RANK, :].set(gla_w_gate[i, zdir])
    w["wgh"] = wgate.astype(BF16)
    w["wgl"] = (wgate - w["wgh"].astype(F32)).astype(BF16)
    w["bg"] = gla_b_gate[i][:, None, :]
    w["onorm"] = jnp.tile(gla_o_norm[i], GLA_HEADS)[None, :]

    w["cw"] = jnp.zeros((SUBLANES, SSD_CONV_DIM), F32).at[:SSD_CONV].set(ssd_conv_w[i])
    w["cb"] = ssd_conv_b[i][None, :]
    flat_bias = ssd_dt_bias[i].reshape(2 * SSD_HEADS)
    flat_alog = ssd_a_log[i].reshape(2 * SSD_HEADS)
    w["bias_r"] = _pad_cols(flat_bias, P_DT)[None, :]
    w["bias_c"] = flat_bias[:, None]
    w["alog_r"] = _pad_cols(flat_alog, P_DT)[None, :]
    w["alog_c"] = flat_alog[:, None]
    w["dskip"] = jnp.repeat(ssd_d[i], SSD_HEADDIM)[None, :]
    w["snorm"] = ssd_norm[i][None, :]

    w["wout"] = w_out[i].astype(BF16)
    w["n2"] = ffn2_norm[i][None, :]
    w["wg2"], w["wu2"], w["wd2"] = ffn_w(ffn2_w_gu[i], ffn2_w_down[i])
    w["fn"] = final_norm[i][None, :]
    return w


def _block_ones(n, blk):
    idx = np.arange(n) // blk
    return jnp.asarray(idx[:, None] == idx[None, :], BF16)


def _seq_consts(l, tg):
    half = MLA_ROPE // 2
    pos = jnp.arange(l, dtype=F32)
    inv_freq = 1.0 / (ROPE_BASE ** (jnp.arange(0, MLA_ROPE, 2, dtype=F32) / MLA_ROPE))
    ang = pos[:, None] * inv_freq[None, :]
    cos, sin = jnp.cos(ang), jnp.sin(ang)
    c = {}
    c["cos_t"], c["sin_t"] = cos.T, sin.T
    ones = jnp.ones((l, HEAD_PAD), F32)
    c["c_n"] = ones.at[:, MLA_NOPE:MLA_NOPE + half].set(cos).at[:, MLA_NOPE + half:MLA_QK].set(cos)
    zeros = jnp.zeros((l, HEAD_PAD), F32)
    c["s1_n"] = zeros.at[:, MLA_NOPE:MLA_NOPE + half].set(-sin)
    c["s2_n"] = zeros.at[:, MLA_NOPE + half:MLA_QK].set(sin)

    r = np.arange(tg)
    same = (r[:, None] // GLA_CHUNK) == (r[None, :] // GLA_CHUNK)
    c["gla_cum"] = jnp.asarray(np.stack([same & (r[None, :] <= r[:, None]), same & (r[None, :] >= r[:, None])]), BF16)
    c["gla_tot"] = jnp.asarray(same, BF16)
    qk_head = np.arange(GLA_QK) // GLA_DK
    v_head = np.arange(GLA_V) // GLA_DV
    p_col = np.arange(GLA_CHUNK * GLA_QK)
    a_col = np.arange(GLA_HEADS * GLA_CHUNK)
    c["gla_ind"] = jnp.asarray(((p_col % GLA_QK) // GLA_DK * GLA_CHUNK + p_col // GLA_QK)[:, None] == a_col[None, :], BF16)
    c["gla_vmask"] = jnp.asarray((a_col // GLA_CHUNK)[:, None] == v_head[None, :], BF16)
    c["gla_hmask"] = jnp.asarray(v_head[:, None] == qk_head[None, :], F32)

    q = np.arange(SSD_CHUNK)
    c["ssd_tri"] = jnp.asarray(np.stack([q[None, :] <= q[:, None], q[None, :] >= q[:, None]]), BF16)
    xe_row = np.arange(SSD_CHUNK + 2 * CONV_HALO)
    offs = [k - SSD_CONV // 2 for k in range(SSD_CONV) if k != SSD_CONV // 2]
    c["ssd_shift"] = jnp.asarray(np.stack([xe_row[None, :] == (q[:, None] + CONV_HALO + o) for o in offs]), BF16)
    src_lane = np.arange(P_DT)
    c["ssd_rep"] = jnp.asarray(np.stack(
        [src_lane[:, None] == (zdir * SSD_HEADS + np.arange(SSD_INNER) // SSD_HEADDIM)[None, :] for zdir in range(2)]), BF16)
    return c


def _pick_tile(n, pref):
    t = min(n, pref)
    while n % t:
        t //= 2
    return t


def _layer(x, w, c, tg):
    b, l, _ = x.shape
    t = b * l
    tm = _pick_tile(t, 512)
    x1, p_mla, p_qkv, p_lr, p_g, p_z, p_xbc, p_dt = _ffn_in(
        x.reshape(t, D_MODEL), w["n1"], w["wg1"], w["wu1"], w["wd1"], w["nmix"], w["win"], tm)

    qt, k, vt = _mla_prep(p_mla.reshape(b, l, P_MLA), w["qn"], w["wuqt"], w["kvn"], w["wk"], w["wvt"],
                          w["gq"], w["gk"], c["cos_t"], c["sin_t"], c["c_n"], c["s1_n"], c["s2_n"],
                          _pick_tile(l, 512))
    o_mla = _mla_attn(qt, k, vt, _pick_tile(l, 512), _pick_tile(l, 256))

    o_gla = _gla_scan(p_qkv.reshape(b, l, P_QKV), p_lr.reshape(b, l, P_LR), w["wgh"], w["wgl"], w["bg"],
                      c["gla_cum"], c["gla_tot"], c["gla_ind"], c["gla_vmask"], c["gla_hmask"], tg)

    dt3 = p_dt.reshape(b, l, P_DT)
    dtt = jnp.swapaxes(dt3[:, :, :2 * SSD_HEADS], 1, 2)
    y_ssd = _ssd_scan(p_xbc.reshape(b, l, P_XBC), dt3, dtt, w["cw"], w["cb"], w["bias_r"], w["bias_c"],
                      w["alog_r"], w["alog_c"], w["dskip"], c["ssd_tri"], c["ssd_shift"], c["ssd_rep"],
                      _pick_tile(l // SSD_CHUNK, SSD_TILE_CHUNKS))

    y = _ffn_out(x1, o_mla.reshape(t, MLA_HEADS * MLA_V), o_gla.reshape(2, t, GLA_V), p_g,
                 y_ssd.reshape(2, t, SSD_INNER), p_z, w["onorm"], w["snorm"],
                 _block_ones(GLA_V, GLA_DV), _block_ones(SSD_INNER, SSD_INNER // SSD_GROUPS),
                 w["wout"], w["n2"], w["wg2"], w["wu2"], w["wd2"], w["fn"], tm)
    return y.reshape(b, l, D_MODEL)


def kernel(x_prompt, x_sample, ffn1_norm, ffn1_w_gu, ffn1_w_down, mix_norm, w_in, mla_q_norm, mla_w_uq,
           mla_kv_norm, mla_w_ukv, mla_q_gain, mla_k_gain, gla_w_gate, gla_b_gate, gla_o_norm, ssd_conv_w,
           ssd_conv_b, ssd_a_log, ssd_dt_bias, ssd_d, ssd_norm, w_out, ffn2_norm, ffn2_w_gu, ffn2_w_down,
           final_norm):
    params = (ffn1_norm, ffn1_w_gu, ffn1_w_down, mix_norm, w_in, mla_q_norm, mla_w_uq, mla_kv_norm,
              mla_w_ukv, mla_q_gain, mla_k_gain, gla_w_gate, gla_b_gate, gla_o_norm, ssd_conv_w, ssd_conv_b,
              ssd_a_log, ssd_dt_bias, ssd_d, ssd_norm, w_out, ffn2_norm, ffn2_w_gu, ffn2_w_down, final_norm)
    depth = ffn1_norm.shape[0]
    streams = [x_prompt, x_sample]
    tgs = [_pick_tile(s.shape[1], 256) for s in streams]
    consts = [_seq_consts(s.shape[1], tg) for s, tg in zip(streams, tgs)]
    for i in range(depth):
        w = _prep_layer(i, *params)
        streams = [_layer(s, w, c, tg) for s, c, tg in zip(streams, consts, tgs)]
    return tuple(streams)
```

```python
import functools
import math

import jax
import jax.numpy as jnp
import numpy as np
from jax import lax
from jax.experimental import pallas as pl
from jax.experimental.pallas import tpu as pltpu

F32 = jnp.float32
BF16 = jnp.bfloat16

D_MODEL = 1024
D_FF = 2816
EPS = 1e-6
MLA_HEADS = 8
MLA_Q_LORA = 384
MLA_KV_LORA = 256
MLA_NOPE = 64
MLA_ROPE = 32
MLA_QK = MLA_NOPE + MLA_ROPE
MLA_V = 64
ROPE_BASE = 10000.0
GLA_HEADS = 4
GLA_DK = 32
GLA_DV = 64
GLA_GATE_RANK = 16
GLA_GATE_NORM = 16.0
GLA_CHUNK = 16
SSD_HEADS = 4
SSD_HEADDIM = 64
SSD_INNER = SSD_HEADS * SSD_HEADDIM
SSD_GROUPS = 2
SSD_STATE = 128
SSD_CONV = 5
SSD_CHUNK = 128
SSD_CONV_DIM = SSD_INNER + 2 * SSD_GROUPS * SSD_STATE
MLA_IN = MLA_Q_LORA + MLA_KV_LORA + MLA_ROPE
GLA_IN = 2 * GLA_HEADS * GLA_DK + 2 * GLA_HEADS * GLA_DV + 2 * GLA_GATE_RANK
SSD_IN = SSD_INNER + SSD_CONV_DIM + 2 * SSD_HEADS

LANES = 128
SUBLANES = 8
MXU_DIM = 256
VMEM_LIMIT = 56 * 1024 * 1024

HEAD_PAD = LANES
V_ROWS = MLA_V + 16
GLA_QK = GLA_HEADS * GLA_DK
GLA_V = GLA_HEADS * GLA_DV
FF_CHUNK = MXU_DIM
N_FF_CHUNKS = D_FF // FF_CHUNK

P_MLA = MLA_Q_LORA + MLA_KV_LORA + HEAD_PAD
P_QKV = 2 * GLA_QK + GLA_V
P_LR = LANES
P_G = GLA_V
P_Z = SSD_INNER
P_XBC = SSD_CONV_DIM
P_DT = LANES
P_TOTAL = P_MLA + P_QKV + P_LR + P_G + P_Z + P_XBC + P_DT

NEG_BIG = -1e30
LOG2E = 1.4426950408889634


def _rms(x, gain):
    return x * lax.rsqrt(jnp.mean(x * x, axis=-1, keepdims=True) + EPS) * gain


def _split_hi_lo(x):
    hi = x.astype(BF16)
    lo = (x - hi.astype(F32)).astype(BF16)
    return hi, lo


def _dot(a, b):
    return jnp.dot(a, b, preferred_element_type=F32)


def _dot_nt(a, b):
    return lax.dot_general(a, b, (((1,), (1,)), ((), ())), preferred_element_type=F32)


def _dot_tn(a, b):
    return lax.dot_general(a, b, (((0,), (0,)), ((), ())), preferred_element_type=F32)


def _cparams(sem):
    return pltpu.CompilerParams(dimension_semantics=sem, vmem_limit_bytes=VMEM_LIMIT)


def _const_spec(shape):
    nd = len(shape)
    return pl.BlockSpec(shape, lambda *_: (0,) * nd, pipeline_mode=pl.Buffered(1))


def _swiglu(h_ref, wg_ref, wu_ref, wd_ref, a_ref):
    for c in range(N_FF_CHUNKS):
        h = h_ref[...]
        g = _dot(h, wg_ref[c])
        u = _dot(h, wu_ref[c])
        a_ref[:, c * FF_CHUNK:(c + 1) * FF_CHUNK] = (g * jax.nn.sigmoid(g) * u).astype(BF16)
    return _dot(a_ref[...], wd_ref[...])


def _ffn_in_kernel(x_ref, n1_ref, wg_ref, wu_ref, wd_ref, n2_ref, win_ref,
                   x1_ref, mla_ref, qkv_ref, lr_ref, g_ref, z_ref, xbc_ref, dt_ref,
                   h_ref, a_ref):
    x = x_ref[...]
    h_ref[...] = _rms(x, n1_ref[...]).astype(BF16)
    x1 = x + 0.5 * _swiglu(h_ref, wg_ref, wu_ref, wd_ref, a_ref)
    x1_ref[...] = x1
    h2 = _rms(x1, n2_ref[...]).astype(BF16)
    off = 0
    for ref, width in ((mla_ref, P_MLA), (qkv_ref, P_QKV), (lr_ref, P_LR), (g_ref, P_G),
                       (z_ref, P_Z), (xbc_ref, P_XBC), (dt_ref, P_DT)):
        ref[...] = _dot(h2, win_ref[:, off:off + width])
        off += width


def _ffn_in(x, n1, wg, wu, wd, n2, win, tm):
    t = x.shape[0]
    widths = (D_MODEL, P_MLA, P_QKV, P_LR, P_G, P_Z, P_XBC, P_DT)
    tok = lambda w: pl.BlockSpec((tm, w), lambda i: (i, 0))
    return pl.pallas_call(
        _ffn_in_kernel,
        grid=(t // tm,),
        in_specs=[tok(D_MODEL), _const_spec(n1.shape), _const_spec(wg.shape), _const_spec(wu.shape),
                  _const_spec(wd.shape), _const_spec(n2.shape), _const_spec(win.shape)],
        out_specs=[tok(w) for w in widths],
        out_shape=[jax.ShapeDtypeStruct((t, w), F32) for w in widths],
        scratch_shapes=[pltpu.VMEM((tm, D_MODEL), BF16), pltpu.VMEM((tm, D_FF), BF16)],
        compiler_params=_cparams(("parallel",)),
        name="ffn_in",
    )(x, n1, wg, wu, wd, n2, win)


def _ffn_out_kernel(x1_ref, omla_ref, ogla_ref, g_ref, yssd_ref, z_ref,
                    onorm_ref, snorm_ref, blk64_ref, blk128_ref, wout_ref,
                    n_ref, wg_ref, wu_ref, wd_ref, fn_ref,
                    y_ref, h_ref, a_ref):
    og = ogla_ref[0] + ogla_ref[1]
    ss = _dot((og * og).astype(BF16), blk64_ref[...])
    g = g_ref[...]
    m_gla = og * lax.rsqrt(ss * (1.0 / GLA_DV) + EPS) * onorm_ref[...] * (g * jax.nn.sigmoid(g))
    z = z_ref[...]
    ys = (yssd_ref[0] + yssd_ref[1]) * (z * jax.nn.sigmoid(z))
    ss2 = _dot((ys * ys).astype(BF16), blk128_ref[...])
    m_ssd = ys * lax.rsqrt(ss2 * (1.0 / (SSD_INNER // SSD_GROUPS)) + EPS) * snorm_ref[...]
    m = jnp.concatenate([omla_ref[...], m_gla.astype(BF16), m_ssd.astype(BF16)], axis=-1)
    x2 = x1_ref[...] + _dot(m, wout_ref[...])
    h_ref[...] = _rms(x2, n_ref[...]).astype(BF16)
    x3 = x2 + 0.5 * _swiglu(h_ref, wg_ref, wu_ref, wd_ref, a_ref)
    y_ref[...] = _rms(x3, fn_ref[...])


def _ffn_out(x1, omla, ogla, g, yssd, z, onorm, snorm, blk64, blk128, wout, n, wg, wu, wd, fn, tm):
    t = x1.shape[0]
    tok = lambda w: pl.BlockSpec((tm, w), lambda i: (i, 0))
    tok2 = lambda w: pl.BlockSpec((2, tm, w), lambda i: (0, i, 0))
    consts = (onorm, snorm, blk64, blk128, wout, n, wg, wu, wd, fn)
    return pl.pallas_call(
        _ffn_out_kernel,
        grid=(t // tm,),
        in_specs=[tok(D_MODEL), tok(MLA_HEADS * MLA_V), tok2(GLA_V), tok(GLA_V), tok2(SSD_INNER), tok(SSD_INNER)]
        + [_const_spec(c.shape) for c in consts],
        out_specs=tok(D_MODEL),
        out_shape=jax.ShapeDtypeStruct((t, D_MODEL), F32),
        scratch_shapes=[pltpu.VMEM((tm, D_MODEL), BF16), pltpu.VMEM((tm, D_FF), BF16)],
        compiler_params=_cparams(("parallel",)),
        name="ffn_out",
    )(x1, omla, ogla, g, yssd, z, *consts)


def _mla_prep_kernel(p_ref, qn_ref, wuqt_ref, kvn_ref, wk_ref, wvt_ref, gq_ref, gk_ref,
                     cost_ref, sint_ref, cn_ref, s1_ref, s2_ref,
                     qt_ref, k_ref, vt_ref):
    p = p_ref[0]
    cq = p[:, :MLA_Q_LORA]
    ckv = p[:, MLA_Q_LORA:MLA_Q_LORA + MLA_KV_LORA]
    kr = p[:, MLA_Q_LORA + MLA_KV_LORA:]
    hq = _rms(cq, qn_ref[...]).astype(BF16)
    hkv = _rms(ckv, kvn_ref[...]).astype(BF16)

    qt = _dot_nt(wuqt_ref[...], hq)
    cos_t = cost_ref[...]
    sin_t = sint_ref[...]
    gq = gq_ref[...]
    half = MLA_ROPE // 2
    for h in range(MLA_HEADS):
        x = qt[h * HEAD_PAD:(h + 1) * HEAD_PAD]
        ss = jnp.sum(x * x, axis=0, keepdims=True)
        x = x * lax.rsqrt(ss * (1.0 / MLA_QK) + EPS) * gq
        x1 = x[MLA_NOPE:MLA_NOPE + half]
        x2 = x[MLA_NOPE + half:MLA_QK]
        qt_ref[0, h, 0:MLA_NOPE, :] = x[0:MLA_NOPE].astype(BF16)
        qt_ref[0, h, MLA_NOPE:MLA_NOPE + half, :] = (x1 * cos_t - x2 * sin_t).astype(BF16)
        qt_ref[0, h, MLA_NOPE + half:MLA_QK, :] = (x1 * sin_t + x2 * cos_t).astype(BF16)
        qt_ref[0, h, MLA_QK:HEAD_PAD, :] = x[MLA_QK:HEAD_PAD].astype(BF16)

    gk = gk_ref[...]
    krg = kr * gk
    k_rot = (krg * cn_ref[...] + pltpu.roll(krg, HEAD_PAD - half, axis=1) * s1_ref[...]
             + pltpu.roll(krg, half, axis=1) * s2_ref[...])
    ss_rope = jnp.sum(kr * kr, axis=-1, keepdims=True)
    kn = _dot(hkv, wk_ref[...])
    for h in range(MLA_HEADS):
        x = kn[:, h * HEAD_PAD:(h + 1) * HEAD_PAD]
        ss = jnp.sum(x * x, axis=-1, keepdims=True) + ss_rope
        k_ref[0, h] = ((x * gk + k_rot) * lax.rsqrt(ss * (1.0 / MLA_QK) + EPS)).astype(BF16)

    vt = _dot_nt(wvt_ref[...], hkv)
    ones_tile = jnp.where(lax.broadcasted_iota(jnp.int32, (V_ROWS - MLA_V, vt.shape[1]), 0) == 0, 1.0, 0.0).astype(BF16)
    for h in range(MLA_HEADS):
        vt_ref[0, h, 0:MLA_V, :] = vt[h * MLA_V:(h + 1) * MLA_V].astype(BF16)
        vt_ref[0, h, MLA_V:V_ROWS, :] = ones_tile


def _mla_prep(p_mla, qn, wuqt, kvn, wk, wvt, gq, gk, cos_t, sin_t, c_n, s1_n, s2_n, tm):
    b, l, _ = p_mla.shape
    consts = (qn, wuqt, kvn, wk, wvt, gq, gk)
    half = MLA_ROPE // 2
    return pl.pallas_call(
        _mla_prep_kernel,
        grid=(b, l // tm),
        in_specs=[pl.BlockSpec((1, tm, P_MLA), lambda i, j: (i, j, 0))]
        + [_const_spec(c.shape) for c in consts]
        + [pl.BlockSpec((half, tm), lambda i, j: (0, j)), pl.BlockSpec((half, tm), lambda i, j: (0, j)),
           pl.BlockSpec((tm, HEAD_PAD), lambda i, j: (j, 0)), pl.BlockSpec((tm, HEAD_PAD), lambda i, j: (j, 0)),
           pl.BlockSpec((tm, HEAD_PAD), lambda i, j: (j, 0))],
        out_specs=[pl.BlockSpec((1, MLA_HEADS, HEAD_PAD, tm), lambda i, j: (i, 0, 0, j)),
                   pl.BlockSpec((1, MLA_HEADS, tm, HEAD_PAD), lambda i, j: (i, 0, j, 0)),
                   pl.BlockSpec((1, MLA_HEADS, V_ROWS, tm), lambda i, j: (i, 0, 0, j))],
        out_shape=[jax.ShapeDtypeStruct((b, MLA_HEADS, HEAD_PAD, l), BF16),
                   jax.ShapeDtypeStruct((b, MLA_HEADS, l, HEAD_PAD), BF16),
                   jax.ShapeDtypeStruct((b, MLA_HEADS, V_ROWS, l), BF16)],
        compiler_params=_cparams(("parallel", "parallel")),
        name="mla_prep",
    )(p_mla, *consts, cos_t, sin_t, c_n, s1_n, s2_n)


ATTN_HEADS_PER_STEP = 2
ATTN_LOOKAHEAD = 3


def _mla_attn_kernel(qt_ref, k_ref, vt_ref, o_ref, ot_ref, s_ref, *, tk, lookahead):
    n_kb = k_ref.shape[2] // tk
    tq = qt_ref.shape[3]
    n_slots = lookahead + 1
    total = ATTN_HEADS_PER_STEP * n_kb
    qts = [qt_ref[0, hh] for hh in range(ATTN_HEADS_PER_STEP)]

    def scores(g):
        hh, kb = divmod(g, n_kb)
        return _dot(k_ref[0, hh, kb * tk:(kb + 1) * tk, :], qts[hh])

    for g in range(min(lookahead, total)):
        s_ref[g % n_slots] = scores(g)
    m = acc = None
    for g in range(total):
        hh, kb = divmod(g, n_kb)
        if kb == 0:
            m = jnp.full((1, tq), -jnp.inf, F32)
            acc = jnp.zeros((V_ROWS, tq), F32)
        if g + lookahead < total:
            s_ref[(g + lookahead) % n_slots] = scores(g + lookahead)
        s = s_ref[g % n_slots]
        m_new = jnp.maximum(m, jnp.max(s, axis=0, keepdims=True))
        alpha = jnp.exp2(m - m_new)
        p = jnp.exp2(s - m_new)
        acc = alpha * acc + _dot(vt_ref[0, hh, :, kb * tk:(kb + 1) * tk], p.astype(BF16))
        m = m_new
        if kb == n_kb - 1:
            ot_ref[hh * MLA_V:(hh + 1) * MLA_V, :] = acc[0:MLA_V] * (1.0 / acc[MLA_V:MLA_V + 1])
    o_ref[0] = ot_ref[...].T.astype(BF16)


def _mla_attn(qt, k, vt, tq, tk):
    b, nh, _, l = qt.shape
    hp = ATTN_HEADS_PER_STEP
    lookahead = min(ATTN_LOOKAHEAD, hp * (l // tk) - 1)
    return pl.pallas_call(
        functools.partial(_mla_attn_kernel, tk=tk, lookahead=lookahead),
        grid=(b, nh // hp, l // tq),
        in_specs=[pl.BlockSpec((1, hp, HEAD_PAD, tq), lambda i, h, j: (i, h, 0, j)),
                  pl.BlockSpec((1, hp, l, HEAD_PAD), lambda i, h, j: (i, h, 0, 0)),
                  pl.BlockSpec((1, hp, V_ROWS, l), lambda i, h, j: (i, h, 0, 0))],
        out_specs=pl.BlockSpec((1, tq, hp * MLA_V), lambda i, h, j: (i, j, h)),
        out_shape=jax.ShapeDtypeStruct((b, l, nh * MLA_V), BF16),
        scratch_shapes=[pltpu.VMEM((hp * MLA_V, tq), F32),
                        pltpu.VMEM((lookahead + 1, tk, tq), F32)],
        compiler_params=_cparams(("parallel", "parallel", "arbitrary")),
        name="mla_attn",
    )(qt, k, vt)


def _gla_kernel(qkv_ref, lr_ref, wgh_ref, wgl_ref, bg_ref, cum_ref, tot_ref, ind_ref, vmask_ref, hmask_ref,
                o_ref, st_ref, dec_ref, upd_ref, stb_ref, p_ref, *, tg):
    d = pl.program_id(1)
    t = pl.program_id(2)
    n_chunks = tg // GLA_CHUNK

    @pl.when(t == 0)
    def _():
        st_ref[...] = jnp.zeros_like(st_ref)

    qkv = qkv_ref[0]
    q = qkv[:, :GLA_QK] * (GLA_DK ** -0.5)
    k = qkv[:, GLA_QK:2 * GLA_QK]
    v = qkv[:, 2 * GLA_QK:]

    lr_hi, lr_lo = _split_hi_lo(lr_ref[0])
    zg = _dot(lr_hi, wgh_ref[0]) + _dot(lr_lo, wgh_ref[0]) + _dot(lr_hi, wgl_ref[0]) + bg_ref[0]
    log_a = (jnp.minimum(zg, 0.0) - jnp.log1p(jnp.exp(-jnp.abs(zg)))) * (1.0 / GLA_GATE_NORM)

    la_hi, la_lo = _split_hi_lo(log_a)
    bcum = _dot(cum_ref[0], la_hi) + _dot(cum_ref[0], la_lo)
    btot = _dot(tot_ref[...], la_hi) + _dot(tot_ref[...], la_lo)

    qd = (q * jnp.exp(bcum)).astype(BF16)
    kd = (k * jnp.exp(btot - bcum)).astype(BF16)
    vb = v.astype(BF16)
    dec_ref[...] = jnp.exp(btot)

    fwd = d == 0
    chunk_rows = [slice(c * GLA_CHUNK, (c + 1) * GLA_CHUNK) for c in range(n_chunks)]

    q3 = q.reshape(n_chunks, GLA_CHUNK, GLA_QK)
    k3 = k.reshape(n_chunks, GLA_CHUNK, GLA_QK)
    b3 = bcum.reshape(n_chunks, GLA_CHUNK, GLA_QK)
    tin = lax.broadcasted_iota(jnp.int32, (n_chunks, GLA_CHUNK, GLA_QK), 1)
    tin = jnp.where(fwd, tin, -tin)
    hmask = hmask_ref[...]
    pending = list(enumerate(chunk_rows))
    for j in range(GLA_CHUNK):
        for c, rows in pending[j::GLA_CHUNK]:
            upd_ref[c] = _dot_tn(vb[rows], kd[rows]) * hmask
        valid = tin >= jnp.where(fwd, j, -j)
        e = jnp.exp(jnp.where(valid, b3 - b3[:, j:j + 1, :], NEG_BIG))
        p_ref[:, j * GLA_QK:(j + 1) * GLA_QK] = (q3 * k3[:, j:j + 1, :] * e).reshape(tg, GLA_QK).astype(BF16)

    def body(ci, st):
        c = jnp.where(fwd, ci, n_chunks - 1 - ci)
        stb_ref[c] = st.T.astype(BF16)
        return st * dec_ref[pl.ds(c * GLA_CHUNK, 1), :] + upd_ref[c]

    st_ref[...] = lax.fori_loop(0, n_chunks, body, st_ref[...], unroll=True)
    a_intra = _dot(p_ref[...], ind_ref[...]).astype(BF16)
    vmask = vmask_ref[...]
    outs = []
    for c, rows in enumerate(chunk_rows):
        v_rows = jnp.concatenate([vb[rows]] * GLA_HEADS, axis=0) * vmask
        lhs = jnp.concatenate([qd[rows], a_intra[rows]], axis=1)
        outs.append(_dot(lhs, jnp.concatenate([stb_ref[c], v_rows], axis=0)))
    o_ref[0, 0] = jnp.concatenate(outs, axis=0)


def _gla_scan(qkv, lr, wgh, wgl, bg, cum, tot, ind, vmask, hmask, tg):
    b, l, _ = qkv.shape
    nt = l // tg
    tile = lambda d, t: t + d * (nt - 1 - 2 * t)
    return pl.pallas_call(
        functools.partial(_gla_kernel, tg=tg),
        grid=(b, 2, nt),
        in_specs=[pl.BlockSpec((1, tg, P_QKV), lambda i, d, t: (i, tile(d, t), 0)),
                  pl.BlockSpec((1, tg, P_LR), lambda i, d, t: (i, tile(d, t), 0)),
                  pl.BlockSpec((1, LANES, GLA_QK), lambda i, d, t: (d, 0, 0)),
                  pl.BlockSpec((1, LANES, GLA_QK), lambda i, d, t: (d, 0, 0)),
                  pl.BlockSpec((1, 1, GLA_QK), lambda i, d, t: (d, 0, 0)),
                  pl.BlockSpec((1, tg, tg), lambda i, d, t: (d, 0, 0)),
                  _const_spec(tot.shape), _const_spec(ind.shape), _const_spec(vmask.shape),
                  _const_spec(hmask.shape)],
        out_specs=pl.BlockSpec((1, 1, tg, GLA_V), lambda i, d, t: (d, i, tile(d, t), 0)),
        out_shape=jax.ShapeDtypeStruct((2, b, l, GLA_V), F32),
        scratch_shapes=[pltpu.VMEM((GLA_V, GLA_QK), F32), pltpu.VMEM((tg, GLA_QK), F32),
                        pltpu.VMEM((tg // GLA_CHUNK, GLA_V, GLA_QK), F32),
                        pltpu.VMEM((tg // GLA_CHUNK, GLA_QK, GLA_V), BF16),
                        pltpu.VMEM((tg, GLA_CHUNK * GLA_QK), BF16)],
        compiler_params=_cparams(("parallel", "parallel", "arbitrary")),
        name="gla_scan",
    )(qkv, lr, wgh, wgl, bg, cum, tot, ind, vmask, hmask)


CONV_HALO = SUBLANES
SSD_FRONT_AHEAD = 1
SSD_TILE_CHUNKS = 8


def _softplus(x):
    return jnp.maximum(x, 0.0) + jnp.log1p(jnp.exp(-jnp.abs(x)))


def _ssd_kernel(xc_ref, xp_ref, xn_ref, dt_ref, dtt_ref, cw_ref, cb_ref, bias_r_ref, bias_c_ref,
                alog_r_ref, alog_c_ref, dskip_ref, tri_ref, shift_ref, rep_ref,
                y_ref, xe_ref, st_ref, upd_ref, dec_ref, stb_ref, cgb_ref, e64_ref, *, n_tiles, n_chunks):
    d = pl.program_id(1)
    t = pl.program_id(2)
    tt = t + d * (n_tiles - 1 - 2 * t)
    cs = SSD_CHUNK
    ts = n_chunks * cs
    fwd = d == 0
    hpg = SSD_HEADS // SSD_GROUPS
    gw = hpg * SSD_HEADDIM
    pad = SSD_CONV // 2

    @pl.when(t == 0)
    def _():
        st_ref[...] = jnp.zeros_like(st_ref)

    xe_ref[0:CONV_HALO, :] = jnp.where(tt == 0, 0.0, xp_ref[0])
    xe_ref[CONV_HALO:CONV_HALO + ts, :] = xc_ref[0]
    xe_ref[CONV_HALO + ts:, :] = jnp.where(tt == n_tiles - 1, 0.0, xn_ref[0])

    tri_d = tri_ref[d]
    tri_o = tri_ref[1 - d]
    mask = tri_d.astype(F32) > 0.5
    rep64 = rep_ref[0]
    neg_a_r = -jnp.exp(alog_r_ref[...])
    neg_a_c = -jnp.exp(alog_c_ref[...])
    lane_head = lax.broadcasted_iota(jnp.int32, (cs, gw), 1) // SSD_HEADDIM
    skip_gain = jnp.where(fwd, 1.0, 0.0) * dskip_ref[...]

    def front(c):
        rows = slice(c * cs, (c + 1) * cs)
        xe_b = xe_ref[c * cs:(c + 1) * cs + 2 * CONV_HALO, :].astype(BF16)
        acc = cb_ref[...] + cw_ref[pad:pad + 1, :] * xe_ref[CONV_HALO + c * cs:CONV_HALO + (c + 1) * cs, :]
        for i, kk in enumerate(k for k in range(SSD_CONV) if k != pad):
            acc = acc + cw_ref[kk:kk + 1, :] * _dot(shift_ref[i], xe_b)
        act = acc * jax.nn.sigmoid(acc)
        xs = act[:, :SSD_INNER]
        bm = act[:, SSD_INNER:SSD_INNER + SSD_GROUPS * SSD_STATE]
        cmb = act[:, SSD_INNER + SSD_GROUPS * SSD_STATE:].astype(BF16)

        dt_c = _softplus(dt_ref[0, rows, :] + bias_r_ref[...])
        la_c = dt_c * neg_a_r
        dt_r = _softplus(dtt_ref[0, :, rows] + bias_c_ref[...])
        la_r = dt_r * neg_a_c
        lc_hi, lc_lo = _split_hi_lo(la_c)
        cum_c = _dot(tri_d, lc_hi) + _dot(tri_d, lc_lo)
        lr_hi, lr_lo = _split_hi_lo(la_r)
        cum_r = _dot(lr_hi, tri_o) + _dot(lr_lo, tri_o)
        dt_hi, dt_lo = _split_hi_lo(dt_c)
        dt64 = _dot(dt_hi, rep64) + _dot(dt_lo, rep64)
        cu_hi, cu_lo = _split_hi_lo(cum_c)
        cum64 = _dot(cu_hi, rep64) + _dot(cu_lo, rep64)
        return xs, bm, cmb, cum_r, dt64, cum64

    def back(c, vals):
        xs, bm, cmb, cum_r, dt64, cum64 = vals
        rows = slice(c * cs, (c + 1) * cs)
        tot64 = jnp.where(fwd, cum64[cs - 1:cs, :], cum64[0:1, :])
        xc = xs * dt64
        xcb = xc.astype(BF16)
        xdec = (xc * jnp.exp(tot64 - cum64)).astype(BF16)
        e64_ref[rows, :] = jnp.exp(cum64)
        dec_ref[c] = jnp.exp(tot64)
        cgb_ref[rows, :] = cmb
        ys = []
        for g in range(SSD_GROUPS):
            gl = slice(g * gw, (g + 1) * gw)
            bg = bm[:, g * SSD_STATE:(g + 1) * SSD_STATE]
            cb = _dot_nt(cmb[:, g * SSD_STATE:(g + 1) * SSD_STATE], bg.astype(BF16))
            y_g = jnp.zeros((cs, gw), F32)
            cum_g = cum64[:, gl]
            cum_swapped = pltpu.roll(cum_g, SSD_HEADDIM, axis=1)
            for hg in range(hpg):
                h = g * hpg + hg
                cr = jnp.where(fwd, cum_r[h:h + 1, :], cum_r[SSD_HEADS + h:SSD_HEADS + h + 1, :])
                cum_h = jnp.where(lane_head == hg, cum_g, cum_swapped)
                seg = jnp.exp(jnp.where(mask, cum_h - cr, NEG_BIG))
                yd = _dot((cb * seg).astype(BF16), xcb[:, gl])
                y_g = y_g + jnp.where(lane_head == hg, yd, 0.0)
            ys.append(y_g)
            upd_ref[c, g] = _dot(bg.T.astype(BF16), xdec[:, gl])
        y_ref[0, 0, rows, :] = jnp.concatenate(ys, axis=1) + skip_gain * xs

    ahead = min(SSD_FRONT_AHEAD, n_chunks)
    fronts = [front(c) for c in range(ahead)]
    for c in range(n_chunks):
        if c + ahead < n_chunks:
            fronts.append(front(c + ahead))
        back(c, fronts[c])
        fronts[c] = None

    def body(ci, st):
        c = jnp.where(fwd, ci, n_chunks - 1 - ci)
        dec = dec_ref[c]
        new = []
        for g in range(SSD_GROUPS):
            stb_ref[c, g] = st[g].astype(BF16)
            new.append(st[g] * dec[:, g * gw:(g + 1) * gw] + upd_ref[c, g])
        return tuple(new)

    st = lax.fori_loop(0, n_chunks, body, tuple(st_ref[g] for g in range(SSD_GROUPS)), unroll=True)
    for g in range(SSD_GROUPS):
        st_ref[g] = st[g]

    for c in range(n_chunks):
        rows = slice(c * cs, (c + 1) * cs)
        y_off = jnp.concatenate(
            [_dot(cgb_ref[rows, g * SSD_STATE:(g + 1) * SSD_STATE], stb_ref[c, g]) for g in range(SSD_GROUPS)], axis=1)
        y_ref[0, 0, rows, :] = y_ref[0, 0, rows, :] + e64_ref[rows, :] * y_off


def _ssd_scan(xbc, dt, dtt, cw, cb, bias_r, bias_c, alog_r, alog_c, dskip, tri, shift, rep, n_chunks):
    b, l, _ = xbc.shape
    ts = n_chunks * SSD_CHUNK
    nt = l // ts
    hb = ts // CONV_HALO
    n_hb = l // CONV_HALO
    gw = SSD_INNER // SSD_GROUPS
    tile = lambda d, t: t + d * (nt - 1 - 2 * t)
    consts = (cw, cb, bias_r, bias_c, alog_r, alog_c, dskip, tri, shift)
    return pl.pallas_call(
        functools.partial(_ssd_kernel, n_tiles=nt, n_chunks=n_chunks),
        grid=(b, 2, nt),
        in_specs=[pl.BlockSpec((1, ts, P_XBC), lambda i, d, t: (i, tile(d, t), 0)),
                  pl.BlockSpec((1, CONV_HALO, P_XBC), lambda i, d, t: (i, jnp.maximum(tile(d, t) * hb - 1, 0), 0)),
                  pl.BlockSpec((1, CONV_HALO, P_XBC),
                               lambda i, d, t: (i, jnp.minimum((tile(d, t) + 1) * hb, n_hb - 1), 0)),
                  pl.BlockSpec((1, ts, P_DT), lambda i, d, t: (i, tile(d, t), 0)),
                  pl.BlockSpec((1, 2 * SSD_HEADS, ts), lambda i, d, t: (i, 0, tile(d, t)))]
        + [_const_spec(c.shape) for c in consts]
        + [pl.BlockSpec((1,) + rep.shape[1:], lambda i, d, t: (d, 0, 0))],
        out_specs=pl.BlockSpec((1, 1, ts, SSD_INNER), lambda i, d, t: (d, i, tile(d, t), 0)),
        out_shape=jax.ShapeDtypeStruct((2, b, l, SSD_INNER), F32),
        scratch_shapes=[pltpu.VMEM((ts + 2 * CONV_HALO, P_XBC), F32),
                        pltpu.VMEM((SSD_GROUPS, SSD_STATE, gw), F32),
                        pltpu.VMEM((n_chunks, SSD_GROUPS, SSD_STATE, gw), F32),
                        pltpu.VMEM((n_chunks, 1, SSD_INNER), F32),
                        pltpu.VMEM((n_chunks, SSD_GROUPS, SSD_STATE, gw), BF16),
                        pltpu.VMEM((ts, SSD_GROUPS * SSD_STATE), BF16),
                        pltpu.VMEM((ts, SSD_INNER), F32)],
        compiler_params=_cparams(("parallel", "parallel", "arbitrary")),
        name="ssd_scan",
    )(xbc, xbc, xbc, dt, dtt, *consts, rep)


def _pad_cols(w, width, at=0):
    out = jnp.zeros(w.shape[:-1] + (width,), w.dtype)
    return out.at[..., at:at + w.shape[-1]].set(w)


def _prep_layer(i, ffn1_norm, ffn1_w_gu, ffn1_w_down, mix_norm, w_in, mla_q_norm, mla_w_uq, mla_kv_norm,
                mla_w_ukv, mla_q_gain, mla_k_gain, gla_w_gate, gla_b_gate, gla_o_norm, ssd_conv_w, ssd_conv_b,
                ssd_a_log, ssd_dt_bias, ssd_d, ssd_norm, w_out, ffn2_norm, ffn2_w_gu, ffn2_w_down, final_norm):
    def ffn_w(w_gu, w_down):
        wg = w_gu[:, :D_FF].reshape(D_MODEL, N_FF_CHUNKS, FF_CHUNK).transpose(1, 0, 2).astype(BF16)
        wu = w_gu[:, D_FF:].reshape(D_MODEL, N_FF_CHUNKS, FF_CHUNK).transpose(1, 0, 2).astype(BF16)
        wd = w_down.astype(BF16)
        return wg, wu, wd

    w = {}
    w["n1"] = ffn1_norm[i][None, :]
    w["wg1"], w["wu1"], w["wd1"] = ffn_w(ffn1_w_gu[i], ffn1_w_down[i])
    w["nmix"] = mix_norm[i][None, :]

    wi = w_in[i]
    o_gla = MLA_IN
    o_ssd = MLA_IN + GLA_IN
    o_mla_kr = MLA_Q_LORA + MLA_KV_LORA
    gla_qkv_w = 2 * GLA_QK + GLA_V
    cols = [
        wi[:, :o_mla_kr],
        _pad_cols(wi[:, o_mla_kr:MLA_IN], HEAD_PAD, at=MLA_NOPE),
        wi[:, o_gla:o_gla + gla_qkv_w],
        _pad_cols(wi[:, o_gla + gla_qkv_w + GLA_V:o_ssd], P_LR),
        wi[:, o_gla + gla_qkv_w:o_gla + gla_qkv_w + GLA_V],
        wi[:, o_ssd:o_ssd + SSD_INNER],
        wi[:, o_ssd + SSD_INNER:o_ssd + SSD_INNER + SSD_CONV_DIM],
        _pad_cols(wi[:, o_ssd + SSD_INNER + SSD_CONV_DIM:], P_DT),
    ]
    w["win"] = jnp.concatenate(cols, axis=1).astype(BF16)

    w["qn"] = mla_q_norm[i][None, :]
    w["kvn"] = mla_kv_norm[i][None, :]
    wuq = _pad_cols(mla_w_uq[i].reshape(MLA_Q_LORA, MLA_HEADS, MLA_QK), HEAD_PAD)
    w["wuqt"] = wuq.reshape(MLA_Q_LORA, MLA_HEADS * HEAD_PAD).T.astype(BF16)
    wukv = mla_w_ukv[i].reshape(MLA_KV_LORA, MLA_HEADS, MLA_NOPE + MLA_V)
    w["wk"] = _pad_cols(wukv[..., :MLA_NOPE], HEAD_PAD).reshape(MLA_KV_LORA, MLA_HEADS * HEAD_PAD).astype(BF16)
    w["wvt"] = wukv[..., MLA_NOPE:].reshape(MLA_KV_LORA, MLA_HEADS * MLA_V).T.astype(BF16)
    w["gq"] = (_pad_cols(mla_q_gain[i], HEAD_PAD) * (MLA_QK ** -0.5 * LOG2E))[:, None]
    w["gk"] = _pad_cols(mla_k_gain[i], HEAD_PAD)[None, :]

    wgate = jnp.zeros((2, LANES, GLA_QK), F32)
    for zdir in range(2):
        wgate = wgate.at[zdir, zdir * GLA_GATE_RANK:(zdir + 1) * GLA_GATE_RANK, :].set(gla_w_gate[i, zdir])
    w["wgh"] = wgate.astype(BF16)
    w["wgl"] = (wgate - w["wgh"].astype(F32)).astype(BF16)
    w["bg"] = gla_b_gate[i][:, None, :]
    w["onorm"] = jnp.tile(gla_o_norm[i], GLA_HEADS)[None, :]

    w["cw"] = jnp.zeros((SUBLANES, SSD_CONV_DIM), F32).at[:SSD_CONV].set(ssd_conv_w[i])
    w["cb"] = ssd_conv_b[i][None, :]
    flat_bias = ssd_dt_bias[i].reshape(2 * SSD_HEADS)
    flat_alog = ssd_a_log[i].reshape(2 * SSD_HEADS)
    w["bias_r"] = _pad_cols(flat_bias, P_DT)[None, :]
    w["bias_c"] = flat_bias[:, None]
    w["alog_r"] = _pad_cols(flat_alog, P_DT)[None, :]
    w["alog_c"] = flat_alog[:, None]
    w["dskip"] = jnp.repeat(ssd_d[i], SSD_HEADDIM)[None, :]
    w["snorm"] = ssd_norm[i][None, :]

    w["wout"] = w_out[i].astype(BF16)
    w["n2"] = ffn2_norm[i][None, :]
    w["wg2"], w["wu2"], w["wd2"] = ffn_w(ffn2_w_gu[i], ffn2_w_down[i])
    w["fn"] = final_norm[i][None, :]
    return w


def _block_ones(n, blk):
    idx = np.arange(n) // blk
    return jnp.asarray(idx[:, None] == idx[None, :], BF16)


def _seq_consts(l, tg):
    half = MLA_ROPE // 2
    pos = jnp.arange(l, dtype=F32)
    inv_freq = 1.0 / (ROPE_BASE ** (jnp.arange(0, MLA_ROPE, 2, dtype=F32) / MLA_ROPE))
    ang = pos[:, None] * inv_freq[None, :]
    cos, sin = jnp.cos(ang), jnp.sin(ang)
    c = {}
    c["cos_t"], c["sin_t"] = cos.T, sin.T
    ones = jnp.ones((l, HEAD_PAD), F32)
    c["c_n"] = ones.at[:, MLA_NOPE:MLA_NOPE + half].set(cos).at[:, MLA_NOPE + half:MLA_QK].set(cos)
    zeros = jnp.zeros((l, HEAD_PAD), F32)
    c["s1_n"] = zeros.at[:, MLA_NOPE:MLA_NOPE + half].set(-sin)
    c["s2_n"] = zeros.at[:, MLA_NOPE + half:MLA_QK].set(sin)

    r = np.arange(tg)
    same = (r[:, None] // GLA_CHUNK) == (r[None, :] // GLA_CHUNK)
    c["gla_cum"] = jnp.asarray(np.stack([same & (r[None, :] <= r[:, None]), same & (r[None, :] >= r[:, None])]), BF16)
    c["gla_tot"] = jnp.asarray(same, BF16)
    qk_head = np.arange(GLA_QK) // GLA_DK
    v_head = np.arange(GLA_V) // GLA_DV
    p_col = np.arange(GLA_CHUNK * GLA_QK)
    a_col = np.arange(GLA_HEADS * GLA_CHUNK)
    c["gla_ind"] = jnp.asarray(((p_col % GLA_QK) // GLA_DK * GLA_CHUNK + p_col // GLA_QK)[:, None] == a_col[None, :], BF16)
    c["gla_vmask"] = jnp.asarray((a_col // GLA_CHUNK)[:, None] == v_head[None, :], BF16)
    c["gla_hmask"] = jnp.asarray(v_head[:, None] == qk_head[None, :], F32)

    q = np.arange(SSD_CHUNK)
    c["ssd_tri"] = jnp.asarray(np.stack([q[None, :] <= q[:, None], q[None, :] >= q[:, None]]), BF16)
    xe_row = np.arange(SSD_CHUNK + 2 * CONV_HALO)
    offs = [k - SSD_CONV // 2 for k in range(SSD_CONV) if k != SSD_CONV // 2]
    c["ssd_shift"] = jnp.asarray(np.stack([xe_row[None, :] == (q[:, None] + CONV_HALO + o) for o in offs]), BF16)
    src_lane = np.arange(P_DT)
    c["ssd_rep"] = jnp.asarray(np.stack(
        [src_lane[:, None] == (zdir * SSD_HEADS + np.arange(SSD_INNER) // SSD_HEADDIM)[None, :] for zdir in range(2)]), BF16)
    return c


def _pick_tile(n, pref):
    t = min(n, pref)
    while n % t:
        t //= 2
    return t


def _layer(x, w, c, tg):
    b, l, _ = x.shape
    t = b * l
    tm = _pick_tile(t, 512)
    x1, p_mla, p_qkv, p_lr, p_g, p_z, p_xbc, p_dt = _ffn_in(
        x.reshape(t, D_MODEL), w["n1"], w["wg1"], w["wu1"], w["wd1"], w["nmix"], w["win"], tm)

    qt, k, vt = _mla_prep(p_mla.reshape(b, l, P_MLA), w["qn"], w["wuqt"], w["kvn"], w["wk"], w["wvt"],
                          w["gq"], w["gk"], c["cos_t"], c["sin_t"], c["c_n"], c["s1_n"], c["s2_n"],
                          _pick_tile(l, 512))
    o_mla = _mla_attn(qt, k, vt, _pick_tile(l, 512), _pick_tile(l, 256))

    o_gla = _gla_scan(p_qkv.reshape(b, l, P_QKV), p_lr.reshape(b, l, P_LR), w["wgh"], w["wgl"], w["bg"],
                      c["gla_cum"], c["gla_tot"], c["gla_ind"], c["gla_vmask"], c["gla_hmask"], tg)

    dt3 = p_dt.reshape(b, l, P_DT)
    dtt = jnp.swapaxes(dt3[:, :, :2 * SSD_HEADS], 1, 2)
    y_ssd = _ssd_scan(p_xbc.reshape(b, l, P_XBC), dt3, dtt, w["cw"], w["cb"], w["bias_r"], w["bias_c"],
                      w["alog_r"], w["alog_c"], w["dskip"], c["ssd_tri"], c["ssd_shift"], c["ssd_rep"],
                      _pick_tile(l // SSD_CHUNK, SSD_TILE_CHUNKS))

    y = _ffn_out(x1, o_mla.reshape(t, MLA_HEADS * MLA_V), o_gla.reshape(2, t, GLA_V), p_g,
                 y_ssd.reshape(2, t, SSD_INNER), p_z, w["onorm"], w["snorm"],
                 _block_ones(GLA_V, GLA_DV), _block_ones(SSD_INNER, SSD_INNER // SSD_GROUPS),
                 w["wout"], w["n2"], w["wg2"], w["wu2"], w["wd2"], w["fn"], tm)
    return y.reshape(b, l, D_MODEL)


def kernel(x_prompt, x_sample, ffn1_norm, ffn1_w_gu, ffn1_w_down, mix_norm, w_in, mla_q_norm, mla_w_uq,
           mla_kv_norm, mla_w_ukv, mla_q_gain, mla_k_gain, gla_w_gate, gla_b_gate, gla_o_norm, ssd_conv_w,
           ssd_conv_b, ssd_a_log, ssd_dt_bias, ssd_d, ssd_norm, w_out, ffn2_norm, ffn2_w_gu, ffn2_w_down,
           final_norm):
    params = (ffn1_norm, ffn1_w_gu, ffn1_w_down, mix_norm, w_in, mla_q_norm, mla_w_uq, mla_kv_norm,
              mla_w_ukv, mla_q_gain, mla_k_gain, gla_w_gate, gla_b_gate, gla_o_norm, ssd_conv_w, ssd_conv_b,
              ssd_a_log, ssd_dt_bias, ssd_d, ssd_norm, w_out, ffn2_norm, ffn2_w_gu, ffn2_w_down, final_norm)
    depth = ffn1_norm.shape[0]
    streams = [x_prompt, x_sample]
    tgs = [_pick_tile(s.shape[1], 512) for s in streams]
    consts = [_seq_consts(s.shape[1], tg) for s, tg in zip(streams, tgs)]
    for i in range(depth):
        w = _prep_layer(i, *params)
        streams = [_layer(s, w, c, tg) for s, c, tg in zip(streams, consts, tgs)]
    return tuple(streams)
```

```python
import functools
import itertools

import jax
import jax.numpy as jnp
import numpy as np
from jax import lax
from jax.experimental import pallas as pl
from jax.experimental.pallas import tpu as pltpu

F32 = jnp.float32
BF16 = jnp.bfloat16

D_MODEL = 1024
D_FF = 2816
EPS = 1e-6
MLA_HEADS = 8
MLA_Q_LORA = 384
MLA_KV_LORA = 256
MLA_NOPE = 64
MLA_ROPE = 32
MLA_QK = MLA_NOPE + MLA_ROPE
MLA_V = 64
ROPE_BASE = 10000.0
GLA_HEADS = 4
GLA_DK = 32
GLA_DV = 64
GLA_GATE_RANK = 16
GLA_GATE_NORM = 16.0
GLA_CHUNK = 16
SSD_HEADS = 4
SSD_HEADDIM = 64
SSD_INNER = SSD_HEADS * SSD_HEADDIM
SSD_GROUPS = 2
SSD_STATE = 128
SSD_CONV = 5
SSD_CHUNK = 128
SSD_CONV_DIM = SSD_INNER + 2 * SSD_GROUPS * SSD_STATE
MLA_IN = MLA_Q_LORA + MLA_KV_LORA + MLA_ROPE
GLA_IN = 2 * GLA_HEADS * GLA_DK + 2 * GLA_HEADS * GLA_DV + 2 * GLA_GATE_RANK
SSD_IN = SSD_INNER + SSD_CONV_DIM + 2 * SSD_HEADS

LANES = 128
SUBLANES = 8
MXU_DIM = 256
VMEM_LIMIT = 56 * 1024 * 1024

HEAD_PAD = LANES
V_ROWS = MLA_V + 16
GLA_QK = GLA_HEADS * GLA_DK
GLA_V = GLA_HEADS * GLA_DV
FF_CHUNK = MXU_DIM
N_FF_CHUNKS = D_FF // FF_CHUNK

P_MLA = MLA_Q_LORA + MLA_KV_LORA + HEAD_PAD
P_QKV = 2 * GLA_QK + GLA_V
P_LR = LANES
P_G = GLA_V
P_Z = SSD_INNER
P_XBC = SSD_CONV_DIM
P_DT = LANES
P_TOTAL = P_MLA + P_QKV + P_LR + P_G + P_Z + P_XBC + P_DT

NEG_BIG = -1e30
LOG2E = 1.4426950408889634


def _rms(x, gain):
    return x * lax.rsqrt(jnp.mean(x * x, axis=-1, keepdims=True) + EPS) * gain


def _split_hi_lo(x):
    hi = x.astype(BF16)
    lo = (x - hi.astype(F32)).astype(BF16)
    return hi, lo


def _dot(a, b):
    return jnp.dot(a, b, preferred_element_type=F32)


def _dot_nt(a, b):
    return lax.dot_general(a, b, (((1,), (1,)), ((), ())), preferred_element_type=F32)


def _dot_tn(a, b):
    return lax.dot_general(a, b, (((0,), (0,)), ((), ())), preferred_element_type=F32)


def _cparams(sem):
    return pltpu.CompilerParams(dimension_semantics=sem, vmem_limit_bytes=VMEM_LIMIT)


def _const_spec(shape):
    nd = len(shape)
    return pl.BlockSpec(shape, lambda *_: (0,) * nd, pipeline_mode=pl.Buffered(1))


def _swiglu(h_ref, wg_ref, wu_ref, wd_ref, a_ref):
    for c in range(N_FF_CHUNKS):
        h = h_ref[...]
        g = _dot(h, wg_ref[c])
        u = _dot(h, wu_ref[c])
        a_ref[:, c * FF_CHUNK:(c + 1) * FF_CHUNK] = (g * jax.nn.sigmoid(g) * u).astype(BF16)
    return _dot(a_ref[...], wd_ref[...])


def _ffn_in_kernel(x_ref, n1_ref, wg_ref, wu_ref, wd_ref, n2_ref, win_ref,
                   x1_ref, mla_ref, qkv_ref, lr_ref, g_ref, z_ref, xbc_ref, dt_ref,
                   h_ref, a_ref):
    x = x_ref[...]
    h_ref[...] = _rms(x, n1_ref[...]).astype(BF16)
    x1 = x + 0.5 * _swiglu(h_ref, wg_ref, wu_ref, wd_ref, a_ref)
    x1_ref[...] = x1
    h2 = _rms(x1, n2_ref[...]).astype(BF16)
    off = 0
    for ref, width in ((mla_ref, P_MLA), (qkv_ref, P_QKV), (lr_ref, P_LR), (g_ref, P_G),
                       (z_ref, P_Z), (xbc_ref, P_XBC), (dt_ref, P_DT)):
        ref[...] = _dot(h2, win_ref[:, off:off + width])
        off += width


def _ffn_in(x, n1, wg, wu, wd, n2, win, tm):
    t = x.shape[0]
    widths = (D_MODEL, P_MLA, P_QKV, P_LR, P_G, P_Z, P_XBC, P_DT)
    tok = lambda w: pl.BlockSpec((tm, w), lambda i: (i, 0))
    return pl.pallas_call(
        _ffn_in_kernel,
        grid=(t // tm,),
        in_specs=[tok(D_MODEL), _const_spec(n1.shape), _const_spec(wg.shape), _const_spec(wu.shape),
                  _const_spec(wd.shape), _const_spec(n2.shape), _const_spec(win.shape)],
        out_specs=[tok(w) for w in widths],
        out_shape=[jax.ShapeDtypeStruct((t, w), F32) for w in widths],
        scratch_shapes=[pltpu.VMEM((tm, D_MODEL), BF16), pltpu.VMEM((tm, D_FF), BF16)],
        compiler_params=_cparams(("parallel",)),
        name="ffn_in",
    )(x, n1, wg, wu, wd, n2, win)


def _ffn_out_kernel(x1_ref, omla_ref, ogf_ref, ogb_ref, g_ref, yssd_ref, z_ref,
                    onorm_ref, snorm_ref, blk64_ref, blk128_ref, wout_ref,
                    n_ref, wg_ref, wu_ref, wd_ref, fn_ref,
                    y_ref, h_ref, a_ref):
    og = ogf_ref[...] + ogb_ref[...]
    ss = _dot((og * og).astype(BF16), blk64_ref[...])
    g = g_ref[...]
    m_gla = og * lax.rsqrt(ss * (1.0 / GLA_DV) + EPS) * onorm_ref[...] * (g * jax.nn.sigmoid(g))
    z = z_ref[...]
    ys = (yssd_ref[0] + yssd_ref[1]) * (z * jax.nn.sigmoid(z))
    ss2 = _dot((ys * ys).astype(BF16), blk128_ref[...])
    m_ssd = ys * lax.rsqrt(ss2 * (1.0 / (SSD_INNER // SSD_GROUPS)) + EPS) * snorm_ref[...]
    m = jnp.concatenate([omla_ref[...], m_gla.astype(BF16), m_ssd.astype(BF16)], axis=-1)
    x2 = x1_ref[...] + _dot(m, wout_ref[...])
    h_ref[...] = _rms(x2, n_ref[...]).astype(BF16)
    x3 = x2 + 0.5 * _swiglu(h_ref, wg_ref, wu_ref, wd_ref, a_ref)
    y_ref[...] = _rms(x3, fn_ref[...])


def _ffn_out(x1, omla, ogf, ogb, g, yssd, z, onorm, snorm, blk64, blk128, wout, n, wg, wu, wd, fn, tm):
    t = x1.shape[0]
    tok = lambda w: pl.BlockSpec((tm, w), lambda i: (i, 0))
    tok2 = lambda w: pl.BlockSpec((2, tm, w), lambda i: (0, i, 0))
    consts = (onorm, snorm, blk64, blk128, wout, n, wg, wu, wd, fn)
    return pl.pallas_call(
        _ffn_out_kernel,
        grid=(t // tm,),
        in_specs=[tok(D_MODEL), tok(MLA_HEADS * MLA_V), tok(GLA_V), tok(GLA_V), tok(GLA_V), tok2(SSD_INNER),
                  tok(SSD_INNER)]
        + [_const_spec(c.shape) for c in consts],
        out_specs=tok(D_MODEL),
        out_shape=jax.ShapeDtypeStruct((t, D_MODEL), F32),
        scratch_shapes=[pltpu.VMEM((tm, D_MODEL), BF16), pltpu.VMEM((tm, D_FF), BF16)],
        compiler_params=_cparams(("parallel",)),
        name="ffn_out",
    )(x1, omla, ogf, ogb, g, yssd, z, *consts)


def _mla_prep_kernel(p_ref, qn_ref, wuqt_ref, kvn_ref, wk_ref, wvt_ref, gq_ref, gk_ref,
                     cost_ref, sint_ref, cn_ref, s1_ref, s2_ref,
                     qt_ref, k_ref, vt_ref):
    p = p_ref[0]
    cq = p[:, :MLA_Q_LORA]
    ckv = p[:, MLA_Q_LORA:MLA_Q_LORA + MLA_KV_LORA]
    kr = p[:, MLA_Q_LORA + MLA_KV_LORA:]
    hq = _rms(cq, qn_ref[...]).astype(BF16)
    hkv = _rms(ckv, kvn_ref[...]).astype(BF16)

    qt = _dot_nt(wuqt_ref[...], hq)
    cos_t = cost_ref[...]
    sin_t = sint_ref[...]
    gq = gq_ref[...]
    half = MLA_ROPE // 2
    for h in range(MLA_HEADS):
        x = qt[h * HEAD_PAD:(h + 1) * HEAD_PAD]
        ss = jnp.sum(x * x, axis=0, keepdims=True)
        x = x * lax.rsqrt(ss * (1.0 / MLA_QK) + EPS) * gq
        x1 = x[MLA_NOPE:MLA_NOPE + half]
        x2 = x[MLA_NOPE + half:MLA_QK]
        qt_ref[0, h, 0:MLA_NOPE, :] = x[0:MLA_NOPE].astype(BF16)
        qt_ref[0, h, MLA_NOPE:MLA_NOPE + half, :] = (x1 * cos_t - x2 * sin_t).astype(BF16)
        qt_ref[0, h, MLA_NOPE + half:MLA_QK, :] = (x1 * sin_t + x2 * cos_t).astype(BF16)
        qt_ref[0, h, MLA_QK:HEAD_PAD, :] = x[MLA_QK:HEAD_PAD].astype(BF16)

    gk = gk_ref[...]
    krg = kr * gk
    k_rot = (krg * cn_ref[...] + pltpu.roll(krg, HEAD_PAD - half, axis=1) * s1_ref[...]
             + pltpu.roll(krg, half, axis=1) * s2_ref[...])
    ss_rope = jnp.sum(kr * kr, axis=-1, keepdims=True)
    kn = _dot(hkv, wk_ref[...])
    for h in range(MLA_HEADS):
        x = kn[:, h * HEAD_PAD:(h + 1) * HEAD_PAD]
        ss = jnp.sum(x * x, axis=-1, keepdims=True) + ss_rope
        k_ref[0, h] = ((x * gk + k_rot) * lax.rsqrt(ss * (1.0 / MLA_QK) + EPS)).astype(BF16)

    vt = _dot_nt(wvt_ref[...], hkv)
    ones_tile = jnp.where(lax.broadcasted_iota(jnp.int32, (V_ROWS - MLA_V, vt.shape[1]), 0) == 0, 1.0, 0.0).astype(BF16)
    for h in range(MLA_HEADS):
        vt_ref[0, h, 0:MLA_V, :] = vt[h * MLA_V:(h + 1) * MLA_V].astype(BF16)
        vt_ref[0, h, MLA_V:V_ROWS, :] = ones_tile


def _mla_prep(p_mla, qn, wuqt, kvn, wk, wvt, gq, gk, cos_t, sin_t, c_n, s1_n, s2_n, tm):
    b, l, _ = p_mla.shape
    consts = (qn, wuqt, kvn, wk, wvt, gq, gk)
    half = MLA_ROPE // 2
    return pl.pallas_call(
        _mla_prep_kernel,
        grid=(b, l // tm),
        in_specs=[pl.BlockSpec((1, tm, P_MLA), lambda i, j: (i, j, 0))]
        + [_const_spec(c.shape) for c in consts]
        + [pl.BlockSpec((half, tm), lambda i, j: (0, j)), pl.BlockSpec((half, tm), lambda i, j: (0, j)),
           pl.BlockSpec((tm, HEAD_PAD), lambda i, j: (j, 0)), pl.BlockSpec((tm, HEAD_PAD), lambda i, j: (j, 0)),
           pl.BlockSpec((tm, HEAD_PAD), lambda i, j: (j, 0))],
        out_specs=[pl.BlockSpec((1, MLA_HEADS, HEAD_PAD, tm), lambda i, j: (i, 0, 0, j)),
                   pl.BlockSpec((1, MLA_HEADS, tm, HEAD_PAD), lambda i, j: (i, 0, j, 0)),
                   pl.BlockSpec((1, MLA_HEADS, V_ROWS, tm), lambda i, j: (i, 0, 0, j))],
        out_shape=[jax.ShapeDtypeStruct((b, MLA_HEADS, HEAD_PAD, l), BF16),
                   jax.ShapeDtypeStruct((b, MLA_HEADS, l, HEAD_PAD), BF16),
                   jax.ShapeDtypeStruct((b, MLA_HEADS, V_ROWS, l), BF16)],
        compiler_params=_cparams(("parallel", "parallel")),
        name="mla_prep",
    )(p_mla, *consts, cos_t, sin_t, c_n, s1_n, s2_n)


ATTN_HEADS_PER_STEP = 2
ATTN_LOOKAHEAD = 3


def _mla_attn_kernel(qt_ref, k_ref, vt_ref, o_ref, ot_ref, s_ref, *, tk, lookahead):
    n_kb = k_ref.shape[2] // tk
    tq = qt_ref.shape[3]
    n_slots = lookahead + 1
    total = ATTN_HEADS_PER_STEP * n_kb
    qts = [qt_ref[0, hh] for hh in range(ATTN_HEADS_PER_STEP)]

    def scores(g):
        hh, kb = divmod(g, n_kb)
        return _dot(k_ref[0, hh, kb * tk:(kb + 1) * tk, :], qts[hh])

    for g in range(min(lookahead, total)):
        s_ref[g % n_slots] = scores(g)
    m = acc = None
    for g in range(total):
        hh, kb = divmod(g, n_kb)
        if kb == 0:
            m = jnp.full((1, tq), -jnp.inf, F32)
            acc = jnp.zeros((V_ROWS, tq), F32)
        if g + lookahead < total:
            s_ref[(g + lookahead) % n_slots] = scores(g + lookahead)
        s = s_ref[g % n_slots]
        m_new = jnp.maximum(m, jnp.max(s, axis=0, keepdims=True))
        alpha = jnp.exp2(m - m_new)
        p = jnp.exp2(s - m_new)
        acc = alpha * acc + _dot(vt_ref[0, hh, :, kb * tk:(kb + 1) * tk], p.astype(BF16))
        m = m_new
        if kb == n_kb - 1:
            ot_ref[hh * MLA_V:(hh + 1) * MLA_V, :] = acc[0:MLA_V] * (1.0 / acc[MLA_V:MLA_V + 1])
    o_ref[0] = ot_ref[...].T.astype(BF16)


def _mla_attn(qt, k, vt, tq, tk):
    b, nh, _, l = qt.shape
    hp = ATTN_HEADS_PER_STEP
    lookahead = min(ATTN_LOOKAHEAD, hp * (l // tk) - 1)
    return pl.pallas_call(
        functools.partial(_mla_attn_kernel, tk=tk, lookahead=lookahead),
        grid=(b, nh // hp, l // tq),
        in_specs=[pl.BlockSpec((1, hp, HEAD_PAD, tq), lambda i, h, j: (i, h, 0, j)),
                  pl.BlockSpec((1, hp, l, HEAD_PAD), lambda i, h, j: (i, h, 0, 0)),
                  pl.BlockSpec((1, hp, V_ROWS, l), lambda i, h, j: (i, h, 0, 0))],
        out_specs=pl.BlockSpec((1, tq, hp * MLA_V), lambda i, h, j: (i, j, h)),
        out_shape=jax.ShapeDtypeStruct((b, l, nh * MLA_V), BF16),
        scratch_shapes=[pltpu.VMEM((hp * MLA_V, tq), F32),
                        pltpu.VMEM((lookahead + 1, tk, tq), F32)],
        compiler_params=_cparams(("parallel", "parallel", "arbitrary")),
        name="mla_attn",
    )(qt, k, vt)


def _gla_kernel(qkv_f_ref, qkv_b_ref, lr_f_ref, lr_b_ref, wgh_ref, wgl_ref, bg_ref, cum_ref, tot_ref, ind_ref,
                vmask_ref, hmask_ref, o_f_ref, o_b_ref, st_ref, upd_ref, p_ref, *, tg):
    n_chunks = tg // GLA_CHUNK
    chunk_rows = [slice(c * GLA_CHUNK, (c + 1) * GLA_CHUNK) for c in range(n_chunks)]
    hmask = hmask_ref[...]
    vkeep = vmask_ref[...] > 0

    @pl.when(pl.program_id(1) == 0)
    def _():
        st_ref[...] = jnp.zeros_like(st_ref)

    def head(d, qkv_ref, lr_ref):
        qkv = qkv_ref[0]
        q = qkv[:, :GLA_QK] * (GLA_DK ** -0.5)
        k = qkv[:, GLA_QK:2 * GLA_QK]
        v = qkv[:, 2 * GLA_QK:]
        lr_hi, lr_lo = _split_hi_lo(lr_ref[0])
        zg = _dot(lr_hi, wgh_ref[d]) + _dot(lr_lo, wgh_ref[d]) + _dot(lr_hi, wgl_ref[d]) + bg_ref[d]
        log_a = (jnp.minimum(zg, 0.0) - jnp.log1p(jnp.exp(-jnp.abs(zg)))) * (1.0 / GLA_GATE_NORM)
        la_hi, la_lo = _split_hi_lo(log_a)
        bcum = _dot(cum_ref[d], la_hi) + _dot(cum_ref[d], la_lo)
        btot = _dot(tot_ref[...], la_hi) + _dot(tot_ref[...], la_lo)
        return dict(q=q, k=k, bcum=bcum, qd=(q * jnp.exp(bcum)).astype(BF16),
                    kd=(k * jnp.exp(btot - bcum)).astype(BF16), vb=v.astype(BF16), dec=jnp.exp(btot))

    def products(d, h):
        q3 = h["q"].reshape(n_chunks, GLA_CHUNK, GLA_QK)
        k3 = h["k"].reshape(n_chunks, GLA_CHUNK, GLA_QK)
        b3 = h["bcum"].reshape(n_chunks, GLA_CHUNK, GLA_QK)
        tin = lax.broadcasted_iota(jnp.int32, (n_chunks, GLA_CHUNK, GLA_QK), 1)
        pending = list(enumerate(chunk_rows))
        for j in range(GLA_CHUNK):
            for c, rows in pending[j::GLA_CHUNK]:
                upd_ref[d, c] = _dot_tn(h["vb"][rows], h["kd"][rows]) * hmask
            valid = (tin >= j) if d == 0 else (tin <= j)
            e = jnp.exp(jnp.where(valid, b3 - b3[:, j:j + 1, :], NEG_BIG))
            p_ref[d, :, j * GLA_QK:(j + 1) * GLA_QK] = (
                q3 * k3[:, j:j + 1, :] * e).reshape(tg, GLA_QK).astype(BF16)

    def outputs(d, h, o_ref):
        a_intra = _dot(p_ref[d], ind_ref[...]).astype(BF16)
        st = st_ref[d]
        for c in (range(n_chunks) if d == 0 else reversed(range(n_chunks))):
            rows = chunk_rows[c]
            v_tiled = jnp.concatenate([h["vb"][rows]] * GLA_HEADS, axis=0)
            v_rows = jnp.where(vkeep, v_tiled, jnp.zeros_like(v_tiled))
            lhs = jnp.concatenate([h["qd"][rows], a_intra[rows]], axis=1)
            o_ref[0, rows, :] = _dot(lhs, jnp.concatenate([st.T.astype(BF16), v_rows], axis=0))
            st = st * h["dec"][c * GLA_CHUNK:c * GLA_CHUNK + 1, :] + upd_ref[d, c]
            yield
        st_ref[d] = st

    h_f = head(0, qkv_f_ref, lr_f_ref)
    h_b = head(1, qkv_b_ref, lr_b_ref)
    products(0, h_f)
    products(1, h_b)
    for _ in itertools.zip_longest(outputs(0, h_f, o_f_ref), outputs(1, h_b, o_b_ref)):
        pass


def _gla_scan(qkv, lr, wgh, wgl, bg, cum, tot, ind, vmask, hmask, tg):
    b, l, _ = qkv.shape
    nt = l // tg
    n_chunks = tg // GLA_CHUNK
    fwd = lambda i, t: (i, t, 0)
    bwd = lambda i, t: (i, nt - 1 - t, 0)
    consts = (wgh, wgl, bg, cum, tot, ind, vmask, hmask)
    return pl.pallas_call(
        functools.partial(_gla_kernel, tg=tg),
        grid=(b, nt),
        in_specs=[pl.BlockSpec((1, tg, P_QKV), fwd), pl.BlockSpec((1, tg, P_QKV), bwd),
                  pl.BlockSpec((1, tg, P_LR), fwd), pl.BlockSpec((1, tg, P_LR), bwd)]
        + [_const_spec(c.shape) for c in consts],
        out_specs=[pl.BlockSpec((1, tg, GLA_V), fwd), pl.BlockSpec((1, tg, GLA_V), bwd)],
        out_shape=[jax.ShapeDtypeStruct((b, l, GLA_V), F32)] * 2,
        scratch_shapes=[pltpu.VMEM((2, GLA_V, GLA_QK), F32),
                        pltpu.VMEM((2, n_chunks, GLA_V, GLA_QK), F32),
                        pltpu.VMEM((2, tg, GLA_CHUNK * GLA_QK), BF16)],
        compiler_params=_cparams(("parallel", "arbitrary")),
        name="gla_scan",
    )(qkv, qkv, lr, lr, *consts)


CONV_HALO = SUBLANES
SSD_FRONT_AHEAD = 1
SSD_TILE_CHUNKS = 8


def _softplus(x):
    return jnp.maximum(x, 0.0) + jnp.log1p(jnp.exp(-jnp.abs(x)))


def _ssd_kernel(xc_ref, xp_ref, xn_ref, dt_ref, dtt_ref, cw_ref, cb_ref, bias_r_ref, bias_c_ref,
                alog_r_ref, alog_c_ref, dskip_ref, tri_ref, shift_ref, rep_ref,
                y_ref, xe_ref, st_ref, upd_ref, dec_ref, stb_ref, cgb_ref, e64_ref, *, n_tiles, n_chunks):
    d = pl.program_id(1)
    t = pl.program_id(2)
    tt = t + d * (n_tiles - 1 - 2 * t)
    cs = SSD_CHUNK
    ts = n_chunks * cs
    fwd = d == 0
    hpg = SSD_HEADS // SSD_GROUPS
    gw = hpg * SSD_HEADDIM
    pad = SSD_CONV // 2

    @pl.when(t == 0)
    def _():
        st_ref[...] = jnp.zeros_like(st_ref)

    xe_ref[0:CONV_HALO, :] = jnp.where(tt == 0, 0.0, xp_ref[0])
    xe_ref[CONV_HALO:CONV_HALO + ts, :] = xc_ref[0]
    xe_ref[CONV_HALO + ts:, :] = jnp.where(tt == n_tiles - 1, 0.0, xn_ref[0])

    tri_d = tri_ref[d]
    tri_o = tri_ref[1 - d]
    mask = tri_d.astype(F32) > 0.5
    rep64 = rep_ref[0]
    neg_a_r = -jnp.exp(alog_r_ref[...])
    neg_a_c = -jnp.exp(alog_c_ref[...])
    lane_head = lax.broadcasted_iota(jnp.int32, (cs, gw), 1) // SSD_HEADDIM
    skip_gain = jnp.where(fwd, 1.0, 0.0) * dskip_ref[...]

    def front(c):
        rows = slice(c * cs, (c + 1) * cs)
        xe_b = xe_ref[c * cs:(c + 1) * cs + 2 * CONV_HALO, :].astype(BF16)
        acc = cb_ref[...] + cw_ref[pad:pad + 1, :] * xe_ref[CONV_HALO + c * cs:CONV_HALO + (c + 1) * cs, :]
        for i, kk in enumerate(k for k in range(SSD_CONV) if k != pad):
            acc = acc + cw_ref[kk:kk + 1, :] * _dot(shift_ref[i], xe_b)
        act = acc * jax.nn.sigmoid(acc)
        xs = act[:, :SSD_INNER]
        bm = act[:, SSD_INNER:SSD_INNER + SSD_GROUPS * SSD_STATE]
        cmb = act[:, SSD_INNER + SSD_GROUPS * SSD_STATE:].astype(BF16)

        dt_c = _softplus(dt_ref[0, rows, :] + bias_r_ref[...])
        la_c = dt_c * neg_a_r
        dt_r = _softplus(dtt_ref[0, :, rows] + bias_c_ref[...])
        la_r = dt_r * neg_a_c
        lc_hi, lc_lo = _split_hi_lo(la_c)
        cum_c = _dot(tri_d, lc_hi) + _dot(tri_d, lc_lo)
        lr_hi, lr_lo = _split_hi_lo(la_r)
        cum_r = _dot(lr_hi, tri_o) + _dot(lr_lo, tri_o)
        dt_hi, dt_lo = _split_hi_lo(dt_c)
        dt64 = _dot(dt_hi, rep64) + _dot(dt_lo, rep64)
        cu_hi, cu_lo = _split_hi_lo(cum_c)
        cum64 = _dot(cu_hi, rep64) + _dot(cu_lo, rep64)
        return xs, bm, cmb, cum_r, dt64, cum64

    def back(c, vals):
        xs, bm, cmb, cum_r, dt64, cum64 = vals
        rows = slice(c * cs, (c + 1) * cs)
        tot64 = jnp.where(fwd, cum64[cs - 1:cs, :], cum64[0:1, :])
        xc = xs * dt64
        xcb = xc.astype(BF16)
        xdec = (xc * jnp.exp(tot64 - cum64)).astype(BF16)
        e64_ref[rows, :] = jnp.exp(cum64)
        dec_ref[c] = jnp.exp(tot64)
        cgb_ref[rows, :] = cmb
        ys = []
        for g in range(SSD_GROUPS):
            gl = slice(g * gw, (g + 1) * gw)
            bg = bm[:, g * SSD_STATE:(g + 1) * SSD_STATE]
            cb = _dot_nt(cmb[:, g * SSD_STATE:(g + 1) * SSD_STATE], bg.astype(BF16))
            y_g = jnp.zeros((cs, gw), F32)
            cum_g = cum64[:, gl]
            cum_swapped = pltpu.roll(cum_g, SSD_HEADDIM, axis=1)
            for hg in range(hpg):
                h = g * hpg + hg
                cr = jnp.where(fwd, cum_r[h:h + 1, :], cum_r[SSD_HEADS + h:SSD_HEADS + h + 1, :])
                cum_h = jnp.where(lane_head == hg, cum_g, cum_swapped)
                seg = jnp.exp(jnp.where(mask, cum_h - cr, NEG_BIG))
                yd = _dot((cb * seg).astype(BF16), xcb[:, gl])
                y_g = y_g + jnp.where(lane_head == hg, yd, 0.0)
            ys.append(y_g)
            upd_ref[c, g] = _dot(bg.T.astype(BF16), xdec[:, gl])
        y_ref[0, 0, rows, :] = jnp.concatenate(ys, axis=1) + skip_gain * xs

    ahead = min(SSD_FRONT_AHEAD, n_chunks)
    fronts = [front(c) for c in range(ahead)]
    for c in range(n_chunks):
        if c + ahead < n_chunks:
            fronts.append(front(c + ahead))
        back(c, fronts[c])
        fronts[c] = None

    def body(ci, st):
        c = jnp.where(fwd, ci, n_chunks - 1 - ci)
        dec = dec_ref[c]
        new = []
        for g in range(SSD_GROUPS):
            stb_ref[c, g] = st[g].astype(BF16)
            new.append(st[g] * dec[:, g * gw:(g + 1) * gw] + upd_ref[c, g])
        return tuple(new)

    st = lax.fori_loop(0, n_chunks, body, tuple(st_ref[g] for g in range(SSD_GROUPS)), unroll=True)
    for g in range(SSD_GROUPS):
        st_ref[g] = st[g]

    for c in range(n_chunks):
        rows = slice(c * cs, (c + 1) * cs)
        y_off = jnp.concatenate(
            [_dot(cgb_ref[rows, g * SSD_STATE:(g + 1) * SSD_STATE], stb_ref[c, g]) for g in range(SSD_GROUPS)], axis=1)
        y_ref[0, 0, rows, :] = y_ref[0, 0, rows, :] + e64_ref[rows, :] * y_off


def _ssd_scan(xbc, dt, dtt, cw, cb, bias_r, bias_c, alog_r, alog_c, dskip, tri, shift, rep, n_chunks):
    b, l, _ = xbc.shape
    ts = n_chunks * SSD_CHUNK
    nt = l // ts
    hb = ts // CONV_HALO
    n_hb = l // CONV_HALO
    gw = SSD_INNER // SSD_GROUPS
    tile = lambda d, t: t + d * (nt - 1 - 2 * t)
    consts = (cw, cb, bias_r, bias_c, alog_r, alog_c, dskip, tri, shift)
    return pl.pallas_call(
        functools.partial(_ssd_kernel, n_tiles=nt, n_chunks=n_chunks),
        grid=(b, 2, nt),
        in_specs=[pl.BlockSpec((1, ts, P_XBC), lambda i, d, t: (i, tile(d, t), 0)),
                  pl.BlockSpec((1, CONV_HALO, P_XBC), lambda i, d, t: (i, jnp.maximum(tile(d, t) * hb - 1, 0), 0)),
                  pl.BlockSpec((1, CONV_HALO, P_XBC),
                               lambda i, d, t: (i, jnp.minimum((tile(d, t) + 1) * hb, n_hb - 1), 0)),
                  pl.BlockSpec((1, ts, P_DT), lambda i, d, t: (i, tile(d, t), 0)),
                  pl.BlockSpec((1, 2 * SSD_HEADS, ts), lambda i, d, t: (i, 0, tile(d, t)))]
        + [_const_spec(c.shape) for c in consts]
        + [pl.BlockSpec((1,) + rep.shape[1:], lambda i, d, t: (d, 0, 0))],
        out_specs=pl.BlockSpec((1, 1, ts, SSD_INNER), lambda i, d, t: (d, i, tile(d, t), 0)),
        out_shape=jax.ShapeDtypeStruct((2, b, l, SSD_INNER), F32),
        scratch_shapes=[pltpu.VMEM((ts + 2 * CONV_HALO, P_XBC), F32),
                        pltpu.VMEM((SSD_GROUPS, SSD_STATE, gw), F32),
                        pltpu.VMEM((n_chunks, SSD_GROUPS, SSD_STATE, gw), F32),
                        pltpu.VMEM((n_chunks, 1, SSD_INNER), F32),
                        pltpu.VMEM((n_chunks, SSD_GROUPS, SSD_STATE, gw), BF16),
                        pltpu.VMEM((ts, SSD_GROUPS * SSD_STATE), BF16),
                        pltpu.VMEM((ts, SSD_INNER), F32)],
        compiler_params=_cparams(("parallel", "parallel", "arbitrary")),
        name="ssd_scan",
    )(xbc, xbc, xbc, dt, dtt, *consts, rep)


def _pad_cols(w, width, at=0):
    out = jnp.zeros(w.shape[:-1] + (width,), w.dtype)
    return out.at[..., at:at + w.shape[-1]].set(w)


def _prep_layer(i, ffn1_norm, ffn1_w_gu, ffn1_w_down, mix_norm, w_in, mla_q_norm, mla_w_uq, mla_kv_norm,
                mla_w_ukv, mla_q_gain, mla_k_gain, gla_w_gate, gla_b_gate, gla_o_norm, ssd_conv_w, ssd_conv_b,
                ssd_a_log, ssd_dt_bias, ssd_d, ssd_norm, w_out, ffn2_norm, ffn2_w_gu, ffn2_w_down, final_norm):
    def ffn_w(w_gu, w_down):
        wg = w_gu[:, :D_FF].reshape(D_MODEL, N_FF_CHUNKS, FF_CHUNK).transpose(1, 0, 2).astype(BF16)
        wu = w_gu[:, D_FF:].reshape(D_MODEL, N_FF_CHUNKS, FF_CHUNK).transpose(1, 0, 2).astype(BF16)
        wd = w_down.astype(BF16)
        return wg, wu, wd

    w = {}
    w["n1"] = ffn1_norm[i][None, :]
    w["wg1"], w["wu1"], w["wd1"] = ffn_w(ffn1_w_gu[i], ffn1_w_down[i])
    w["nmix"] = mix_norm[i][None, :]

    wi = w_in[i]
    o_gla = MLA_IN
    o_ssd = MLA_IN + GLA_IN
    o_mla_kr = MLA_Q_LORA + MLA_KV_LORA
    gla_qkv_w = 2 * GLA_QK + GLA_V
    cols = [
        wi[:, :o_mla_kr],
        _pad_cols(wi[:, o_mla_kr:MLA_IN], HEAD_PAD, at=MLA_NOPE),
        wi[:, o_gla:o_gla + gla_qkv_w],
        _pad_cols(wi[:, o_gla + gla_qkv_w + GLA_V:o_ssd], P_LR),
        wi[:, o_gla + gla_qkv_w:o_gla + gla_qkv_w + GLA_V],
        wi[:, o_ssd:o_ssd + SSD_INNER],
        wi[:, o_ssd + SSD_INNER:o_ssd + SSD_INNER + SSD_CONV_DIM],
        _pad_cols(wi[:, o_ssd + SSD_INNER + SSD_CONV_DIM:], P_DT),
    ]
    w["win"] = jnp.concatenate(cols, axis=1).astype(BF16)

    w["qn"] = mla_q_norm[i][None, :]
    w["kvn"] = mla_kv_norm[i][None, :]
    wuq = _pad_cols(mla_w_uq[i].reshape(MLA_Q_LORA, MLA_HEADS, MLA_QK), HEAD_PAD)
    w["wuqt"] = wuq.reshape(MLA_Q_LORA, MLA_HEADS * HEAD_PAD).T.astype(BF16)
    wukv = mla_w_ukv[i].reshape(MLA_KV_LORA, MLA_HEADS, MLA_NOPE + MLA_V)
    w["wk"] = _pad_cols(wukv[..., :MLA_NOPE], HEAD_PAD).reshape(MLA_KV_LORA, MLA_HEADS * HEAD_PAD).astype(BF16)
    w["wvt"] = wukv[..., MLA_NOPE:].reshape(MLA_KV_LORA, MLA_HEADS * MLA_V).T.astype(BF16)
    w["gq"] = (_pad_cols(mla_q_gain[i], HEAD_PAD) * (MLA_QK ** -0.5 * LOG2E))[:, None]
    w["gk"] = _pad_cols(mla_k_gain[i], HEAD_PAD)[None, :]

    wgate = jnp.zeros((2, LANES, GLA_QK), F32)
    for zdir in range(2):
        wgate = wgate.at[zdir, zdir * GLA_GATE_RANK:(zdir + 1) * GLA_GATE_RANK, :].set(gla_w_gate[i, zdir])
    w["wgh"] = wgate.astype(BF16)
    w["wgl"] = (wgate - w["wgh"].astype(F32)).astype(BF16)
    w["bg"] = gla_b_gate[i][:, None, :]
    w["onorm"] = jnp.tile(gla_o_norm[i], GLA_HEADS)[None, :]

    w["cw"] = jnp.zeros((SUBLANES, SSD_CONV_DIM), F32).at[:SSD_CONV].set(ssd_conv_w[i])
    w["cb"] = ssd_conv_b[i][None, :]
    flat_bias = ssd_dt_bias[i].reshape(2 * SSD_HEADS)
    flat_alog = ssd_a_log[i].reshape(2 * SSD_HEADS)
    w["bias_r"] = _pad_cols(flat_bias, P_DT)[None, :]
    w["bias_c"] = flat_bias[:, None]
    w["alog_r"] = _pad_cols(flat_alog, P_DT)[None, :]
    w["alog_c"] = flat_alog[:, None]
    w["dskip"] = jnp.repeat(ssd_d[i], SSD_HEADDIM)[None, :]
    w["snorm"] = ssd_norm[i][None, :]

    w["wout"] = w_out[i].astype(BF16)
    w["n2"] = ffn2_norm[i][None, :]
    w["wg2"], w["wu2"], w["wd2"] = ffn_w(ffn2_w_gu[i], ffn2_w_down[i])
    w["fn"] = final_norm[i][None, :]
    return w


def _block_ones(n, blk):
    idx = np.arange(n) // blk
    return jnp.asarray(idx[:, None] == idx[None, :], BF16)


def _seq_consts(l, tg):
    half = MLA_ROPE // 2
    pos = jnp.arange(l, dtype=F32)
    inv_freq = 1.0 / (ROPE_BASE ** (jnp.arange(0, MLA_ROPE, 2, dtype=F32) / MLA_ROPE))
    ang = pos[:, None] * inv_freq[None, :]
    cos, sin = jnp.cos(ang), jnp.sin(ang)
    c = {}
    c["cos_t"], c["sin_t"] = cos.T, sin.T
    ones = jnp.ones((l, HEAD_PAD), F32)
    c["c_n"] = ones.at[:, MLA_NOPE:MLA_NOPE + half].set(cos).at[:, MLA_NOPE + half:MLA_QK].set(cos)
    zeros = jnp.zeros((l, HEAD_PAD), F32)
    c["s1_n"] = zeros.at[:, MLA_NOPE:MLA_NOPE + half].set(-sin)
    c["s2_n"] = zeros.at[:, MLA_NOPE + half:MLA_QK].set(sin)

    r = np.arange(tg)
    same = (r[:, None] // GLA_CHUNK) == (r[None, :] // GLA_CHUNK)
    c["gla_cum"] = jnp.asarray(np.stack([same & (r[None, :] <= r[:, None]), same & (r[None, :] >= r[:, None])]), BF16)
    c["gla_tot"] = jnp.asarray(same, BF16)
    qk_head = np.arange(GLA_QK) // GLA_DK
    v_head = np.arange(GLA_V) // GLA_DV
    p_col = np.arange(GLA_CHUNK * GLA_QK)
    a_col = np.arange(GLA_HEADS * GLA_CHUNK)
    c["gla_ind"] = jnp.asarray(((p_col % GLA_QK) // GLA_DK * GLA_CHUNK + p_col // GLA_QK)[:, None] == a_col[None, :], BF16)
    c["gla_vmask"] = jnp.asarray((a_col // GLA_CHUNK)[:, None] == v_head[None, :], BF16)
    c["gla_hmask"] = jnp.asarray(v_head[:, None] == qk_head[None, :], F32)

    q = np.arange(SSD_CHUNK)
    c["ssd_tri"] = jnp.asarray(np.stack([q[None, :] <= q[:, None], q[None, :] >= q[:, None]]), BF16)
    xe_row = np.arange(SSD_CHUNK + 2 * CONV_HALO)
    offs = [k - SSD_CONV // 2 for k in range(SSD_CONV) if k != SSD_CONV // 2]
    c["ssd_shift"] = jnp.asarray(np.stack([xe_row[None, :] == (q[:, None] + CONV_HALO + o) for o in offs]), BF16)
    src_lane = np.arange(P_DT)
    c["ssd_rep"] = jnp.asarray(np.stack(
        [src_lane[:, None] == (zdir * SSD_HEADS + np.arange(SSD_INNER) // SSD_HEADDIM)[None, :] for zdir in range(2)]), BF16)
    return c


def _pick_tile(n, pref):
    t = min(n, pref)
    while n % t:
        t //= 2
    return t


def _layer(x, w, c, tg):
    b, l, _ = x.shape
    t = b * l
    tm = _pick_tile(t, 512)
    x1, p_mla, p_qkv, p_lr, p_g, p_z, p_xbc, p_dt = _ffn_in(
        x.reshape(t, D_MODEL), w["n1"], w["wg1"], w["wu1"], w["wd1"], w["nmix"], w["win"], tm)

    qt, k, vt = _mla_prep(p_mla.reshape(b, l, P_MLA), w["qn"], w["wuqt"], w["kvn"], w["wk"], w["wvt"],
                          w["gq"], w["gk"], c["cos_t"], c["sin_t"], c["c_n"], c["s1_n"], c["s2_n"],
                          _pick_tile(l, 512))
    o_mla = _mla_attn(qt, k, vt, _pick_tile(l, 512), _pick_tile(l, 256))

    og_f, og_b = _gla_scan(p_qkv.reshape(b, l, P_QKV), p_lr.reshape(b, l, P_LR), w["wgh"], w["wgl"], w["bg"],
                           c["gla_cum"], c["gla_tot"], c["gla_ind"], c["gla_vmask"], c["gla_hmask"], tg)

    dt3 = p_dt.reshape(b, l, P_DT)
    dtt = jnp.swapaxes(dt3[:, :, :2 * SSD_HEADS], 1, 2)
    y_ssd = _ssd_scan(p_xbc.reshape(b, l, P_XBC), dt3, dtt, w["cw"], w["cb"], w["bias_r"], w["bias_c"],
                      w["alog_r"], w["alog_c"], w["dskip"], c["ssd_tri"], c["ssd_shift"], c["ssd_rep"],
                      _pick_tile(l // SSD_CHUNK, SSD_TILE_CHUNKS))

    y = _ffn_out(x1, o_mla.reshape(t, MLA_HEADS * MLA_V), og_f.reshape(t, GLA_V), og_b.reshape(t, GLA_V), p_g,
                 y_ssd.reshape(2, t, SSD_INNER), p_z, w["onorm"], w["snorm"],
                 _block_ones(GLA_V, GLA_DV), _block_ones(SSD_INNER, SSD_INNER // SSD_GROUPS),
                 w["wout"], w["n2"], w["wg2"], w["wu2"], w["wd2"], w["fn"], tm)
    return y.reshape(b, l, D_MODEL)


def kernel(x_prompt, x_sample, ffn1_norm, ffn1_w_gu, ffn1_w_down, mix_norm, w_in, mla_q_norm, mla_w_uq,
           mla_kv_norm, mla_w_ukv, mla_q_gain, mla_k_gain, gla_w_gate, gla_b_gate, gla_o_norm, ssd_conv_w,
           ssd_conv_b, ssd_a_log, ssd_dt_bias, ssd_d, ssd_norm, w_out, ffn2_norm, ffn2_w_gu, ffn2_w_down,
           final_norm):
    params = (ffn1_norm, ffn1_w_gu, ffn1_w_down, mix_norm, w_in, mla_q_norm, mla_w_uq, mla_kv_norm,
              mla_w_ukv, mla_q_gain, mla_k_gain, gla_w_gate, gla_b_gate, gla_o_norm, ssd_conv_w, ssd_conv_b,
              ssd_a_log, ssd_dt_bias, ssd_d, ssd_norm, w_out, ffn2_norm, ffn2_w_gu, ffn2_w_down, final_norm)
    depth = ffn1_norm.shape[0]
    streams = [x_prompt, x_sample]
    tgs = [_pick_tile(s.shape[1], 512) for s in streams]
    consts = [_seq_consts(s.shape[1], tg) for s, tg in zip(streams, tgs)]
    for i in range(depth):
        w = _prep_layer(i, *params)
        streams = [_layer(s, w, c, tg) for s, c, tg in zip(streams, consts, tgs)]
    return tuple(streams)
```

```python
import functools
import itertools

import jax
import jax.numpy as jnp
import numpy as np
from jax import lax
from jax.experimental import pallas as pl
from jax.experimental.pallas import tpu as pltpu

F32 = jnp.float32
BF16 = jnp.bfloat16

D_MODEL = 1024
D_FF = 2816
EPS = 1e-6
MLA_HEADS = 8
MLA_Q_LORA = 384
MLA_KV_LORA = 256
MLA_NOPE = 64
MLA_ROPE = 32
MLA_QK = MLA_NOPE + MLA_ROPE
MLA_V = 64
ROPE_BASE = 10000.0
GLA_HEADS = 4
GLA_DK = 32
GLA_DV = 64
GLA_GATE_RANK = 16
GLA_GATE_NORM = 16.0
GLA_CHUNK = 16
SSD_HEADS = 4
SSD_HEADDIM = 64
SSD_INNER = SSD_HEADS * SSD_HEADDIM
SSD_GROUPS = 2
SSD_STATE = 128
SSD_CONV = 5
SSD_CHUNK = 128
SSD_CONV_DIM = SSD_INNER + 2 * SSD_GROUPS * SSD_STATE
MLA_IN = MLA_Q_LORA + MLA_KV_LORA + MLA_ROPE
GLA_IN = 2 * GLA_HEADS * GLA_DK + 2 * GLA_HEADS * GLA_DV + 2 * GLA_GATE_RANK
SSD_IN = SSD_INNER + SSD_CONV_DIM + 2 * SSD_HEADS

LANES = 128
SUBLANES = 8
MXU_DIM = 256
VMEM_LIMIT = 56 * 1024 * 1024

HEAD_PAD = LANES
V_ROWS = MLA_V + 16
GLA_QK = GLA_HEADS * GLA_DK
GLA_V = GLA_HEADS * GLA_DV
FF_CHUNK = MXU_DIM
N_FF_CHUNKS = D_FF // FF_CHUNK

P_MLA = MLA_Q_LORA + MLA_KV_LORA + HEAD_PAD
P_QKV = 2 * GLA_QK + GLA_V
P_LR = LANES
P_G = GLA_V
P_Z = SSD_INNER
P_XBC = SSD_CONV_DIM
P_DT = LANES
P_TOTAL = P_MLA + P_QKV + P_LR + P_G + P_Z + P_XBC + P_DT

NEG_BIG = -1e30
LOG2E = 1.4426950408889634


def _rms(x, gain):
    return x * lax.rsqrt(jnp.mean(x * x, axis=-1, keepdims=True) + EPS) * gain


def _split_hi_lo(x):
    hi = x.astype(BF16)
    lo = (x - hi.astype(F32)).astype(BF16)
    return hi, lo


def _dot(a, b):
    return jnp.dot(a, b, preferred_element_type=F32)


def _dot_nt(a, b):
    return lax.dot_general(a, b, (((1,), (1,)), ((), ())), preferred_element_type=F32)


def _dot_tn(a, b):
    return lax.dot_general(a, b, (((0,), (0,)), ((), ())), preferred_element_type=F32)


def _cparams(sem):
    return pltpu.CompilerParams(dimension_semantics=sem, vmem_limit_bytes=VMEM_LIMIT)


def _const_spec(shape):
    nd = len(shape)
    return pl.BlockSpec(shape, lambda *_: (0,) * nd, pipeline_mode=pl.Buffered(1))


def _swiglu(h_ref, wg_ref, wu_ref, wd_ref, a_ref):
    for c in range(N_FF_CHUNKS):
        h = h_ref[...]
        g = _dot(h, wg_ref[c])
        u = _dot(h, wu_ref[c])
        a_ref[:, c * FF_CHUNK:(c + 1) * FF_CHUNK] = (g * jax.nn.sigmoid(g) * u).astype(BF16)
    return _dot(a_ref[...], wd_ref[...])


def _ffn_in_kernel(x_ref, n1_ref, wg_ref, wu_ref, wd_ref, n2_ref, win_ref,
                   x1_ref, mla_ref, qkv_ref, lr_ref, g_ref, z_ref, xbc_ref, dt_ref,
                   h_ref, a_ref):
    x = x_ref[...]
    h_ref[...] = _rms(x, n1_ref[...]).astype(BF16)
    x1 = x + 0.5 * _swiglu(h_ref, wg_ref, wu_ref, wd_ref, a_ref)
    x1_ref[...] = x1
    h2 = _rms(x1, n2_ref[...]).astype(BF16)
    off = 0
    for ref, width in ((mla_ref, P_MLA), (qkv_ref, P_QKV), (lr_ref, P_LR), (g_ref, P_G),
                       (z_ref, P_Z), (xbc_ref, P_XBC), (dt_ref, P_DT)):
        ref[...] = _dot(h2, win_ref[:, off:off + width])
        off += width


def _ffn_in(x, n1, wg, wu, wd, n2, win, tm):
    t = x.shape[0]
    widths = (D_MODEL, P_MLA, P_QKV, P_LR, P_G, P_Z, P_XBC, P_DT)
    tok = lambda w: pl.BlockSpec((tm, w), lambda i: (i, 0))
    return pl.pallas_call(
        _ffn_in_kernel,
        grid=(t // tm,),
        in_specs=[tok(D_MODEL), _const_spec(n1.shape), _const_spec(wg.shape), _const_spec(wu.shape),
                  _const_spec(wd.shape), _const_spec(n2.shape), _const_spec(win.shape)],
        out_specs=[tok(w) for w in widths],
        out_shape=[jax.ShapeDtypeStruct((t, w), F32) for w in widths],
        scratch_shapes=[pltpu.VMEM((tm, D_MODEL), BF16), pltpu.VMEM((tm, D_FF), BF16)],
        compiler_params=_cparams(("parallel",)),
        name="ffn_in",
    )(x, n1, wg, wu, wd, n2, win)


def _ffn_out_kernel(x1_ref, omla_ref, ogf_ref, ogb_ref, g_ref, ysf_ref, ysb_ref, z_ref,
                    onorm_ref, snorm_ref, blk64_ref, blk128_ref, wout_ref,
                    n_ref, wg_ref, wu_ref, wd_ref, fn_ref,
                    y_ref, h_ref, a_ref):
    og = ogf_ref[...] + ogb_ref[...]
    ss = _dot((og * og).astype(BF16), blk64_ref[...])
    g = g_ref[...]
    m_gla = og * lax.rsqrt(ss * (1.0 / GLA_DV) + EPS) * onorm_ref[...] * (g * jax.nn.sigmoid(g))
    z = z_ref[...]
    ys = (ysf_ref[...] + ysb_ref[...]) * (z * jax.nn.sigmoid(z))
    ss2 = _dot((ys * ys).astype(BF16), blk128_ref[...])
    m_ssd = ys * lax.rsqrt(ss2 * (1.0 / (SSD_INNER // SSD_GROUPS)) + EPS) * snorm_ref[...]
    m = jnp.concatenate([omla_ref[...], m_gla.astype(BF16), m_ssd.astype(BF16)], axis=-1)
    x2 = x1_ref[...] + _dot(m, wout_ref[...])
    h_ref[...] = _rms(x2, n_ref[...]).astype(BF16)
    x3 = x2 + 0.5 * _swiglu(h_ref, wg_ref, wu_ref, wd_ref, a_ref)
    y_ref[...] = _rms(x3, fn_ref[...])


def _ffn_out(x1, omla, ogf, ogb, g, ysf, ysb, z, onorm, snorm, blk64, blk128, wout, n, wg, wu, wd, fn, tm):
    t = x1.shape[0]
    tok = lambda w: pl.BlockSpec((tm, w), lambda i: (i, 0))
    consts = (onorm, snorm, blk64, blk128, wout, n, wg, wu, wd, fn)
    return pl.pallas_call(
        _ffn_out_kernel,
        grid=(t // tm,),
        in_specs=[tok(D_MODEL), tok(MLA_HEADS * MLA_V), tok(GLA_V), tok(GLA_V), tok(GLA_V), tok(SSD_INNER),
                  tok(SSD_INNER), tok(SSD_INNER)]
        + [_const_spec(c.shape) for c in consts],
        out_specs=tok(D_MODEL),
        out_shape=jax.ShapeDtypeStruct((t, D_MODEL), F32),
        scratch_shapes=[pltpu.VMEM((tm, D_MODEL), BF16), pltpu.VMEM((tm, D_FF), BF16)],
        compiler_params=_cparams(("parallel",)),
        name="ffn_out",
    )(x1, omla, ogf, ogb, g, ysf, ysb, z, *consts)


def _mla_prep_kernel(p_ref, qn_ref, wuqt_ref, kvn_ref, wk_ref, wvt_ref, gq_ref, gk_ref,
                     cost_ref, sint_ref, cn_ref, s1_ref, s2_ref,
                     qt_ref, k_ref, vt_ref):
    p = p_ref[0]
    cq = p[:, :MLA_Q_LORA]
    ckv = p[:, MLA_Q_LORA:MLA_Q_LORA + MLA_KV_LORA]
    kr = p[:, MLA_Q_LORA + MLA_KV_LORA:]
    hq = _rms(cq, qn_ref[...]).astype(BF16)
    hkv = _rms(ckv, kvn_ref[...]).astype(BF16)

    qt = _dot_nt(wuqt_ref[...], hq)
    cos_t = cost_ref[...]
    sin_t = sint_ref[...]
    gq = gq_ref[...]
    half = MLA_ROPE // 2
    for h in range(MLA_HEADS):
        x = qt[h * HEAD_PAD:(h + 1) * HEAD_PAD]
        ss = jnp.sum(x * x, axis=0, keepdims=True)
        x = x * lax.rsqrt(ss * (1.0 / MLA_QK) + EPS) * gq
        x1 = x[MLA_NOPE:MLA_NOPE + half]
        x2 = x[MLA_NOPE + half:MLA_QK]
        qt_ref[0, h, 0:MLA_NOPE, :] = x[0:MLA_NOPE].astype(BF16)
        qt_ref[0, h, MLA_NOPE:MLA_NOPE + half, :] = (x1 * cos_t - x2 * sin_t).astype(BF16)
        qt_ref[0, h, MLA_NOPE + half:MLA_QK, :] = (x1 * sin_t + x2 * cos_t).astype(BF16)
        qt_ref[0, h, MLA_QK:HEAD_PAD, :] = x[MLA_QK:HEAD_PAD].astype(BF16)

    gk = gk_ref[...]
    krg = kr * gk
    k_rot = (krg * cn_ref[...] + pltpu.roll(krg, HEAD_PAD - half, axis=1) * s1_ref[...]
             + pltpu.roll(krg, half, axis=1) * s2_ref[...])
    ss_rope = jnp.sum(kr * kr, axis=-1, keepdims=True)
    kn = _dot(hkv, wk_ref[...])
    for h in range(MLA_HEADS):
        x = kn[:, h * HEAD_PAD:(h + 1) * HEAD_PAD]
        ss = jnp.sum(x * x, axis=-1, keepdims=True) + ss_rope
        k_ref[0, h] = ((x * gk + k_rot) * lax.rsqrt(ss * (1.0 / MLA_QK) + EPS)).astype(BF16)

    vt = _dot_nt(wvt_ref[...], hkv)
    ones_tile = jnp.where(lax.broadcasted_iota(jnp.int32, (V_ROWS - MLA_V, vt.shape[1]), 0) == 0, 1.0, 0.0).astype(BF16)
    for h in range(MLA_HEADS):
        vt_ref[0, h, 0:MLA_V, :] = vt[h * MLA_V:(h + 1) * MLA_V].astype(BF16)
        vt_ref[0, h, MLA_V:V_ROWS, :] = ones_tile


def _mla_prep(p_mla, qn, wuqt, kvn, wk, wvt, gq, gk, cos_t, sin_t, c_n, s1_n, s2_n, tm):
    b, l, _ = p_mla.shape
    consts = (qn, wuqt, kvn, wk, wvt, gq, gk)
    half = MLA_ROPE // 2
    return pl.pallas_call(
        _mla_prep_kernel,
        grid=(b, l // tm),
        in_specs=[pl.BlockSpec((1, tm, P_MLA), lambda i, j: (i, j, 0))]
        + [_const_spec(c.shape) for c in consts]
        + [pl.BlockSpec((half, tm), lambda i, j: (0, j)), pl.BlockSpec((half, tm), lambda i, j: (0, j)),
           pl.BlockSpec((tm, HEAD_PAD), lambda i, j: (j, 0)), pl.BlockSpec((tm, HEAD_PAD), lambda i, j: (j, 0)),
           pl.BlockSpec((tm, HEAD_PAD), lambda i, j: (j, 0))],
        out_specs=[pl.BlockSpec((1, MLA_HEADS, HEAD_PAD, tm), lambda i, j: (i, 0, 0, j)),
                   pl.BlockSpec((1, MLA_HEADS, tm, HEAD_PAD), lambda i, j: (i, 0, j, 0)),
                   pl.BlockSpec((1, MLA_HEADS, V_ROWS, tm), lambda i, j: (i, 0, 0, j))],
        out_shape=[jax.ShapeDtypeStruct((b, MLA_HEADS, HEAD_PAD, l), BF16),
                   jax.ShapeDtypeStruct((b, MLA_HEADS, l, HEAD_PAD), BF16),
                   jax.ShapeDtypeStruct((b, MLA_HEADS, V_ROWS, l), BF16)],
        compiler_params=_cparams(("parallel", "parallel")),
        name="mla_prep",
    )(p_mla, *consts, cos_t, sin_t, c_n, s1_n, s2_n)


ATTN_HEADS_PER_STEP = 2
ATTN_LOOKAHEAD = 3


def _mla_attn_kernel(qt_ref, k_ref, vt_ref, o_ref, ot_ref, s_ref, *, tk, lookahead):
    n_kb = k_ref.shape[2] // tk
    tq = qt_ref.shape[3]
    n_slots = lookahead + 1
    total = ATTN_HEADS_PER_STEP * n_kb
    qts = [qt_ref[0, hh] for hh in range(ATTN_HEADS_PER_STEP)]

    def scores(g):
        hh, kb = divmod(g, n_kb)
        return _dot(k_ref[0, hh, kb * tk:(kb + 1) * tk, :], qts[hh])

    for g in range(min(lookahead, total)):
        s_ref[g % n_slots] = scores(g)
    m = acc = None
    for g in range(total):
        hh, kb = divmod(g, n_kb)
        if kb == 0:
            m = jnp.full((1, tq), -jnp.inf, F32)
            acc = jnp.zeros((V_ROWS, tq), F32)
        if g + lookahead < total:
            s_ref[(g + lookahead) % n_slots] = scores(g + lookahead)
        s = s_ref[g % n_slots]
        m_new = jnp.maximum(m, jnp.max(s, axis=0, keepdims=True))
        alpha = jnp.exp2(m - m_new)
        p = jnp.exp2(s - m_new)
        acc = alpha * acc + _dot(vt_ref[0, hh, :, kb * tk:(kb + 1) * tk], p.astype(BF16))
        m = m_new
        if kb == n_kb - 1:
            ot_ref[hh * MLA_V:(hh + 1) * MLA_V, :] = acc[0:MLA_V] * (1.0 / acc[MLA_V:MLA_V + 1])
    o_ref[0] = ot_ref[...].T.astype(BF16)


def _mla_attn(qt, k, vt, tq, tk):
    b, nh, _, l = qt.shape
    hp = ATTN_HEADS_PER_STEP
    lookahead = min(ATTN_LOOKAHEAD, hp * (l // tk) - 1)
    return pl.pallas_call(
        functools.partial(_mla_attn_kernel, tk=tk, lookahead=lookahead),
        grid=(b, nh // hp, l // tq),
        in_specs=[pl.BlockSpec((1, hp, HEAD_PAD, tq), lambda i, h, j: (i, h, 0, j)),
                  pl.BlockSpec((1, hp, l, HEAD_PAD), lambda i, h, j: (i, h, 0, 0)),
                  pl.BlockSpec((1, hp, V_ROWS, l), lambda i, h, j: (i, h, 0, 0))],
        out_specs=pl.BlockSpec((1, tq, hp * MLA_V), lambda i, h, j: (i, j, h)),
        out_shape=jax.ShapeDtypeStruct((b, l, nh * MLA_V), BF16),
        scratch_shapes=[pltpu.VMEM((hp * MLA_V, tq), F32),
                        pltpu.VMEM((lookahead + 1, tk, tq), F32)],
        compiler_params=_cparams(("parallel", "parallel", "arbitrary")),
        name="mla_attn",
    )(qt, k, vt)


def _gla_kernel(qkv_f_ref, qkv_b_ref, lr_f_ref, lr_b_ref, wgh_ref, wgl_ref, bg_ref, cum_ref, tot_ref, ind_ref,
                vmask_ref, hmask_ref, o_f_ref, o_b_ref, st_ref, upd_ref, p_ref, *, tg):
    n_chunks = tg // GLA_CHUNK
    chunk_rows = [slice(c * GLA_CHUNK, (c + 1) * GLA_CHUNK) for c in range(n_chunks)]
    hmask = hmask_ref[...]
    vkeep = vmask_ref[...] > 0

    @pl.when(pl.program_id(1) == 0)
    def _():
        st_ref[...] = jnp.zeros_like(st_ref)

    def head(d, qkv_ref, lr_ref):
        qkv = qkv_ref[0]
        q = qkv[:, :GLA_QK] * (GLA_DK ** -0.5)
        k = qkv[:, GLA_QK:2 * GLA_QK]
        v = qkv[:, 2 * GLA_QK:]
        lr_hi, lr_lo = _split_hi_lo(lr_ref[0])
        zg = _dot(lr_hi, wgh_ref[d]) + _dot(lr_lo, wgh_ref[d]) + _dot(lr_hi, wgl_ref[d]) + bg_ref[d]
        log_a = (jnp.minimum(zg, 0.0) - jnp.log1p(jnp.exp(-jnp.abs(zg)))) * (1.0 / GLA_GATE_NORM)
        la_hi, la_lo = _split_hi_lo(log_a)
        bcum = _dot(cum_ref[d], la_hi) + _dot(cum_ref[d], la_lo)
        btot = _dot(tot_ref[...], la_hi) + _dot(tot_ref[...], la_lo)
        return dict(q=q, k=k, bcum=bcum, qd=(q * jnp.exp(bcum)).astype(BF16),
                    kd=(k * jnp.exp(btot - bcum)).astype(BF16), vb=v.astype(BF16), dec=jnp.exp(btot))

    def products(d, h):
        q3 = h["q"].reshape(n_chunks, GLA_CHUNK, GLA_QK)
        k3 = h["k"].reshape(n_chunks, GLA_CHUNK, GLA_QK)
        b3 = h["bcum"].reshape(n_chunks, GLA_CHUNK, GLA_QK)
        tin = lax.broadcasted_iota(jnp.int32, (n_chunks, GLA_CHUNK, GLA_QK), 1)
        pending = list(enumerate(chunk_rows))
        for j in range(GLA_CHUNK):
            for c, rows in pending[j::GLA_CHUNK]:
                upd_ref[d, c] = _dot_tn(h["vb"][rows], h["kd"][rows]) * hmask
            valid = (tin >= j) if d == 0 else (tin <= j)
            e = jnp.exp(jnp.where(valid, b3 - b3[:, j:j + 1, :], NEG_BIG))
            p_ref[d, :, j * GLA_QK:(j + 1) * GLA_QK] = (
                q3 * k3[:, j:j + 1, :] * e).reshape(tg, GLA_QK).astype(BF16)

    def outputs(d, h, o_ref):
        a_intra = _dot(p_ref[d], ind_ref[...]).astype(BF16)
        st = st_ref[d]
        for c in (range(n_chunks) if d == 0 else reversed(range(n_chunks))):
            rows = chunk_rows[c]
            v_tiled = jnp.concatenate([h["vb"][rows]] * GLA_HEADS, axis=0)
            v_rows = jnp.where(vkeep, v_tiled, jnp.zeros_like(v_tiled))
            lhs = jnp.concatenate([h["qd"][rows], a_intra[rows]], axis=1)
            o_ref[0, rows, :] = _dot(lhs, jnp.concatenate([st.T.astype(BF16), v_rows], axis=0))
            st = st * h["dec"][c * GLA_CHUNK:c * GLA_CHUNK + 1, :] + upd_ref[d, c]
            yield
        st_ref[d] = st

    h_f = head(0, qkv_f_ref, lr_f_ref)
    h_b = head(1, qkv_b_ref, lr_b_ref)
    products(0, h_f)
    products(1, h_b)
    for _ in itertools.zip_longest(outputs(0, h_f, o_f_ref), outputs(1, h_b, o_b_ref)):
        pass


def _gla_scan(qkv, lr, wgh, wgl, bg, cum, tot, ind, vmask, hmask, tg):
    b, l, _ = qkv.shape
    nt = l // tg
    n_chunks = tg // GLA_CHUNK
    fwd = lambda i, t: (i, t, 0)
    bwd = lambda i, t: (i, nt - 1 - t, 0)
    consts = (wgh, wgl, bg, cum, tot, ind, vmask, hmask)
    return pl.pallas_call(
        functools.partial(_gla_kernel, tg=tg),
        grid=(b, nt),
        in_specs=[pl.BlockSpec((1, tg, P_QKV), fwd), pl.BlockSpec((1, tg, P_QKV), bwd),
                  pl.BlockSpec((1, tg, P_LR), fwd), pl.BlockSpec((1, tg, P_LR), bwd)]
        + [_const_spec(c.shape) for c in consts],
        out_specs=[pl.BlockSpec((1, tg, GLA_V), fwd), pl.BlockSpec((1, tg, GLA_V), bwd)],
        out_shape=[jax.ShapeDtypeStruct((b, l, GLA_V), F32)] * 2,
        scratch_shapes=[pltpu.VMEM((2, GLA_V, GLA_QK), F32),
                        pltpu.VMEM((2, n_chunks, GLA_V, GLA_QK), F32),
                        pltpu.VMEM((2, tg, GLA_CHUNK * GLA_QK), BF16)],
        compiler_params=_cparams(("parallel", "arbitrary")),
        name="gla_scan",
    )(qkv, qkv, lr, lr, *consts)


CONV_HALO = SUBLANES
SSD_TILE_CHUNKS = 8


def _softplus(x):
    return jnp.maximum(x, 0.0) + jnp.log1p(jnp.exp(-jnp.abs(x)))


def _ssd_kernel(xc_f_ref, xp_f_ref, xn_f_ref, xc_b_ref, xp_b_ref, xn_b_ref, dt_f_ref, dt_b_ref, dtt_f_ref, dtt_b_ref,
                cw_ref, cb_ref, bias_r_ref, bias_c_ref, alog_r_ref, alog_c_ref, dskip_ref, tri_ref, shift_ref, rep_ref,
                y_f_ref, y_b_ref, xe_ref, st_ref, upd_ref, dec_ref, cgb_ref, e64_ref, *, n_tiles, n_chunks):
    t = pl.program_id(1)
    cs = SSD_CHUNK
    ts = n_chunks * cs
    hpg = SSD_HEADS // SSD_GROUPS
    gw = hpg * SSD_HEADDIM
    pad = SSD_CONV // 2
    neg_a_r = -jnp.exp(alog_r_ref[...])
    neg_a_c = -jnp.exp(alog_c_ref[...])
    lane_head = lax.broadcasted_iota(jnp.int32, (cs, gw), 1) // SSD_HEADDIM

    @pl.when(t == 0)
    def _():
        st_ref[...] = jnp.zeros_like(st_ref)

    def direction(d, xc_ref, xp_ref, xn_ref, dt_ref, dtt_ref, y_ref):
        tt = t if d == 0 else n_tiles - 1 - t
        xe_ref[d, 0:CONV_HALO, :] = jnp.where(tt == 0, 0.0, xp_ref[0])
        xe_ref[d, CONV_HALO:CONV_HALO + ts, :] = xc_ref[0]
        xe_ref[d, CONV_HALO + ts:, :] = jnp.where(tt == n_tiles - 1, 0.0, xn_ref[0])
        tri_d = tri_ref[d]
        tri_o = tri_ref[1 - d]
        mask = tri_d.astype(F32) > 0.5
        rep64 = rep_ref[d]

        def front(c):
            rows = slice(c * cs, (c + 1) * cs)
            xe_b = xe_ref[d, c * cs:(c + 1) * cs + 2 * CONV_HALO, :].astype(BF16)
            acc = cb_ref[...] + cw_ref[pad:pad + 1, :] * xe_ref[d, CONV_HALO + c * cs:CONV_HALO + (c + 1) * cs, :]
            for i, kk in enumerate(k for k in range(SSD_CONV) if k != pad):
                acc = acc + cw_ref[kk:kk + 1, :] * _dot(shift_ref[i], xe_b)
            act = acc * jax.nn.sigmoid(acc)
            xs = act[:, :SSD_INNER]
            bm = act[:, SSD_INNER:SSD_INNER + SSD_GROUPS * SSD_STATE]
            cmb = act[:, SSD_INNER + SSD_GROUPS * SSD_STATE:].astype(BF16)
            dt_c = _softplus(dt_ref[0, rows, :] + bias_r_ref[...])
            la_c = dt_c * neg_a_r
            dt_r = _softplus(dtt_ref[0, :, rows] + bias_c_ref[...])
            la_r = dt_r * neg_a_c
            lc_hi, lc_lo = _split_hi_lo(la_c)
            cum_c = _dot(tri_d, lc_hi) + _dot(tri_d, lc_lo)
            lr_hi, lr_lo = _split_hi_lo(la_r)
            cum_r = _dot(lr_hi, tri_o) + _dot(lr_lo, tri_o)
            dt_hi, dt_lo = _split_hi_lo(dt_c)
            dt64 = _dot(dt_hi, rep64) + _dot(dt_lo, rep64)
            cu_hi, cu_lo = _split_hi_lo(cum_c)
            cum64 = _dot(cu_hi, rep64) + _dot(cu_lo, rep64)
            return xs, bm, cmb, cum_r, dt64, cum64

        def back(c, vals):
            xs, bm, cmb, cum_r, dt64, cum64 = vals
            rows = slice(c * cs, (c + 1) * cs)
            tot64 = cum64[cs - 1:cs, :] if d == 0 else cum64[0:1, :]
            xc = xs * dt64
            xcb = xc.astype(BF16)
            xdec = (xc * jnp.exp(tot64 - cum64)).astype(BF16)
            e64_ref[d, rows, :] = jnp.exp(cum64)
            dec_ref[d, c] = jnp.exp(tot64)
            cgb_ref[d, rows, :] = cmb
            ys = []
            for g in range(SSD_GROUPS):
                gl = slice(g * gw, (g + 1) * gw)
                bg = bm[:, g * SSD_STATE:(g + 1) * SSD_STATE]
                cb = _dot_nt(cmb[:, g * SSD_STATE:(g + 1) * SSD_STATE], bg.astype(BF16))
                y_g = jnp.zeros((cs, gw), F32)
                cum_g = cum64[:, gl]
                cum_swapped = pltpu.roll(cum_g, SSD_HEADDIM, axis=1)
                for hg in range(hpg):
                    h = d * SSD_HEADS + g * hpg + hg
                    cum_h = jnp.where(lane_head == hg, cum_g, cum_swapped)
                    seg = jnp.exp(jnp.where(mask, cum_h - cum_r[h:h + 1, :], NEG_BIG))
                    yd = _dot((cb * seg).astype(BF16), xcb[:, gl])
                    y_g = y_g + jnp.where(lane_head == hg, yd, 0.0)
                ys.append(y_g)
                upd_ref[d, c, g] = _dot(bg.T.astype(BF16), xdec[:, gl])
            y = jnp.concatenate(ys, axis=1)
            y_ref[0, rows, :] = y + dskip_ref[...] * xs if d == 0 else y

        vals = front(0)
        for c in range(n_chunks):
            nxt = front(c + 1) if c + 1 < n_chunks else None
            back(c, vals)
            vals = nxt
            yield

        st = [st_ref[d, g] for g in range(SSD_GROUPS)]
        for c in (range(n_chunks) if d == 0 else reversed(range(n_chunks))):
            rows = slice(c * cs, (c + 1) * cs)
            y_off = jnp.concatenate(
                [_dot(cgb_ref[d, rows, g * SSD_STATE:(g + 1) * SSD_STATE], st[g].astype(BF16))
                 for g in range(SSD_GROUPS)], axis=1)
            y_ref[0, rows, :] = y_ref[0, rows, :] + e64_ref[d, rows, :] * y_off
            dec = dec_ref[d, c]
            st = [st[g] * dec[:, g * gw:(g + 1) * gw] + upd_ref[d, c, g] for g in range(SSD_GROUPS)]
            yield
        for g in range(SSD_GROUPS):
            st_ref[d, g] = st[g]

    for _ in itertools.zip_longest(direction(0, xc_f_ref, xp_f_ref, xn_f_ref, dt_f_ref, dtt_f_ref, y_f_ref),
                                   direction(1, xc_b_ref, xp_b_ref, xn_b_ref, dt_b_ref, dtt_b_ref, y_b_ref)):
        pass


def _ssd_scan(xbc, dt, dtt, cw, cb, bias_r, bias_c, alog_r, alog_c, dskip, tri, shift, rep, n_chunks):
    b, l, _ = xbc.shape
    ts = n_chunks * SSD_CHUNK
    nt = l // ts
    hb = ts // CONV_HALO
    n_hb = l // CONV_HALO
    gw = SSD_INNER // SSD_GROUPS
    tiles = (lambda t: t, lambda t: nt - 1 - t)
    cur = [lambda i, t, f=f: (i, f(t), 0) for f in tiles]
    prev = [lambda i, t, f=f: (i, jnp.maximum(f(t) * hb - 1, 0), 0) for f in tiles]
    nxt = [lambda i, t, f=f: (i, jnp.minimum((f(t) + 1) * hb, n_hb - 1), 0) for f in tiles]
    tok_rows = [lambda i, t, f=f: (i, 0, f(t)) for f in tiles]
    consts = (cw, cb, bias_r, bias_c, alog_r, alog_c, dskip, tri, shift, rep)
    halo = lambda m: pl.BlockSpec((1, CONV_HALO, P_XBC), m)
    return pl.pallas_call(
        functools.partial(_ssd_kernel, n_tiles=nt, n_chunks=n_chunks),
        grid=(b, nt),
        in_specs=[pl.BlockSpec((1, ts, P_XBC), cur[0]), halo(prev[0]), halo(nxt[0]),
                  pl.BlockSpec((1, ts, P_XBC), cur[1]), halo(prev[1]), halo(nxt[1]),
                  pl.BlockSpec((1, ts, P_DT), cur[0]), pl.BlockSpec((1, ts, P_DT), cur[1]),
                  pl.BlockSpec((1, 2 * SSD_HEADS, ts), tok_rows[0]), pl.BlockSpec((1, 2 * SSD_HEADS, ts), tok_rows[1])]
        + [_const_spec(c.shape) for c in consts],
        out_specs=[pl.BlockSpec((1, ts, SSD_INNER), cur[0]), pl.BlockSpec((1, ts, SSD_INNER), cur[1])],
        out_shape=[jax.ShapeDtypeStruct((b, l, SSD_INNER), F32)] * 2,
        scratch_shapes=[pltpu.VMEM((2, ts + 2 * CONV_HALO, P_XBC), F32),
                        pltpu.VMEM((2, SSD_GROUPS, SSD_STATE, gw), F32),
                        pltpu.VMEM((2, n_chunks, SSD_GROUPS, SSD_STATE, gw), F32),
                        pltpu.VMEM((2, n_chunks, 1, SSD_INNER), F32),
                        pltpu.VMEM((2, ts, SSD_GROUPS * SSD_STATE), BF16),
                        pltpu.VMEM((2, ts, SSD_INNER), F32)],
        compiler_params=_cparams(("parallel", "arbitrary")),
        name="ssd_scan",
    )(xbc, xbc, xbc, xbc, xbc, xbc, dt, dt, dtt, dtt, *consts)


def _pad_cols(w, width, at=0):
    out = jnp.zeros(w.shape[:-1] + (width,), w.dtype)
    return out.at[..., at:at + w.shape[-1]].set(w)


def _prep_layer(i, ffn1_norm, ffn1_w_gu, ffn1_w_down, mix_norm, w_in, mla_q_norm, mla_w_uq, mla_kv_norm,
                mla_w_ukv, mla_q_gain, mla_k_gain, gla_w_gate, gla_b_gate, gla_o_norm, ssd_conv_w, ssd_conv_b,
                ssd_a_log, ssd_dt_bias, ssd_d, ssd_norm, w_out, ffn2_norm, ffn2_w_gu, ffn2_w_down, final_norm):
    def ffn_w(w_gu, w_down):
        wg = w_gu[:, :D_FF].reshape(D_MODEL, N_FF_CHUNKS, FF_CHUNK).transpose(1, 0, 2).astype(BF16)
        wu = w_gu[:, D_FF:].reshape(D_MODEL, N_FF_CHUNKS, FF_CHUNK).transpose(1, 0, 2).astype(BF16)
        wd = w_down.astype(BF16)
        return wg, wu, wd

    w = {}
    w["n1"] = ffn1_norm[i][None, :]
    w["wg1"], w["wu1"], w["wd1"] = ffn_w(ffn1_w_gu[i], ffn1_w_down[i])
    w["nmix"] = mix_norm[i][None, :]

    wi = w_in[i]
    o_gla = MLA_IN
    o_ssd = MLA_IN + GLA_IN
    o_mla_kr = MLA_Q_LORA + MLA_KV_LORA
    gla_qkv_w = 2 * GLA_QK + GLA_V
    cols = [
        wi[:, :o_mla_kr],
        _pad_cols(wi[:, o_mla_kr:MLA_IN], HEAD_PAD, at=MLA_NOPE),
        wi[:, o_gla:o_gla + gla_qkv_w],
        _pad_cols(wi[:, o_gla + gla_qkv_w + GLA_V:o_ssd], P_LR),
        wi[:, o_gla + gla_qkv_w:o_gla + gla_qkv_w + GLA_V],
        wi[:, o_ssd:o_ssd + SSD_INNER],
        wi[:, o_ssd + SSD_INNER:o_ssd + SSD_INNER + SSD_CONV_DIM],
        _pad_cols(wi[:, o_ssd + SSD_INNER + SSD_CONV_DIM:], P_DT),
    ]
    w["win"] = jnp.concatenate(cols, axis=1).astype(BF16)

    w["qn"] = mla_q_norm[i][None, :]
    w["kvn"] = mla_kv_norm[i][None, :]
    wuq = _pad_cols(mla_w_uq[i].reshape(MLA_Q_LORA, MLA_HEADS, MLA_QK), HEAD_PAD)
    w["wuqt"] = wuq.reshape(MLA_Q_LORA, MLA_HEADS * HEAD_PAD).T.astype(BF16)
    wukv = mla_w_ukv[i].reshape(MLA_KV_LORA, MLA_HEADS, MLA_NOPE + MLA_V)
    w["wk"] = _pad_cols(wukv[..., :MLA_NOPE], HEAD_PAD).reshape(MLA_KV_LORA, MLA_HEADS * HEAD_PAD).astype(BF16)
    w["wvt"] = wukv[..., MLA_NOPE:].reshape(MLA_KV_LORA, MLA_HEADS * MLA_V).T.astype(BF16)
    w["gq"] = (_pad_cols(mla_q_gain[i], HEAD_PAD) * (MLA_QK ** -0.5 * LOG2E))[:, None]
    w["gk"] = _pad_cols(mla_k_gain[i], HEAD_PAD)[None, :]

    wgate = jnp.zeros((2, LANES, GLA_QK), F32)
    for zdir in range(2):
        wgate = wgate.at[zdir, zdir * GLA_GATE_RANK:(zdir + 1) * GLA_GATE_RANK, :].set(gla_w_gate[i, zdir])
    w["wgh"] = wgate.astype(BF16)
    w["wgl"] = (wgate - w["wgh"].astype(F32)).astype(BF16)
    w["bg"] = gla_b_gate[i][:, None, :]
    w["onorm"] = jnp.tile(gla_o_norm[i], GLA_HEADS)[None, :]

    w["cw"] = jnp.zeros((SUBLANES, SSD_CONV_DIM), F32).at[:SSD_CONV].set(ssd_conv_w[i])
    w["cb"] = ssd_conv_b[i][None, :]
    flat_bias = ssd_dt_bias[i].reshape(2 * SSD_HEADS)
    flat_alog = ssd_a_log[i].reshape(2 * SSD_HEADS)
    w["bias_r"] = _pad_cols(flat_bias, P_DT)[None, :]
    w["bias_c"] = flat_bias[:, None]
    w["alog_r"] = _pad_cols(flat_alog, P_DT)[None, :]
    w["alog_c"] = flat_alog[:, None]
    w["dskip"] = jnp.repeat(ssd_d[i], SSD_HEADDIM)[None, :]
    w["snorm"] = ssd_norm[i][None, :]

    w["wout"] = w_out[i].astype(BF16)
    w["n2"] = ffn2_norm[i][None, :]
    w["wg2"], w["wu2"], w["wd2"] = ffn_w(ffn2_w_gu[i], ffn2_w_down[i])
    w["fn"] = final_norm[i][None, :]
    return w


def _block_ones(n, blk):
    idx = np.arange(n) // blk
    return jnp.asarray(idx[:, None] == idx[None, :], BF16)


def _seq_consts(l, tg):
    half = MLA_ROPE // 2
    pos = jnp.arange(l, dtype=F32)
    inv_freq = 1.0 / (ROPE_BASE ** (jnp.arange(0, MLA_ROPE, 2, dtype=F32) / MLA_ROPE))
    ang = pos[:, None] * inv_freq[None, :]
    cos, sin = jnp.cos(ang), jnp.sin(ang)
    c = {}
    c["cos_t"], c["sin_t"] = cos.T, sin.T
    ones = jnp.ones((l, HEAD_PAD), F32)
    c["c_n"] = ones.at[:, MLA_NOPE:MLA_NOPE + half].set(cos).at[:, MLA_NOPE + half:MLA_QK].set(cos)
    zeros = jnp.zeros((l, HEAD_PAD), F32)
    c["s1_n"] = zeros.at[:, MLA_NOPE:MLA_NOPE + half].set(-sin)
    c["s2_n"] = zeros.at[:, MLA_NOPE + half:MLA_QK].set(sin)

    r = np.arange(tg)
    same = (r[:, None] // GLA_CHUNK) == (r[None, :] // GLA_CHUNK)
    c["gla_cum"] = jnp.asarray(np.stack([same & (r[None, :] <= r[:, None]), same & (r[None, :] >= r[:, None])]), BF16)
    c["gla_tot"] = jnp.asarray(same, BF16)
    qk_head = np.arange(GLA_QK) // GLA_DK
    v_head = np.arange(GLA_V) // GLA_DV
    p_col = np.arange(GLA_CHUNK * GLA_QK)
    a_col = np.arange(GLA_HEADS * GLA_CHUNK)
    c["gla_ind"] = jnp.asarray(((p_col % GLA_QK) // GLA_DK * GLA_CHUNK + p_col // GLA_QK)[:, None] == a_col[None, :], BF16)
    c["gla_vmask"] = jnp.asarray((a_col // GLA_CHUNK)[:, None] == v_head[None, :], BF16)
    c["gla_hmask"] = jnp.asarray(v_head[:, None] == qk_head[None, :], F32)

    q = np.arange(SSD_CHUNK)
    c["ssd_tri"] = jnp.asarray(np.stack([q[None, :] <= q[:, None], q[None, :] >= q[:, None]]), BF16)
    xe_row = np.arange(SSD_CHUNK + 2 * CONV_HALO)
    offs = [k - SSD_CONV // 2 for k in range(SSD_CONV) if k != SSD_CONV // 2]
    c["ssd_shift"] = jnp.asarray(np.stack([xe_row[None, :] == (q[:, None] + CONV_HALO + o) for o in offs]), BF16)
    src_lane = np.arange(P_DT)
    c["ssd_rep"] = jnp.asarray(np.stack(
        [src_lane[:, None] == (zdir * SSD_HEADS + np.arange(SSD_INNER) // SSD_HEADDIM)[None, :] for zdir in range(2)]), BF16)
    return c


def _pick_tile(n, pref):
    t = min(n, pref)
    while n % t:
        t //= 2
    return t


def _layer(x, w, c, tg):
    b, l, _ = x.shape
    t = b * l
    tm = _pick_tile(t, 512)
    x1, p_mla, p_qkv, p_lr, p_g, p_z, p_xbc, p_dt = _ffn_in(
        x.reshape(t, D_MODEL), w["n1"], w["wg1"], w["wu1"], w["wd1"], w["nmix"], w["win"], tm)

    qt, k, vt = _mla_prep(p_mla.reshape(b, l, P_MLA), w["qn"], w["wuqt"], w["kvn"], w["wk"], w["wvt"],
                          w["gq"], w["gk"], c["cos_t"], c["sin_t"], c["c_n"], c["s1_n"], c["s2_n"],
                          _pick_tile(l, 512))
    o_mla = _mla_attn(qt, k, vt, _pick_tile(l, 512), _pick_tile(l, 256))

    og_f, og_b = _gla_scan(p_qkv.reshape(b, l, P_QKV), p_lr.reshape(b, l, P_LR), w["wgh"], w["wgl"], w["bg"],
                           c["gla_cum"], c["gla_tot"], c["gla_ind"], c["gla_vmask"], c["gla_hmask"], tg)

    dt3 = p_dt.reshape(b, l, P_DT)
    dtt = jnp.swapaxes(dt3[:, :, :2 * SSD_HEADS], 1, 2)
    ys_f, ys_b = _ssd_scan(p_xbc.reshape(b, l, P_XBC), dt3, dtt, w["cw"], w["cb"], w["bias_r"], w["bias_c"],
                      w["alog_r"], w["alog_c"], w["dskip"], c["ssd_tri"], c["ssd_shift"], c["ssd_rep"],
                      _pick_tile(l // SSD_CHUNK, SSD_TILE_CHUNKS))

    y = _ffn_out(x1, o_mla.reshape(t, MLA_HEADS * MLA_V), og_f.reshape(t, GLA_V), og_b.reshape(t, GLA_V), p_g,
                 ys_f.reshape(t, SSD_INNER), ys_b.reshape(t, SSD_INNER), p_z, w["onorm"], w["snorm"],
                 _block_ones(GLA_V, GLA_DV), _block_ones(SSD_INNER, SSD_INNER // SSD_GROUPS),
                 w["wout"], w["n2"], w["wg2"], w["wu2"], w["wd2"], w["fn"], tm)
    return y.reshape(b, l, D_MODEL)


def kernel(x_prompt, x_sample, ffn1_norm, ffn1_w_gu, ffn1_w_down, mix_norm, w_in, mla_q_norm, mla_w_uq,
           mla_kv_norm, mla_w_ukv, mla_q_gain, mla_k_gain, gla_w_gate, gla_b_gate, gla_o_norm, ssd_conv_w,
           ssd_conv_b, ssd_a_log, ssd_dt_bias, ssd_d, ssd_norm, w_out, ffn2_norm, ffn2_w_gu, ffn2_w_down,
           final_norm):
    params = (ffn1_norm, ffn1_w_gu, ffn1_w_down, mix_norm, w_in, mla_q_norm, mla_w_uq, mla_kv_norm,
              mla_w_ukv, mla_q_gain, mla_k_gain, gla_w_gate, gla_b_gate, gla_o_norm, ssd_conv_w, ssd_conv_b,
              ssd_a_log, ssd_dt_bias, ssd_d, ssd_norm, w_out, ffn2_norm, ffn2_w_gu, ffn2_w_down, final_norm)
    depth = ffn1_norm.shape[0]
    streams = [x_prompt, x_sample]
    tgs = [_pick_tile(s.shape[1], 512) for s in streams]
    consts = [_seq_consts(s.shape[1], tg) for s, tg in zip(streams, tgs)]
    for i in range(depth):
        w = _prep_layer(i, *params)
        streams = [_layer(s, w, c, tg) for s, c, tg in zip(streams, consts, tgs)]
    return tuple(streams)
```

```python
import functools
import itertools

import jax
import jax.numpy as jnp
import numpy as np
from jax import lax
from jax.experimental import pallas as pl
from jax.experimental.pallas import tpu as pltpu

F32 = jnp.float32
BF16 = jnp.bfloat16

D_MODEL = 1024
D_FF = 2816
EPS = 1e-6
MLA_HEADS = 8
MLA_Q_LORA = 384
MLA_KV_LORA = 256
MLA_NOPE = 64
MLA_ROPE = 32
MLA_QK = MLA_NOPE + MLA_ROPE
MLA_V = 64
ROPE_BASE = 10000.0
GLA_HEADS = 4
GLA_DK = 32
GLA_DV = 64
GLA_GATE_RANK = 16
GLA_GATE_NORM = 16.0
GLA_CHUNK = 16
SSD_HEADS = 4
SSD_HEADDIM = 64
SSD_INNER = SSD_HEADS * SSD_HEADDIM
SSD_GROUPS = 2
SSD_STATE = 128
SSD_CONV = 5
SSD_CHUNK = 128
SSD_CONV_DIM = SSD_INNER + 2 * SSD_GROUPS * SSD_STATE
MLA_IN = MLA_Q_LORA + MLA_KV_LORA + MLA_ROPE
GLA_IN = 2 * GLA_HEADS * GLA_DK + 2 * GLA_HEADS * GLA_DV + 2 * GLA_GATE_RANK
SSD_IN = SSD_INNER + SSD_CONV_DIM + 2 * SSD_HEADS

LANES = 128
SUBLANES = 8
MXU_DIM = 256
VMEM_LIMIT = 56 * 1024 * 1024

HEAD_PAD = LANES
V_ROWS = MLA_V + 16
GLA_QK = GLA_HEADS * GLA_DK
GLA_V = GLA_HEADS * GLA_DV
FF_CHUNK = MXU_DIM
N_FF_CHUNKS = D_FF // FF_CHUNK

P_MLA = MLA_Q_LORA + MLA_KV_LORA + HEAD_PAD
P_QKV = 2 * GLA_QK + GLA_V
P_LR = LANES
P_G = GLA_V
P_Z = SSD_INNER
P_XBC = SSD_CONV_DIM
P_DT = LANES
P_TOTAL = P_MLA + P_QKV + P_LR + P_G + P_Z + P_XBC + P_DT

NEG_BIG = -1e30
LOG2E = 1.4426950408889634


def _rms(x, gain):
    return x * lax.rsqrt(jnp.mean(x * x, axis=-1, keepdims=True) + EPS) * gain


def _split_hi_lo(x):
    hi = x.astype(BF16)
    lo = (x - hi.astype(F32)).astype(BF16)
    return hi, lo


def _dot(a, b):
    return jnp.dot(a, b, preferred_element_type=F32)


def _dot_nt(a, b):
    return lax.dot_general(a, b, (((1,), (1,)), ((), ())), preferred_element_type=F32)


def _dot_tn(a, b):
    return lax.dot_general(a, b, (((0,), (0,)), ((), ())), preferred_element_type=F32)


def _cparams(sem):
    return pltpu.CompilerParams(dimension_semantics=sem, vmem_limit_bytes=VMEM_LIMIT)


def _const_spec(shape):
    nd = len(shape)
    return pl.BlockSpec(shape, lambda *_: (0,) * nd, pipeline_mode=pl.Buffered(1))


def _swiglu(h_ref, wg_ref, wu_ref, wd_ref, a_ref):
    for c in range(N_FF_CHUNKS):
        h = h_ref[...]
        g = _dot(h, wg_ref[c])
        u = _dot(h, wu_ref[c])
        a_ref[:, c * FF_CHUNK:(c + 1) * FF_CHUNK] = (g * jax.nn.sigmoid(g) * u).astype(BF16)
    return _dot(a_ref[...], wd_ref[...])


def _ffn_in_kernel(x_ref, n1_ref, wg_ref, wu_ref, wd_ref, n2_ref, win_ref,
                   x1_ref, mla_ref, qkv_ref, lr_ref, g_ref, z_ref, xbc_ref, dt_ref,
                   h_ref, a_ref):
    x = x_ref[...]
    h_ref[...] = _rms(x, n1_ref[...]).astype(BF16)
    x1 = x + 0.5 * _swiglu(h_ref, wg_ref, wu_ref, wd_ref, a_ref)
    x1_ref[...] = x1
    h2 = _rms(x1, n2_ref[...]).astype(BF16)
    off = 0
    for ref, width in ((mla_ref, P_MLA), (qkv_ref, P_QKV), (lr_ref, P_LR), (g_ref, P_G),
                       (z_ref, P_Z), (xbc_ref, P_XBC), (dt_ref, P_DT)):
        ref[...] = _dot(h2, win_ref[:, off:off + width])
        off += width


def _ffn_in(x, n1, wg, wu, wd, n2, win, tm):
    t = x.shape[0]
    widths = (D_MODEL, P_MLA, P_QKV, P_LR, P_G, P_Z, P_XBC, P_DT)
    tok = lambda w: pl.BlockSpec((tm, w), lambda i: (i, 0))
    return pl.pallas_call(
        _ffn_in_kernel,
        grid=(t // tm,),
        in_specs=[tok(D_MODEL), _const_spec(n1.shape), _const_spec(wg.shape), _const_spec(wu.shape),
                  _const_spec(wd.shape), _const_spec(n2.shape), _const_spec(win.shape)],
        out_specs=[tok(w) for w in widths],
        out_shape=[jax.ShapeDtypeStruct((t, w), F32) for w in widths],
        scratch_shapes=[pltpu.VMEM((tm, D_MODEL), BF16), pltpu.VMEM((tm, D_FF), BF16)],
        compiler_params=_cparams(("parallel",)),
        name="ffn_in",
    )(x, n1, wg, wu, wd, n2, win)


def _ffn_out_kernel(x1_ref, omla_ref, ogf_ref, ogb_ref, g_ref, ysf_ref, ysb_ref, z_ref,
                    onorm_ref, snorm_ref, blk64_ref, blk128_ref, wout_ref,
                    n_ref, wg_ref, wu_ref, wd_ref, fn_ref,
                    y_ref, h_ref, a_ref):
    og = ogf_ref[...] + ogb_ref[...]
    ss = _dot((og * og).astype(BF16), blk64_ref[...])
    g = g_ref[...]
    m_gla = og * lax.rsqrt(ss * (1.0 / GLA_DV) + EPS) * onorm_ref[...] * (g * jax.nn.sigmoid(g))
    z = z_ref[...]
    ys = (ysf_ref[...] + ysb_ref[...]) * (z * jax.nn.sigmoid(z))
    ss2 = _dot((ys * ys).astype(BF16), blk128_ref[...])
    m_ssd = ys * lax.rsqrt(ss2 * (1.0 / (SSD_INNER // SSD_GROUPS)) + EPS) * snorm_ref[...]
    m = jnp.concatenate([omla_ref[...], m_gla.astype(BF16), m_ssd.astype(BF16)], axis=-1)
    x2 = x1_ref[...] + _dot(m, wout_ref[...])
    h_ref[...] = _rms(x2, n_ref[...]).astype(BF16)
    x3 = x2 + 0.5 * _swiglu(h_ref, wg_ref, wu_ref, wd_ref, a_ref)
    y_ref[...] = _rms(x3, fn_ref[...])


def _ffn_out(x1, omla, ogf, ogb, g, ysf, ysb, z, onorm, snorm, blk64, blk128, wout, n, wg, wu, wd, fn, tm):
    t = x1.shape[0]
    tok = lambda w: pl.BlockSpec((tm, w), lambda i: (i, 0))
    consts = (onorm, snorm, blk64, blk128, wout, n, wg, wu, wd, fn)
    return pl.pallas_call(
        _ffn_out_kernel,
        grid=(t // tm,),
        in_specs=[tok(D_MODEL), tok(MLA_HEADS * MLA_V), tok(GLA_V), tok(GLA_V), tok(GLA_V), tok(SSD_INNER),
                  tok(SSD_INNER), tok(SSD_INNER)]
        + [_const_spec(c.shape) for c in consts],
        out_specs=tok(D_MODEL),
        out_shape=jax.ShapeDtypeStruct((t, D_MODEL), F32),
        scratch_shapes=[pltpu.VMEM((tm, D_MODEL), BF16), pltpu.VMEM((tm, D_FF), BF16)],
        compiler_params=_cparams(("parallel",)),
        name="ffn_out",
    )(x1, omla, ogf, ogb, g, ysf, ysb, z, *consts)


def _mla_prep_kernel(p_ref, qn_ref, wuqt_ref, kvn_ref, wk_ref, wvt_ref, gq_ref, gk_ref,
                     cost_ref, sint_ref, cn_ref, s1_ref, s2_ref,
                     qt_ref, k_ref, vt_ref):
    p = p_ref[0]
    cq = p[:, :MLA_Q_LORA]
    ckv = p[:, MLA_Q_LORA:MLA_Q_LORA + MLA_KV_LORA]
    kr = p[:, MLA_Q_LORA + MLA_KV_LORA:]
    hq = _rms(cq, qn_ref[...]).astype(BF16)
    hkv = _rms(ckv, kvn_ref[...]).astype(BF16)

    qt = _dot_nt(wuqt_ref[...], hq)
    cos_t = cost_ref[...]
    sin_t = sint_ref[...]
    gq = gq_ref[...]
    half = MLA_ROPE // 2
    for h in range(MLA_HEADS):
        x = qt[h * HEAD_PAD:(h + 1) * HEAD_PAD]
        ss = jnp.sum(x * x, axis=0, keepdims=True)
        x = x * lax.rsqrt(ss * (1.0 / MLA_QK) + EPS) * gq
        x1 = x[MLA_NOPE:MLA_NOPE + half]
        x2 = x[MLA_NOPE + half:MLA_QK]
        qt_ref[0, h, 0:MLA_NOPE, :] = x[0:MLA_NOPE].astype(BF16)
        qt_ref[0, h, MLA_NOPE:MLA_NOPE + half, :] = (x1 * cos_t - x2 * sin_t).astype(BF16)
        qt_ref[0, h, MLA_NOPE + half:MLA_QK, :] = (x1 * sin_t + x2 * cos_t).astype(BF16)
        qt_ref[0, h, MLA_QK:HEAD_PAD, :] = x[MLA_QK:HEAD_PAD].astype(BF16)

    gk = gk_ref[...]
    krg = kr * gk
    k_rot = (krg * cn_ref[...] + pltpu.roll(krg, HEAD_PAD - half, axis=1) * s1_ref[...]
             + pltpu.roll(krg, half, axis=1) * s2_ref[...])
    ss_rope = jnp.sum(kr * kr, axis=-1, keepdims=True)
    kn = _dot(hkv, wk_ref[...])
    for h in range(MLA_HEADS):
        x = kn[:, h * HEAD_PAD:(h + 1) * HEAD_PAD]
        ss = jnp.sum(x * x, axis=-1, keepdims=True) + ss_rope
        k_ref[0, h] = ((x * gk + k_rot) * lax.rsqrt(ss * (1.0 / MLA_QK) + EPS)).astype(BF16)

    vt = _dot_nt(wvt_ref[...], hkv)
    ones_tile = jnp.where(lax.broadcasted_iota(jnp.int32, (V_ROWS - MLA_V, vt.shape[1]), 0) == 0, 1.0, 0.0).astype(BF16)
    for h in range(MLA_HEADS):
        vt_ref[0, h, 0:MLA_V, :] = vt[h * MLA_V:(h + 1) * MLA_V].astype(BF16)
        vt_ref[0, h, MLA_V:V_ROWS, :] = ones_tile


def _mla_prep(p_mla, qn, wuqt, kvn, wk, wvt, gq, gk, cos_t, sin_t, c_n, s1_n, s2_n, tm):
    b, l, _ = p_mla.shape
    consts = (qn, wuqt, kvn, wk, wvt, gq, gk)
    half = MLA_ROPE // 2
    return pl.pallas_call(
        _mla_prep_kernel,
        grid=(b, l // tm),
        in_specs=[pl.BlockSpec((1, tm, P_MLA), lambda i, j: (i, j, 0))]
        + [_const_spec(c.shape) for c in consts]
        + [pl.BlockSpec((half, tm), lambda i, j: (0, j)), pl.BlockSpec((half, tm), lambda i, j: (0, j)),
           pl.BlockSpec((tm, HEAD_PAD), lambda i, j: (j, 0)), pl.BlockSpec((tm, HEAD_PAD), lambda i, j: (j, 0)),
           pl.BlockSpec((tm, HEAD_PAD), lambda i, j: (j, 0))],
        out_specs=[pl.BlockSpec((1, MLA_HEADS, HEAD_PAD, tm), lambda i, j: (i, 0, 0, j)),
                   pl.BlockSpec((1, MLA_HEADS, tm, HEAD_PAD), lambda i, j: (i, 0, j, 0)),
                   pl.BlockSpec((1, MLA_HEADS, V_ROWS, tm), lambda i, j: (i, 0, 0, j))],
        out_shape=[jax.ShapeDtypeStruct((b, MLA_HEADS, HEAD_PAD, l), BF16),
                   jax.ShapeDtypeStruct((b, MLA_HEADS, l, HEAD_PAD), BF16),
                   jax.ShapeDtypeStruct((b, MLA_HEADS, V_ROWS, l), BF16)],
        compiler_params=_cparams(("parallel", "parallel")),
        name="mla_prep",
    )(p_mla, *consts, cos_t, sin_t, c_n, s1_n, s2_n)


ATTN_HEADS_PER_STEP = 2
ATTN_LOOKAHEAD = 2


def _mla_attn_kernel(qt_ref, k_ref, vt_ref, o_ref, ot_ref, s_ref, *, tk, lookahead):
    n_kb = k_ref.shape[2] // tk
    tq = qt_ref.shape[3]
    n_slots = lookahead + 1
    total = ATTN_HEADS_PER_STEP * n_kb
    qts = [qt_ref[0, hh] for hh in range(ATTN_HEADS_PER_STEP)]

    def scores(g):
        hh, kb = divmod(g, n_kb)
        return _dot(k_ref[0, hh, kb * tk:(kb + 1) * tk, :], qts[hh])

    for g in range(min(lookahead, total)):
        s_ref[g % n_slots] = scores(g)
    m = acc = None
    for g in range(total):
        hh, kb = divmod(g, n_kb)
        if kb == 0:
            m = jnp.full((1, tq), -jnp.inf, F32)
            acc = jnp.zeros((V_ROWS, tq), F32)
        if g + lookahead < total:
            s_ref[(g + lookahead) % n_slots] = scores(g + lookahead)
        s = s_ref[g % n_slots]
        m_new = jnp.maximum(m, jnp.max(s, axis=0, keepdims=True))
        alpha = jnp.exp2(m - m_new)
        p = jnp.exp2(s - m_new)
        acc = alpha * acc + _dot(vt_ref[0, hh, :, kb * tk:(kb + 1) * tk], p.astype(BF16))
        m = m_new
        if kb == n_kb - 1:
            ot_ref[hh * MLA_V:(hh + 1) * MLA_V, :] = acc[0:MLA_V] * (1.0 / acc[MLA_V:MLA_V + 1])
    o_ref[0] = ot_ref[...].T.astype(BF16)


def _mla_attn(qt, k, vt, tq, tk):
    b, nh, _, l = qt.shape
    hp = ATTN_HEADS_PER_STEP
    lookahead = min(ATTN_LOOKAHEAD, hp * (l // tk) - 1)
    return pl.pallas_call(
        functools.partial(_mla_attn_kernel, tk=tk, lookahead=lookahead),
        grid=(b, nh // hp, l // tq),
        in_specs=[pl.BlockSpec((1, hp, HEAD_PAD, tq), lambda i, h, j: (i, h, 0, j)),
                  pl.BlockSpec((1, hp, l, HEAD_PAD), lambda i, h, j: (i, h, 0, 0)),
                  pl.BlockSpec((1, hp, V_ROWS, l), lambda i, h, j: (i, h, 0, 0))],
        out_specs=pl.BlockSpec((1, tq, hp * MLA_V), lambda i, h, j: (i, j, h)),
        out_shape=jax.ShapeDtypeStruct((b, l, nh * MLA_V), BF16),
        scratch_shapes=[pltpu.VMEM((hp * MLA_V, tq), F32),
                        pltpu.VMEM((lookahead + 1, tk, tq), F32)],
        compiler_params=_cparams(("parallel", "parallel", "arbitrary")),
        name="mla_attn",
    )(qt, k, vt)


def _gla_kernel(qkv_f_ref, qkv_b_ref, lr_f_ref, lr_b_ref, wgh_ref, wgl_ref, bg_ref, cum_ref, tot_ref, ind_ref,
                vmask_ref, hmask_ref, o_f_ref, o_b_ref, st_ref, upd_ref, p_ref, *, tg):
    n_chunks = tg // GLA_CHUNK
    chunk_rows = [slice(c * GLA_CHUNK, (c + 1) * GLA_CHUNK) for c in range(n_chunks)]
    hmask = hmask_ref[...]
    vkeep = vmask_ref[...] > 0

    @pl.when(pl.program_id(1) == 0)
    def _():
        st_ref[...] = jnp.zeros_like(st_ref)

    def head(d, qkv_ref, lr_ref):
        qkv = qkv_ref[0]
        q = qkv[:, :GLA_QK] * (GLA_DK ** -0.5)
        k = qkv[:, GLA_QK:2 * GLA_QK]
        v = qkv[:, 2 * GLA_QK:]
        lr_hi, lr_lo = _split_hi_lo(lr_ref[0])
        zg = _dot(lr_hi, wgh_ref[d]) + _dot(lr_lo, wgh_ref[d]) + _dot(lr_hi, wgl_ref[d]) + bg_ref[d]
        log_a = (jnp.minimum(zg, 0.0) - jnp.log1p(jnp.exp(-jnp.abs(zg)))) * (1.0 / GLA_GATE_NORM)
        la_hi, la_lo = _split_hi_lo(log_a)
        bcum = _dot(cum_ref[d], la_hi) + _dot(cum_ref[d], la_lo)
        btot = _dot(tot_ref[...], la_hi) + _dot(tot_ref[...], la_lo)
        return dict(q=q, k=k, bcum=bcum, qd=(q * jnp.exp(bcum)).astype(BF16),
                    kd=(k * jnp.exp(btot - bcum)).astype(BF16), vb=v.astype(BF16), dec=jnp.exp(btot))

    def products(d, h):
        q3 = h["q"].reshape(n_chunks, GLA_CHUNK, GLA_QK)
        k3 = h["k"].reshape(n_chunks, GLA_CHUNK, GLA_QK)
        b3 = h["bcum"].reshape(n_chunks, GLA_CHUNK, GLA_QK)
        tin = lax.broadcasted_iota(jnp.int32, (n_chunks, GLA_CHUNK, GLA_QK), 1)
        pending = list(enumerate(chunk_rows))
        for j in range(GLA_CHUNK):
            for c, rows in pending[j::GLA_CHUNK]:
                upd_ref[d, c] = _dot_tn(h["vb"][rows], h["kd"][rows]) * hmask
            valid = (tin >= j) if d == 0 else (tin <= j)
            e = jnp.exp(jnp.where(valid, b3 - b3[:, j:j + 1, :], NEG_BIG))
            p_ref[d, :, j * GLA_QK:(j + 1) * GLA_QK] = (
                q3 * k3[:, j:j + 1, :] * e).reshape(tg, GLA_QK).astype(BF16)

    def outputs(d, h, o_ref):
        a_intra = _dot(p_ref[d], ind_ref[...]).astype(BF16)
        st = st_ref[d]
        for c in (range(n_chunks) if d == 0 else reversed(range(n_chunks))):
            rows = chunk_rows[c]
            v_tiled = jnp.concatenate([h["vb"][rows]] * GLA_HEADS, axis=0)
            v_rows = jnp.where(vkeep, v_tiled, jnp.zeros_like(v_tiled))
            lhs = jnp.concatenate([h["qd"][rows], a_intra[rows]], axis=1)
            o_ref[0, rows, :] = _dot(lhs, jnp.concatenate([st.T.astype(BF16), v_rows], axis=0))
            st = st * h["dec"][c * GLA_CHUNK:c * GLA_CHUNK + 1, :] + upd_ref[d, c]
            yield
        st_ref[d] = st

    h_f = head(0, qkv_f_ref, lr_f_ref)
    h_b = head(1, qkv_b_ref, lr_b_ref)
    products(0, h_f)
    products(1, h_b)
    for _ in itertools.zip_longest(outputs(0, h_f, o_f_ref), outputs(1, h_b, o_b_ref)):
        pass


def _gla_scan(qkv, lr, wgh, wgl, bg, cum, tot, ind, vmask, hmask, tg):
    b, l, _ = qkv.shape
    nt = l // tg
    n_chunks = tg // GLA_CHUNK
    fwd = lambda i, t: (i, t, 0)
    bwd = lambda i, t: (i, nt - 1 - t, 0)
    consts = (wgh, wgl, bg, cum, tot, ind, vmask, hmask)
    return pl.pallas_call(
        functools.partial(_gla_kernel, tg=tg),
        grid=(b, nt),
        in_specs=[pl.BlockSpec((1, tg, P_QKV), fwd), pl.BlockSpec((1, tg, P_QKV), bwd),
                  pl.BlockSpec((1, tg, P_LR), fwd), pl.BlockSpec((1, tg, P_LR), bwd)]
        + [_const_spec(c.shape) for c in consts],
        out_specs=[pl.BlockSpec((1, tg, GLA_V), fwd), pl.BlockSpec((1, tg, GLA_V), bwd)],
        out_shape=[jax.ShapeDtypeStruct((b, l, GLA_V), F32)] * 2,
        scratch_shapes=[pltpu.VMEM((2, GLA_V, GLA_QK), F32),
                        pltpu.VMEM((2, n_chunks, GLA_V, GLA_QK), F32),
                        pltpu.VMEM((2, tg, GLA_CHUNK * GLA_QK), BF16)],
        compiler_params=_cparams(("parallel", "arbitrary")),
        name="gla_scan",
    )(qkv, qkv, lr, lr, *consts)


CONV_HALO = SUBLANES
SSD_TILE_CHUNKS = 8


def _softplus(x):
    return jnp.maximum(x, 0.0) + jnp.log1p(jnp.exp(-jnp.abs(x)))


def _ssd_kernel(xc_f_ref, xp_f_ref, xn_f_ref, xc_b_ref, xp_b_ref, xn_b_ref, dt_f_ref, dt_b_ref, dtt_f_ref, dtt_b_ref,
                cw_ref, cb_ref, bias_r_ref, bias_c_ref, alog_r_ref, alog_c_ref, dskip_ref, tri_ref, shift_ref, rep_ref,
                y_f_ref, y_b_ref, xe_ref, st_ref, upd_ref, dec_ref, cgb_ref, e64_ref, *, n_tiles, n_chunks):
    t = pl.program_id(1)
    cs = SSD_CHUNK
    ts = n_chunks * cs
    hpg = SSD_HEADS // SSD_GROUPS
    gw = hpg * SSD_HEADDIM
    pad = SSD_CONV // 2
    neg_a_r = -jnp.exp(alog_r_ref[...])
    neg_a_c = -jnp.exp(alog_c_ref[...])
    lane_head = lax.broadcasted_iota(jnp.int32, (cs, gw), 1) // SSD_HEADDIM

    @pl.when(t == 0)
    def _():
        st_ref[...] = jnp.zeros_like(st_ref)

    def direction(d, xc_ref, xp_ref, xn_ref, dt_ref, dtt_ref, y_ref):
        tt = t if d == 0 else n_tiles - 1 - t
        xe_ref[d, 0:CONV_HALO, :] = jnp.where(tt == 0, 0.0, xp_ref[0])
        xe_ref[d, CONV_HALO:CONV_HALO + ts, :] = xc_ref[0]
        xe_ref[d, CONV_HALO + ts:, :] = jnp.where(tt == n_tiles - 1, 0.0, xn_ref[0])
        tri_d = tri_ref[d]
        tri_o = tri_ref[1 - d]
        mask = tri_d.astype(F32) > 0.5
        rep64 = rep_ref[d]

        def front(c):
            rows = slice(c * cs, (c + 1) * cs)
            xe_b = xe_ref[d, c * cs:(c + 1) * cs + 2 * CONV_HALO, :].astype(BF16)
            acc = cb_ref[...] + cw_ref[pad:pad + 1, :] * xe_ref[d, CONV_HALO + c * cs:CONV_HALO + (c + 1) * cs, :]
            for i, kk in enumerate(k for k in range(SSD_CONV) if k != pad):
                acc = acc + cw_ref[kk:kk + 1, :] * _dot(shift_ref[i], xe_b)
            act = acc * jax.nn.sigmoid(acc)
            xs = act[:, :SSD_INNER]
            bm = act[:, SSD_INNER:SSD_INNER + SSD_GROUPS * SSD_STATE]
            cmb = act[:, SSD_INNER + SSD_GROUPS * SSD_STATE:].astype(BF16)
            dt_c = _softplus(dt_ref[0, rows, :] + bias_r_ref[...])
            la_c = dt_c * neg_a_r
            dt_r = _softplus(dtt_ref[0, :, rows] + bias_c_ref[...])
            la_r = dt_r * neg_a_c
            lc_hi, lc_lo = _split_hi_lo(la_c)
            cum_c = _dot(tri_d, lc_hi) + _dot(tri_d, lc_lo)
            lr_hi, lr_lo = _split_hi_lo(la_r)
            cum_r = _dot(lr_hi, tri_o) + _dot(lr_lo, tri_o)
            dt_hi, dt_lo = _split_hi_lo(dt_c)
            dt64 = _dot(dt_hi, rep64) + _dot(dt_lo, rep64)
            cu_hi, cu_lo = _split_hi_lo(cum_c)
            cum64 = _dot(cu_hi, rep64) + _dot(cu_lo, rep64)
            return xs, bm, cmb, cum_r, dt64, cum64

        def back(c, vals):
            xs, bm, cmb, cum_r, dt64, cum64 = vals
            rows = slice(c * cs, (c + 1) * cs)
            tot64 = cum64[cs - 1:cs, :] if d == 0 else cum64[0:1, :]
            xc = xs * dt64
            xcb = xc.astype(BF16)
            xdec = (xc * jnp.exp(tot64 - cum64)).astype(BF16)
            e64_ref[d, rows, :] = jnp.exp(cum64)
            dec_ref[d, c] = jnp.exp(tot64)
            cgb_ref[d, rows, :] = cmb
            ys = []
            for g in range(SSD_GROUPS):
                gl = slice(g * gw, (g + 1) * gw)
                bg = bm[:, g * SSD_STATE:(g + 1) * SSD_STATE]
                cb = _dot_nt(cmb[:, g * SSD_STATE:(g + 1) * SSD_STATE], bg.astype(BF16))
                y_g = jnp.zeros((cs, gw), F32)
                cum_g = cum64[:, gl]
                cum_swapped = pltpu.roll(cum_g, SSD_HEADDIM, axis=1)
                for hg in range(hpg):
                    h = d * SSD_HEADS + g * hpg + hg
                    cum_h = jnp.where(lane_head == hg, cum_g, cum_swapped)
                    seg = jnp.exp(jnp.where(mask, cum_h - cum_r[h:h + 1, :], NEG_BIG))
                    yd = _dot((cb * seg).astype(BF16), xcb[:, gl])
                    y_g = y_g + jnp.where(lane_head == hg, yd, 0.0)
                ys.append(y_g)
                upd_ref[d, c, g] = _dot(bg.T.astype(BF16), xdec[:, gl])
            y = jnp.concatenate(ys, axis=1)
            y_ref[0, rows, :] = y + dskip_ref[...] * xs if d == 0 else y

        vals = front(0)
        for c in range(n_chunks):
            nxt = front(c + 1) if c + 1 < n_chunks else None
            back(c, vals)
            vals = nxt
            yield

        st = [st_ref[d, g] for g in range(SSD_GROUPS)]
        for c in (range(n_chunks) if d == 0 else reversed(range(n_chunks))):
            rows = slice(c * cs, (c + 1) * cs)
            y_off = jnp.concatenate(
                [_dot(cgb_ref[d, rows, g * SSD_STATE:(g + 1) * SSD_STATE], st[g].astype(BF16))
                 for g in range(SSD_GROUPS)], axis=1)
            y_ref[0, rows, :] = y_ref[0, rows, :] + e64_ref[d, rows, :] * y_off
            dec = dec_ref[d, c]
            st = [st[g] * dec[:, g * gw:(g + 1) * gw] + upd_ref[d, c, g] for g in range(SSD_GROUPS)]
            yield
        for g in range(SSD_GROUPS):
            st_ref[d, g] = st[g]

    for _ in itertools.zip_longest(direction(0, xc_f_ref, xp_f_ref, xn_f_ref, dt_f_ref, dtt_f_ref, y_f_ref),
                                   direction(1, xc_b_ref, xp_b_ref, xn_b_ref, dt_b_ref, dtt_b_ref, y_b_ref)):
        pass


def _ssd_scan(xbc, dt, dtt, cw, cb, bias_r, bias_c, alog_r, alog_c, dskip, tri, shift, rep, n_chunks):
    b, l, _ = xbc.shape
    ts = n_chunks * SSD_CHUNK
    nt = l // ts
    hb = ts // CONV_HALO
    n_hb = l // CONV_HALO
    gw = SSD_INNER // SSD_GROUPS
    tiles = (lambda t: t, lambda t: nt - 1 - t)
    cur = [lambda i, t, f=f: (i, f(t), 0) for f in tiles]
    prev = [lambda i, t, f=f: (i, jnp.maximum(f(t) * hb - 1, 0), 0) for f in tiles]
    nxt = [lambda i, t, f=f: (i, jnp.minimum((f(t) + 1) * hb, n_hb - 1), 0) for f in tiles]
    tok_rows = [lambda i, t, f=f: (i, 0, f(t)) for f in tiles]
    consts = (cw, cb, bias_r, bias_c, alog_r, alog_c, dskip, tri, shift, rep)
    halo = lambda m: pl.BlockSpec((1, CONV_HALO, P_XBC), m)
    return pl.pallas_call(
        functools.partial(_ssd_kernel, n_tiles=nt, n_chunks=n_chunks),
        grid=(b, nt),
        in_specs=[pl.BlockSpec((1, ts, P_XBC), cur[0]), halo(prev[0]), halo(nxt[0]),
                  pl.BlockSpec((1, ts, P_XBC), cur[1]), halo(prev[1]), halo(nxt[1]),
                  pl.BlockSpec((1, ts, P_DT), cur[0]), pl.BlockSpec((1, ts, P_DT), cur[1]),
                  pl.BlockSpec((1, 2 * SSD_HEADS, ts), tok_rows[0]), pl.BlockSpec((1, 2 * SSD_HEADS, ts), tok_rows[1])]
        + [_const_spec(c.shape) for c in consts],
        out_specs=[pl.BlockSpec((1, ts, SSD_INNER), cur[0]), pl.BlockSpec((1, ts, SSD_INNER), cur[1])],
        out_shape=[jax.ShapeDtypeStruct((b, l, SSD_INNER), F32)] * 2,
        scratch_shapes=[pltpu.VMEM((2, ts + 2 * CONV_HALO, P_XBC), F32),
                        pltpu.VMEM((2, SSD_GROUPS, SSD_STATE, gw), F32),
                        pltpu.VMEM((2, n_chunks, SSD_GROUPS, SSD_STATE, gw), F32),
                        pltpu.VMEM((2, n_chunks, 1, SSD_INNER), F32),
                        pltpu.VMEM((2, ts, SSD_GROUPS * SSD_STATE), BF16),
                        pltpu.VMEM((2, ts, SSD_INNER), F32)],
        compiler_params=_cparams(("parallel", "arbitrary")),
        name="ssd_scan",
    )(xbc, xbc, xbc, xbc, xbc, xbc, dt, dt, dtt, dtt, *consts)


def _pad_cols(w, width, at=0):
    out = jnp.zeros(w.shape[:-1] + (width,), w.dtype)
    return out.at[..., at:at + w.shape[-1]].set(w)


def _prep_layer(i, ffn1_norm, ffn1_w_gu, ffn1_w_down, mix_norm, w_in, mla_q_norm, mla_w_uq, mla_kv_norm,
                mla_w_ukv, mla_q_gain, mla_k_gain, gla_w_gate, gla_b_gate, gla_o_norm, ssd_conv_w, ssd_conv_b,
                ssd_a_log, ssd_dt_bias, ssd_d, ssd_norm, w_out, ffn2_norm, ffn2_w_gu, ffn2_w_down, final_norm):
    def ffn_w(w_gu, w_down):
        wg = w_gu[:, :D_FF].reshape(D_MODEL, N_FF_CHUNKS, FF_CHUNK).transpose(1, 0, 2).astype(BF16)
        wu = w_gu[:, D_FF:].reshape(D_MODEL, N_FF_CHUNKS, FF_CHUNK).transpose(1, 0, 2).astype(BF16)
        wd = w_down.astype(BF16)
        return wg, wu, wd

    w = {}
    w["n1"] = ffn1_norm[i][None, :]
    w["wg1"], w["wu1"], w["wd1"] = ffn_w(ffn1_w_gu[i], ffn1_w_down[i])
    w["nmix"] = mix_norm[i][None, :]

    wi = w_in[i]
    o_gla = MLA_IN
    o_ssd = MLA_IN + GLA_IN
    o_mla_kr = MLA_Q_LORA + MLA_KV_LORA
    gla_qkv_w = 2 * GLA_QK + GLA_V
    cols = [
        wi[:, :o_mla_kr],
        _pad_cols(wi[:, o_mla_kr:MLA_IN], HEAD_PAD, at=MLA_NOPE),
        wi[:, o_gla:o_gla + gla_qkv_w],
        _pad_cols(wi[:, o_gla + gla_qkv_w + GLA_V:o_ssd], P_LR),
        wi[:, o_gla + gla_qkv_w:o_gla + gla_qkv_w + GLA_V],
        wi[:, o_ssd:o_ssd + SSD_INNER],
        wi[:, o_ssd + SSD_INNER:o_ssd + SSD_INNER + SSD_CONV_DIM],
        _pad_cols(wi[:, o_ssd + SSD_INNER + SSD_CONV_DIM:], P_DT),
    ]
    w["win"] = jnp.concatenate(cols, axis=1).astype(BF16)

    w["qn"] = mla_q_norm[i][None, :]
    w["kvn"] = mla_kv_norm[i][None, :]
    wuq = _pad_cols(mla_w_uq[i].reshape(MLA_Q_LORA, MLA_HEADS, MLA_QK), HEAD_PAD)
    w["wuqt"] = wuq.reshape(MLA_Q_LORA, MLA_HEADS * HEAD_PAD).T.astype(BF16)
    wukv = mla_w_ukv[i].reshape(MLA_KV_LORA, MLA_HEADS, MLA_NOPE + MLA_V)
    w["wk"] = _pad_cols(wukv[..., :MLA_NOPE], HEAD_PAD).reshape(MLA_KV_LORA, MLA_HEADS * HEAD_PAD).astype(BF16)
    w["wvt"] = wukv[..., MLA_NOPE:].reshape(MLA_KV_LORA, MLA_HEADS * MLA_V).T.astype(BF16)
    w["gq"] = (_pad_cols(mla_q_gain[i], HEAD_PAD) * (MLA_QK ** -0.5 * LOG2E))[:, None]
    w["gk"] = _pad_cols(mla_k_gain[i], HEAD_PAD)[None, :]

    wgate = jnp.zeros((2, LANES, GLA_QK), F32)
    for zdir in range(2):
        wgate = wgate.at[zdir, zdir * GLA_GATE_RANK:(zdir + 1) * GLA_GATE_RANK, :].set(gla_w_gate[i, zdir])
    w["wgh"] = wgate.astype(BF16)
    w["wgl"] = (wgate - w["wgh"].astype(F32)).astype(BF16)
    w["bg"] = gla_b_gate[i][:, None, :]
    w["onorm"] = jnp.tile(gla_o_norm[i], GLA_HEADS)[None, :]

    w["cw"] = jnp.zeros((SUBLANES, SSD_CONV_DIM), F32).at[:SSD_CONV].set(ssd_conv_w[i])
    w["cb"] = ssd_conv_b[i][None, :]
    flat_bias = ssd_dt_bias[i].reshape(2 * SSD_HEADS)
    flat_alog = ssd_a_log[i].reshape(2 * SSD_HEADS)
    w["bias_r"] = _pad_cols(flat_bias, P_DT)[None, :]
    w["bias_c"] = flat_bias[:, None]
    w["alog_r"] = _pad_cols(flat_alog, P_DT)[None, :]
    w["alog_c"] = flat_alog[:, None]
    w["dskip"] = jnp.repeat(ssd_d[i], SSD_HEADDIM)[None, :]
    w["snorm"] = ssd_norm[i][None, :]

    w["wout"] = w_out[i].astype(BF16)
    w["n2"] = ffn2_norm[i][None, :]
    w["wg2"], w["wu2"], w["wd2"] = ffn_w(ffn2_w_gu[i], ffn2_w_down[i])
    w["fn"] = final_norm[i][None, :]
    return w


def _block_ones(n, blk):
    idx = np.arange(n) // blk
    return jnp.asarray(idx[:, None] == idx[None, :], BF16)


def _seq_consts(l, tg):
    half = MLA_ROPE // 2
    pos = jnp.arange(l, dtype=F32)
    inv_freq = 1.0 / (ROPE_BASE ** (jnp.arange(0, MLA_ROPE, 2, dtype=F32) / MLA_ROPE))
    ang = pos[:, None] * inv_freq[None, :]
    cos, sin = jnp.cos(ang), jnp.sin(ang)
    c = {}
    c["cos_t"], c["sin_t"] = cos.T, sin.T
    ones = jnp.ones((l, HEAD_PAD), F32)
    c["c_n"] = ones.at[:, MLA_NOPE:MLA_NOPE + half].set(cos).at[:, MLA_NOPE + half:MLA_QK].set(cos)
    zeros = jnp.zeros((l, HEAD_PAD), F32)
    c["s1_n"] = zeros.at[:, MLA_NOPE:MLA_NOPE + half].set(-sin)
    c["s2_n"] = zeros.at[:, MLA_NOPE + half:MLA_QK].set(sin)

    r = np.arange(tg)
    same = (r[:, None] // GLA_CHUNK) == (r[None, :] // GLA_CHUNK)
    c["gla_cum"] = jnp.asarray(np.stack([same & (r[None, :] <= r[:, None]), same & (r[None, :] >= r[:, None])]), BF16)
    c["gla_tot"] = jnp.asarray(same, BF16)
    qk_head = np.arange(GLA_QK) // GLA_DK
    v_head = np.arange(GLA_V) // GLA_DV
    p_col = np.arange(GLA_CHUNK * GLA_QK)
    a_col = np.arange(GLA_HEADS * GLA_CHUNK)
    c["gla_ind"] = jnp.asarray(((p_col % GLA_QK) // GLA_DK * GLA_CHUNK + p_col // GLA_QK)[:, None] == a_col[None, :], BF16)
    c["gla_vmask"] = jnp.asarray((a_col // GLA_CHUNK)[:, None] == v_head[None, :], BF16)
    c["gla_hmask"] = jnp.asarray(v_head[:, None] == qk_head[None, :], F32)

    q = np.arange(SSD_CHUNK)
    c["ssd_tri"] = jnp.asarray(np.stack([q[None, :] <= q[:, None], q[None, :] >= q[:, None]]), BF16)
    xe_row = np.arange(SSD_CHUNK + 2 * CONV_HALO)
    offs = [k - SSD_CONV // 2 for k in range(SSD_CONV) if k != SSD_CONV // 2]
    c["ssd_shift"] = jnp.asarray(np.stack([xe_row[None, :] == (q[:, None] + CONV_HALO + o) for o in offs]), BF16)
    src_lane = np.arange(P_DT)
    c["ssd_rep"] = jnp.asarray(np.stack(
        [src_lane[:, None] == (zdir * SSD_HEADS + np.arange(SSD_INNER) // SSD_HEADDIM)[None, :] for zdir in range(2)]), BF16)
    return c


def _pick_tile(n, pref):
    t = min(n, pref)
    while n % t:
        t //= 2
    return t


def _layer(x, w, c, tg):
    b, l, _ = x.shape
    t = b * l
    tm = _pick_tile(t, 512)
    x1, p_mla, p_qkv, p_lr, p_g, p_z, p_xbc, p_dt = _ffn_in(
        x.reshape(t, D_MODEL), w["n1"], w["wg1"], w["wu1"], w["wd1"], w["nmix"], w["win"], tm)

    qt, k, vt = _mla_prep(p_mla.reshape(b, l, P_MLA), w["qn"], w["wuqt"], w["kvn"], w["wk"], w["wvt"],
                          w["gq"], w["gk"], c["cos_t"], c["sin_t"], c["c_n"], c["s1_n"], c["s2_n"],
                          _pick_tile(l, 512))
    o_mla = _mla_attn(qt, k, vt, _pick_tile(l, 512), _pick_tile(l, 256))

    og_f, og_b = _gla_scan(p_qkv.reshape(b, l, P_QKV), p_lr.reshape(b, l, P_LR), w["wgh"], w["wgl"], w["bg"],
                           c["gla_cum"], c["gla_tot"], c["gla_ind"], c["gla_vmask"], c["gla_hmask"], tg)

    dt3 = p_dt.reshape(b, l, P_DT)
    dtt = jnp.swapaxes(dt3[:, :, :2 * SSD_HEADS], 1, 2)
    ys_f, ys_b = _ssd_scan(p_xbc.reshape(b, l, P_XBC), dt3, dtt, w["cw"], w["cb"], w["bias_r"], w["bias_c"],
                      w["alog_r"], w["alog_c"], w["dskip"], c["ssd_tri"], c["ssd_shift"], c["ssd_rep"],
                      _pick_tile(l // SSD_CHUNK, SSD_TILE_CHUNKS))

    y = _ffn_out(x1, o_mla.reshape(t, MLA_HEADS * MLA_V), og_f.reshape(t, GLA_V), og_b.reshape(t, GLA_V), p_g,
                 ys_f.reshape(t, SSD_INNER), ys_b.reshape(t, SSD_INNER), p_z, w["onorm"], w["snorm"],
                 _block_ones(GLA_V, GLA_DV), _block_ones(SSD_INNER, SSD_INNER // SSD_GROUPS),
                 w["wout"], w["n2"], w["wg2"], w["wu2"], w["wd2"], w["fn"], tm)
    return y.reshape(b, l, D_MODEL)


def kernel(x_prompt, x_sample, ffn1_norm, ffn1_w_gu, ffn1_w_down, mix_norm, w_in, mla_q_norm, mla_w_uq,
           mla_kv_norm, mla_w_ukv, mla_q_gain, mla_k_gain, gla_w_gate, gla_b_gate, gla_o_norm, ssd_conv_w,
           ssd_conv_b, ssd_a_log, ssd_dt_bias, ssd_d, ssd_norm, w_out, ffn2_norm, ffn2_w_gu, ffn2_w_down,
           final_norm):
    params = (ffn1_norm, ffn1_w_gu, ffn1_w_down, mix_norm, w_in, mla_q_norm, mla_w_uq, mla_kv_norm,
              mla_w_ukv, mla_q_gain, mla_k_gain, gla_w_gate, gla_b_gate, gla_o_norm, ssd_conv_w, ssd_conv_b,
              ssd_a_log, ssd_dt_bias, ssd_d, ssd_norm, w_out, ffn2_norm, ffn2_w_gu, ffn2_w_down, final_norm)
    depth = ffn1_norm.shape[0]
    streams = [x_prompt, x_sample]
    tgs = [_pick_tile(s.shape[1], 512) for s in streams]
    consts = [_seq_consts(s.shape[1], tg) for s, tg in zip(streams, tgs)]
    for i in range(depth):
        w = _prep_layer(i, *params)
        streams = [_layer(s, w, c, tg) for s, c, tg in zip(streams, consts, tgs)]
    return tuple(streams)
```

```python
import functools
import itertools

import jax
import jax.numpy as jnp
import numpy as np
from jax import lax
from jax.experimental import pallas as pl
from jax.experimental.pallas import tpu as pltpu

F32 = jnp.float32
BF16 = jnp.bfloat16

D_MODEL = 1024
D_FF = 2816
EPS = 1e-6
MLA_HEADS = 8
MLA_Q_LORA = 384
MLA_KV_LORA = 256
MLA_NOPE = 64
MLA_ROPE = 32
MLA_QK = MLA_NOPE + MLA_ROPE
MLA_V = 64
ROPE_BASE = 10000.0
GLA_HEADS = 4
GLA_DK = 32
GLA_DV = 64
GLA_GATE_RANK = 16
GLA_GATE_NORM = 16.0
GLA_CHUNK = 16
SSD_HEADS = 4
SSD_HEADDIM = 64
SSD_INNER = SSD_HEADS * SSD_HEADDIM
SSD_GROUPS = 2
SSD_STATE = 128
SSD_CONV = 5
SSD_CHUNK = 128
SSD_CONV_DIM = SSD_INNER + 2 * SSD_GROUPS * SSD_STATE
MLA_IN = MLA_Q_LORA + MLA_KV_LORA + MLA_ROPE
GLA_IN = 2 * GLA_HEADS * GLA_DK + 2 * GLA_HEADS * GLA_DV + 2 * GLA_GATE_RANK
SSD_IN = SSD_INNER + SSD_CONV_DIM + 2 * SSD_HEADS

LANES = 128
SUBLANES = 8
MXU_DIM = 256
VMEM_LIMIT = 56 * 1024 * 1024

HEAD_PAD = LANES
V_ROWS = MLA_V + 16
GLA_QK = GLA_HEADS * GLA_DK
GLA_V = GLA_HEADS * GLA_DV
FF_CHUNK = MXU_DIM
N_FF_CHUNKS = D_FF // FF_CHUNK

P_MLA = MLA_Q_LORA + MLA_KV_LORA + HEAD_PAD
P_QKV = 2 * GLA_QK + GLA_V
P_LR = LANES
P_G = GLA_V
P_Z = SSD_INNER
P_XBC = SSD_CONV_DIM
P_DT = LANES
P_TOTAL = P_MLA + P_QKV + P_LR + P_G + P_Z + P_XBC + P_DT

NEG_BIG = -1e30
LOG2E = 1.4426950408889634


def _rms(x, gain):
    return x * lax.rsqrt(jnp.mean(x * x, axis=-1, keepdims=True) + EPS) * gain


def _split_hi_lo(x):
    hi = x.astype(BF16)
    lo = (x - hi.astype(F32)).astype(BF16)
    return hi, lo


def _dot(a, b):
    return jnp.dot(a, b, preferred_element_type=F32)


def _dot_nt(a, b):
    return lax.dot_general(a, b, (((1,), (1,)), ((), ())), preferred_element_type=F32)


def _dot_tn(a, b):
    return lax.dot_general(a, b, (((0,), (0,)), ((), ())), preferred_element_type=F32)


def _cparams(sem):
    return pltpu.CompilerParams(dimension_semantics=sem, vmem_limit_bytes=VMEM_LIMIT)


def _const_spec(shape):
    nd = len(shape)
    return pl.BlockSpec(shape, lambda *_: (0,) * nd, pipeline_mode=pl.Buffered(1))


def _swiglu(h_ref, wg_ref, wu_ref, wd_ref, a_ref):
    for c in range(N_FF_CHUNKS):
        h = h_ref[...]
        g = _dot(h, wg_ref[c])
        u = _dot(h, wu_ref[c])
        a_ref[:, c * FF_CHUNK:(c + 1) * FF_CHUNK] = (g * jax.nn.sigmoid(g) * u).astype(BF16)
    return _dot(a_ref[...], wd_ref[...])


def _ffn_in_kernel(x_ref, n1_ref, wg_ref, wu_ref, wd_ref, n2_ref, win_ref,
                   x1_ref, mla_ref, qkv_ref, lr_ref, g_ref, z_ref, xbc_ref, dt_ref,
                   h_ref, a_ref):
    x = x_ref[...]
    h_ref[...] = _rms(x, n1_ref[...]).astype(BF16)
    x1 = x + 0.5 * _swiglu(h_ref, wg_ref, wu_ref, wd_ref, a_ref)
    x1_ref[...] = x1
    h2 = _rms(x1, n2_ref[...]).astype(BF16)
    off = 0
    for ref, width in ((mla_ref, P_MLA), (qkv_ref, P_QKV), (lr_ref, P_LR), (g_ref, P_G),
                       (z_ref, P_Z), (xbc_ref, P_XBC), (dt_ref, P_DT)):
        ref[...] = _dot(h2, win_ref[:, off:off + width])
        off += width


def _ffn_in(x, n1, wg, wu, wd, n2, win, tm):
    t = x.shape[0]
    widths = (D_MODEL, P_MLA, P_QKV, P_LR, P_G, P_Z, P_XBC, P_DT)
    tok = lambda w: pl.BlockSpec((tm, w), lambda i: (i, 0))
    return pl.pallas_call(
        _ffn_in_kernel,
        grid=(t // tm,),
        in_specs=[tok(D_MODEL), _const_spec(n1.shape), _const_spec(wg.shape), _const_spec(wu.shape),
                  _const_spec(wd.shape), _const_spec(n2.shape), _const_spec(win.shape)],
        out_specs=[tok(w) for w in widths],
        out_shape=[jax.ShapeDtypeStruct((t, w), F32) for w in widths],
        scratch_shapes=[pltpu.VMEM((tm, D_MODEL), BF16), pltpu.VMEM((tm, D_FF), BF16)],
        compiler_params=_cparams(("parallel",)),
        name="ffn_in",
    )(x, n1, wg, wu, wd, n2, win)


def _ffn_out_kernel(x1_ref, omla_ref, ogf_ref, ogb_ref, g_ref, ysf_ref, ysb_ref, z_ref,
                    onorm_ref, snorm_ref, blk64_ref, blk128_ref, wout_ref,
                    n_ref, wg_ref, wu_ref, wd_ref, fn_ref,
                    y_ref, h_ref, a_ref):
    og = ogf_ref[...] + ogb_ref[...]
    ss = _dot((og * og).astype(BF16), blk64_ref[...])
    g = g_ref[...]
    m_gla = og * lax.rsqrt(ss * (1.0 / GLA_DV) + EPS) * onorm_ref[...] * (g * jax.nn.sigmoid(g))
    z = z_ref[...]
    ys = (ysf_ref[...] + ysb_ref[...]) * (z * jax.nn.sigmoid(z))
    ss2 = _dot((ys * ys).astype(BF16), blk128_ref[...])
    m_ssd = ys * lax.rsqrt(ss2 * (1.0 / (SSD_INNER // SSD_GROUPS)) + EPS) * snorm_ref[...]
    m = jnp.concatenate([omla_ref[...], m_gla.astype(BF16), m_ssd.astype(BF16)], axis=-1)
    x2 = x1_ref[...] + _dot(m, wout_ref[...])
    h_ref[...] = _rms(x2, n_ref[...]).astype(BF16)
    x3 = x2 + 0.5 * _swiglu(h_ref, wg_ref, wu_ref, wd_ref, a_ref)
    y_ref[...] = _rms(x3, fn_ref[...])


def _ffn_out(x1, omla, ogf, ogb, g, ysf, ysb, z, onorm, snorm, blk64, blk128, wout, n, wg, wu, wd, fn, tm):
    t = x1.shape[0]
    tok = lambda w: pl.BlockSpec((tm, w), lambda i: (i, 0))
    consts = (onorm, snorm, blk64, blk128, wout, n, wg, wu, wd, fn)
    return pl.pallas_call(
        _ffn_out_kernel,
        grid=(t // tm,),
        in_specs=[tok(D_MODEL), tok(MLA_HEADS * MLA_V), tok(GLA_V), tok(GLA_V), tok(GLA_V), tok(SSD_INNER),
                  tok(SSD_INNER), tok(SSD_INNER)]
        + [_const_spec(c.shape) for c in consts],
        out_specs=tok(D_MODEL),
        out_shape=jax.ShapeDtypeStruct((t, D_MODEL), F32),
        scratch_shapes=[pltpu.VMEM((tm, D_MODEL), BF16), pltpu.VMEM((tm, D_FF), BF16)],
        compiler_params=_cparams(("parallel",)),
        name="ffn_out",
    )(x1, omla, ogf, ogb, g, ysf, ysb, z, *consts)


def _mla_prep_kernel(p_ref, qn_ref, wuqt_ref, kvn_ref, wk_ref, wvt_ref, gq_ref, gk_ref,
                     cost_ref, sint_ref, cn_ref, s1_ref, s2_ref,
                     qt_ref, k_ref, vt_ref):
    p = p_ref[0]
    cq = p[:, :MLA_Q_LORA]
    ckv = p[:, MLA_Q_LORA:MLA_Q_LORA + MLA_KV_LORA]
    kr = p[:, MLA_Q_LORA + MLA_KV_LORA:]
    hq = _rms(cq, qn_ref[...]).astype(BF16)
    hkv = _rms(ckv, kvn_ref[...]).astype(BF16)

    qt = _dot_nt(wuqt_ref[...], hq)
    cos_t = cost_ref[...]
    sin_t = sint_ref[...]
    gq = gq_ref[...]
    half = MLA_ROPE // 2
    for h in range(MLA_HEADS):
        x = qt[h * HEAD_PAD:(h + 1) * HEAD_PAD]
        ss = jnp.sum(x * x, axis=0, keepdims=True)
        x = x * lax.rsqrt(ss * (1.0 / MLA_QK) + EPS) * gq
        x1 = x[MLA_NOPE:MLA_NOPE + half]
        x2 = x[MLA_NOPE + half:MLA_QK]
        qt_ref[0, h, 0:MLA_NOPE, :] = x[0:MLA_NOPE].astype(BF16)
        qt_ref[0, h, MLA_NOPE:MLA_NOPE + half, :] = (x1 * cos_t - x2 * sin_t).astype(BF16)
        qt_ref[0, h, MLA_NOPE + half:MLA_QK, :] = (x1 * sin_t + x2 * cos_t).astype(BF16)
        qt_ref[0, h, MLA_QK:HEAD_PAD, :] = x[MLA_QK:HEAD_PAD].astype(BF16)

    gk = gk_ref[...]
    krg = kr * gk
    k_rot = (krg * cn_ref[...] + pltpu.roll(krg, HEAD_PAD - half, axis=1) * s1_ref[...]
             + pltpu.roll(krg, half, axis=1) * s2_ref[...])
    ss_rope = jnp.sum(kr * kr, axis=-1, keepdims=True)
    kn = _dot(hkv, wk_ref[...])
    for h in range(MLA_HEADS):
        x = kn[:, h * HEAD_PAD:(h + 1) * HEAD_PAD]
        ss = jnp.sum(x * x, axis=-1, keepdims=True) + ss_rope
        k_ref[0, h] = ((x * gk + k_rot) * lax.rsqrt(ss * (1.0 / MLA_QK) + EPS)).astype(BF16)

    vt = _dot_nt(wvt_ref[...], hkv)
    ones_tile = jnp.where(lax.broadcasted_iota(jnp.int32, (V_ROWS - MLA_V, vt.shape[1]), 0) == 0, 1.0, 0.0).astype(BF16)
    for h in range(MLA_HEADS):
        vt_ref[0, h, 0:MLA_V, :] = vt[h * MLA_V:(h + 1) * MLA_V].astype(BF16)
        vt_ref[0, h, MLA_V:V_ROWS, :] = ones_tile


def _mla_prep(p_mla, qn, wuqt, kvn, wk, wvt, gq, gk, cos_t, sin_t, c_n, s1_n, s2_n, tm):
    b, l, _ = p_mla.shape
    consts = (qn, wuqt, kvn, wk, wvt, gq, gk)
    half = MLA_ROPE // 2
    return pl.pallas_call(
        _mla_prep_kernel,
        grid=(b, l // tm),
        in_specs=[pl.BlockSpec((1, tm, P_MLA), lambda i, j: (i, j, 0))]
        + [_const_spec(c.shape) for c in consts]
        + [pl.BlockSpec((half, tm), lambda i, j: (0, j)), pl.BlockSpec((half, tm), lambda i, j: (0, j)),
           pl.BlockSpec((tm, HEAD_PAD), lambda i, j: (j, 0)), pl.BlockSpec((tm, HEAD_PAD), lambda i, j: (j, 0)),
           pl.BlockSpec((tm, HEAD_PAD), lambda i, j: (j, 0))],
        out_specs=[pl.BlockSpec((1, MLA_HEADS, HEAD_PAD, tm), lambda i, j: (i, 0, 0, j)),
                   pl.BlockSpec((1, MLA_HEADS, tm, HEAD_PAD), lambda i, j: (i, 0, j, 0)),
                   pl.BlockSpec((1, MLA_HEADS, V_ROWS, tm), lambda i, j: (i, 0, 0, j))],
        out_shape=[jax.ShapeDtypeStruct((b, MLA_HEADS, HEAD_PAD, l), BF16),
                   jax.ShapeDtypeStruct((b, MLA_HEADS, l, HEAD_PAD), BF16),
                   jax.ShapeDtypeStruct((b, MLA_HEADS, V_ROWS, l), BF16)],
        compiler_params=_cparams(("parallel", "parallel")),
        name="mla_prep",
    )(p_mla, *consts, cos_t, sin_t, c_n, s1_n, s2_n)


ATTN_HEADS_PER_STEP = 2
ATTN_LOOKAHEAD = 2


def _mla_attn_kernel(qt_ref, k_ref, vt_ref, o_ref, ot_ref, s_ref, *, tk, lookahead):
    n_kb = k_ref.shape[2] // tk
    tq = qt_ref.shape[3]
    n_slots = lookahead + 1
    total = ATTN_HEADS_PER_STEP * n_kb
    qts = [qt_ref[0, hh] for hh in range(ATTN_HEADS_PER_STEP)]

    def scores(g):
        hh, kb = divmod(g, n_kb)
        return _dot(k_ref[0, hh, kb * tk:(kb + 1) * tk, :], qts[hh])

    for g in range(min(lookahead, total)):
        s_ref[g % n_slots] = scores(g)
    m = acc = None
    for g in range(total):
        hh, kb = divmod(g, n_kb)
        if kb == 0:
            m = jnp.full((1, tq), -jnp.inf, F32)
            acc = jnp.zeros((V_ROWS, tq), F32)
        if g + lookahead < total:
            s_ref[(g + lookahead) % n_slots] = scores(g + lookahead)
        s = s_ref[g % n_slots]
        m_new = jnp.maximum(m, jnp.max(s, axis=0, keepdims=True))
        alpha = jnp.exp2(m - m_new)
        p = jnp.exp2(s - m_new)
        acc = alpha * acc + _dot(vt_ref[0, hh, :, kb * tk:(kb + 1) * tk], p.astype(BF16))
        m = m_new
        if kb == n_kb - 1:
            ot_ref[hh * MLA_V:(hh + 1) * MLA_V, :] = acc[0:MLA_V] * (1.0 / acc[MLA_V:MLA_V + 1])
    o_ref[0] = ot_ref[...].T.astype(BF16)


def _mla_attn(qt, k, vt, tq, tk):
    b, nh, _, l = qt.shape
    hp = ATTN_HEADS_PER_STEP
    lookahead = min(ATTN_LOOKAHEAD, hp * (l // tk) - 1)
    return pl.pallas_call(
        functools.partial(_mla_attn_kernel, tk=tk, lookahead=lookahead),
        grid=(b, nh // hp, l // tq),
        in_specs=[pl.BlockSpec((1, hp, HEAD_PAD, tq), lambda i, h, j: (i, h, 0, j)),
                  pl.BlockSpec((1, hp, l, HEAD_PAD), lambda i, h, j: (i, h, 0, 0)),
                  pl.BlockSpec((1, hp, V_ROWS, l), lambda i, h, j: (i, h, 0, 0))],
        out_specs=pl.BlockSpec((1, tq, hp * MLA_V), lambda i, h, j: (i, j, h)),
        out_shape=jax.ShapeDtypeStruct((b, l, nh * MLA_V), BF16),
        scratch_shapes=[pltpu.VMEM((hp * MLA_V, tq), F32),
                        pltpu.VMEM((lookahead + 1, tk, tq), F32)],
        compiler_params=_cparams(("parallel", "parallel", "arbitrary")),
        name="mla_attn",
    )(qt, k, vt)


def _gla_kernel(qkv_f_ref, qkv_b_ref, lr_f_ref, lr_b_ref, wgh_ref, wgl_ref, bg_ref, cum_ref, tot_ref, ind_ref,
                vmask_ref, hmask_ref, o_f_ref, o_b_ref, st_ref, upd_ref, p_ref, *, tg):
    n_chunks = tg // GLA_CHUNK
    chunk_rows = [slice(c * GLA_CHUNK, (c + 1) * GLA_CHUNK) for c in range(n_chunks)]
    hmask = hmask_ref[...]
    vkeep = vmask_ref[...] > 0

    @pl.when(pl.program_id(1) == 0)
    def _():
        st_ref[...] = jnp.zeros_like(st_ref)

    def head(d, qkv_ref, lr_ref):
        qkv = qkv_ref[0]
        q = qkv[:, :GLA_QK] * (GLA_DK ** -0.5)
        k = qkv[:, GLA_QK:2 * GLA_QK]
        v = qkv[:, 2 * GLA_QK:]
        lr_hi, lr_lo = _split_hi_lo(lr_ref[0])
        zg = _dot(lr_hi, wgh_ref[d]) + _dot(lr_lo, wgh_ref[d]) + _dot(lr_hi, wgl_ref[d]) + bg_ref[d]
        log_a = (jnp.minimum(zg, 0.0) - jnp.log1p(jnp.exp(-jnp.abs(zg)))) * (1.0 / GLA_GATE_NORM)
        la_hi, la_lo = _split_hi_lo(log_a)
        bcum = _dot(cum_ref[d], la_hi) + _dot(cum_ref[d], la_lo)
        btot = _dot(tot_ref[...], la_hi) + _dot(tot_ref[...], la_lo)
        return dict(q=q, k=k, bcum=bcum, qd=(q * jnp.exp(bcum)).astype(BF16),
                    kd=(k * jnp.exp(btot - bcum)).astype(BF16), vb=v.astype(BF16), dec=jnp.exp(btot))

    def products(d, h):
        q3 = h["q"].reshape(n_chunks, GLA_CHUNK, GLA_QK)
        k3 = h["k"].reshape(n_chunks, GLA_CHUNK, GLA_QK)
        b3 = h["bcum"].reshape(n_chunks, GLA_CHUNK, GLA_QK)
        tin = lax.broadcasted_iota(jnp.int32, (n_chunks, GLA_CHUNK, GLA_QK), 1)
        pending = list(enumerate(chunk_rows))
        for j in range(GLA_CHUNK):
            for c, rows in pending[j::GLA_CHUNK]:
                upd_ref[d, c] = _dot_tn(h["vb"][rows], h["kd"][rows]) * hmask
            valid = (tin >= j) if d == 0 else (tin <= j)
            e = jnp.exp(jnp.where(valid, b3 - b3[:, j:j + 1, :], NEG_BIG))
            p_ref[d, :, j * GLA_QK:(j + 1) * GLA_QK] = (
                q3 * k3[:, j:j + 1, :] * e).reshape(tg, GLA_QK).astype(BF16)

    def outputs(d, h, o_ref):
        a_intra = _dot(p_ref[d], ind_ref[...]).astype(BF16)
        st = st_ref[d]
        for c in (range(n_chunks) if d == 0 else reversed(range(n_chunks))):
            rows = chunk_rows[c]
            v_tiled = jnp.concatenate([h["vb"][rows]] * GLA_HEADS, axis=0)
            v_rows = jnp.where(vkeep, v_tiled, jnp.zeros_like(v_tiled))
            lhs = jnp.concatenate([h["qd"][rows], a_intra[rows]], axis=1)
            o_ref[0, rows, :] = _dot(lhs, jnp.concatenate([st.T.astype(BF16), v_rows], axis=0))
            st = st * h["dec"][c * GLA_CHUNK:c * GLA_CHUNK + 1, :] + upd_ref[d, c]
            yield
        st_ref[d] = st

    h_f = head(0, qkv_f_ref, lr_f_ref)
    h_b = head(1, qkv_b_ref, lr_b_ref)
    products(0, h_f)
    products(1, h_b)
    for _ in itertools.zip_longest(outputs(0, h_f, o_f_ref), outputs(1, h_b, o_b_ref)):
        pass


def _gla_scan(qkv, lr, wgh, wgl, bg, cum, tot, ind, vmask, hmask, tg):
    b, l, _ = qkv.shape
    nt = l // tg
    n_chunks = tg // GLA_CHUNK
    fwd = lambda i, t: (i, t, 0)
    bwd = lambda i, t: (i, nt - 1 - t, 0)
    consts = (wgh, wgl, bg, cum, tot, ind, vmask, hmask)
    return pl.pallas_call(
        functools.partial(_gla_kernel, tg=tg),
        grid=(b, nt),
        in_specs=[pl.BlockSpec((1, tg, P_QKV), fwd), pl.BlockSpec((1, tg, P_QKV), bwd),
                  pl.BlockSpec((1, tg, P_LR), fwd), pl.BlockSpec((1, tg, P_LR), bwd)]
        + [_const_spec(c.shape) for c in consts],
        out_specs=[pl.BlockSpec((1, tg, GLA_V), fwd), pl.BlockSpec((1, tg, GLA_V), bwd)],
        out_shape=[jax.ShapeDtypeStruct((b, l, GLA_V), F32)] * 2,
        scratch_shapes=[pltpu.VMEM((2, GLA_V, GLA_QK), F32),
                        pltpu.VMEM((2, n_chunks, GLA_V, GLA_QK), F32),
                        pltpu.VMEM((2, tg, GLA_CHUNK * GLA_QK), BF16)],
        compiler_params=_cparams(("parallel", "arbitrary")),
        name="gla_scan",
    )(qkv, qkv, lr, lr, *consts)


CONV_HALO = SUBLANES
SSD_TILE_CHUNKS = 8


def _softplus(x):
    return jnp.maximum(x, 0.0) + jnp.log1p(jnp.exp(-jnp.abs(x)))


def _ssd_kernel(xc_f_ref, xp_f_ref, xn_f_ref, xc_b_ref, xp_b_ref, xn_b_ref, dt_f_ref, dt_b_ref, dtt_f_ref, dtt_b_ref,
                cw_ref, cb_ref, bias_r_ref, bias_c_ref, alog_r_ref, alog_c_ref, dskip_ref, tri_ref, shift_ref, rep_ref,
                y_f_ref, y_b_ref, xe_ref, st_ref, upd_ref, dec_ref, cgb_ref, e64_ref, *, n_tiles, n_chunks):
    t = pl.program_id(1)
    cs = SSD_CHUNK
    ts = n_chunks * cs
    hpg = SSD_HEADS // SSD_GROUPS
    gw = hpg * SSD_HEADDIM
    pad = SSD_CONV // 2
    neg_a_r = -jnp.exp(alog_r_ref[...])
    neg_a_c = -jnp.exp(alog_c_ref[...])
    lane_head = lax.broadcasted_iota(jnp.int32, (cs, gw), 1) // SSD_HEADDIM

    @pl.when(t == 0)
    def _():
        st_ref[...] = jnp.zeros_like(st_ref)

    def direction(d, xc_ref, xp_ref, xn_ref, dt_ref, dtt_ref, y_ref):
        tt = t if d == 0 else n_tiles - 1 - t
        xe_ref[d, 0:CONV_HALO, :] = jnp.where(tt == 0, 0.0, xp_ref[0])
        xe_ref[d, CONV_HALO:CONV_HALO + ts, :] = xc_ref[0]
        xe_ref[d, CONV_HALO + ts:, :] = jnp.where(tt == n_tiles - 1, 0.0, xn_ref[0])
        tri_d = tri_ref[d]
        tri_o = tri_ref[1 - d]
        mask = tri_d.astype(F32) > 0.5
        rep64 = rep_ref[d]

        def front(c):
            rows = slice(c * cs, (c + 1) * cs)
            xe_b = xe_ref[d, c * cs:(c + 1) * cs + 2 * CONV_HALO, :].astype(BF16)
            acc = cb_ref[...] + cw_ref[pad:pad + 1, :] * xe_ref[d, CONV_HALO + c * cs:CONV_HALO + (c + 1) * cs, :]
            for i, kk in enumerate(k for k in range(SSD_CONV) if k != pad):
                acc = acc + cw_ref[kk:kk + 1, :] * _dot(shift_ref[i], xe_b)
            act = acc * jax.nn.sigmoid(acc)
            xs = act[:, :SSD_INNER]
            bm = act[:, SSD_INNER:SSD_INNER + SSD_GROUPS * SSD_STATE]
            cmb = act[:, SSD_INNER + SSD_GROUPS * SSD_STATE:].astype(BF16)
            dt_c = _softplus(dt_ref[0, rows, :] + bias_r_ref[...])
            la_c = dt_c * neg_a_r
            dt_r = _softplus(dtt_ref[0, :, rows] + bias_c_ref[...])
            la_r = dt_r * neg_a_c
            lc_hi, lc_lo = _split_hi_lo(la_c)
            cum_c = _dot(tri_d, lc_hi) + _dot(tri_d, lc_lo)
            lr_hi, lr_lo = _split_hi_lo(la_r)
            cum_r = _dot(lr_hi, tri_o) + _dot(lr_lo, tri_o)
            dt_hi, dt_lo = _split_hi_lo(dt_c)
            dt64 = _dot(dt_hi, rep64) + _dot(dt_lo, rep64)
            cu_hi, cu_lo = _split_hi_lo(cum_c)
            cum64 = _dot(cu_hi, rep64) + _dot(cu_lo, rep64)
            return xs, bm, cmb, cum_r, dt64, cum64

        def back(c, vals):
            xs, bm, cmb, cum_r, dt64, cum64 = vals
            rows = slice(c * cs, (c + 1) * cs)
            tot64 = cum64[cs - 1:cs, :] if d == 0 else cum64[0:1, :]
            xc = xs * dt64
            xcb = xc.astype(BF16)
            xdec = (xc * jnp.exp(tot64 - cum64)).astype(BF16)
            e64_ref[d, rows, :] = jnp.exp(cum64)
            dec_ref[d, c] = jnp.exp(tot64)
            cgb_ref[d, rows, :] = cmb
            ys = []
            for g in range(SSD_GROUPS):
                gl = slice(g * gw, (g + 1) * gw)
                bg = bm[:, g * SSD_STATE:(g + 1) * SSD_STATE]
                cb = _dot_nt(cmb[:, g * SSD_STATE:(g + 1) * SSD_STATE], bg.astype(BF16))
                y_g = jnp.zeros((cs, gw), F32)
                cum_g = cum64[:, gl]
                cum_swapped = pltpu.roll(cum_g, SSD_HEADDIM, axis=1)
                for hg in range(hpg):
                    h = d * SSD_HEADS + g * hpg + hg
                    cum_h = jnp.where(lane_head == hg, cum_g, cum_swapped)
                    seg = jnp.exp(jnp.where(mask, cum_h - cum_r[h:h + 1, :], NEG_BIG))
                    yd = _dot((cb * seg).astype(BF16), xcb[:, gl])
                    y_g = y_g + jnp.where(lane_head == hg, yd, 0.0)
                ys.append(y_g)
                upd_ref[d, c, g] = _dot(bg.T.astype(BF16), xdec[:, gl])
            y = jnp.concatenate(ys, axis=1)
            y_ref[0, rows, :] = y + dskip_ref[...] * xs if d == 0 else y

        vals = front(0)
        for c in range(n_chunks):
            nxt = front(c + 1) if c + 1 < n_chunks else None
            back(c, vals)
            vals = nxt
            yield

        st = [st_ref[d, g] for g in range(SSD_GROUPS)]
        for c in (range(n_chunks) if d == 0 else reversed(range(n_chunks))):
            rows = slice(c * cs, (c + 1) * cs)
            y_off = jnp.concatenate(
                [_dot(cgb_ref[d, rows, g * SSD_STATE:(g + 1) * SSD_STATE], st[g].astype(BF16))
                 for g in range(SSD_GROUPS)], axis=1)
            y_ref[0, rows, :] = y_ref[0, rows, :] + e64_ref[d, rows, :] * y_off
            dec = dec_ref[d, c]
            st = [st[g] * dec[:, g * gw:(g + 1) * gw] + upd_ref[d, c, g] for g in range(SSD_GROUPS)]
            yield
        for g in range(SSD_GROUPS):
            st_ref[d, g] = st[g]

    for _ in itertools.zip_longest(direction(0, xc_f_ref, xp_f_ref, xn_f_ref, dt_f_ref, dtt_f_ref, y_f_ref),
                                   direction(1, xc_b_ref, xp_b_ref, xn_b_ref, dt_b_ref, dtt_b_ref, y_b_ref)):
        pass


def _ssd_scan(xbc, dt, dtt, cw, cb, bias_r, bias_c, alog_r, alog_c, dskip, tri, shift, rep, n_chunks):
    b, l, _ = xbc.shape
    ts = n_chunks * SSD_CHUNK
    nt = l // ts
    hb = ts // CONV_HALO
    n_hb = l // CONV_HALO
    gw = SSD_INNER // SSD_GROUPS
    tiles = (lambda t: t, lambda t: nt - 1 - t)
    cur = [lambda i, t, f=f: (i, f(t), 0) for f in tiles]
    prev = [lambda i, t, f=f: (i, jnp.maximum(f(t) * hb - 1, 0), 0) for f in tiles]
    nxt = [lambda i, t, f=f: (i, jnp.minimum((f(t) + 1) * hb, n_hb - 1), 0) for f in tiles]
    tok_rows = [lambda i, t, f=f: (i, 0, f(t)) for f in tiles]
    consts = (cw, cb, bias_r, bias_c, alog_r, alog_c, dskip, tri, shift, rep)
    halo = lambda m: pl.BlockSpec((1, CONV_HALO, P_XBC), m)
    return pl.pallas_call(
        functools.partial(_ssd_kernel, n_tiles=nt, n_chunks=n_chunks),
        grid=(b, nt),
        in_specs=[pl.BlockSpec((1, ts, P_XBC), cur[0]), halo(prev[0]), halo(nxt[0]),
                  pl.BlockSpec((1, ts, P_XBC), cur[1]), halo(prev[1]), halo(nxt[1]),
                  pl.BlockSpec((1, ts, P_DT), cur[0]), pl.BlockSpec((1, ts, P_DT), cur[1]),
                  pl.BlockSpec((1, 2 * SSD_HEADS, ts), tok_rows[0]), pl.BlockSpec((1, 2 * SSD_HEADS, ts), tok_rows[1])]
        + [_const_spec(c.shape) for c in consts],
        out_specs=[pl.BlockSpec((1, ts, SSD_INNER), cur[0]), pl.BlockSpec((1, ts, SSD_INNER), cur[1])],
        out_shape=[jax.ShapeDtypeStruct((b, l, SSD_INNER), F32)] * 2,
        scratch_shapes=[pltpu.VMEM((2, ts + 2 * CONV_HALO, P_XBC), F32),
                        pltpu.VMEM((2, SSD_GROUPS, SSD_STATE, gw), F32),
                        pltpu.VMEM((2, n_chunks, SSD_GROUPS, SSD_STATE, gw), F32),
                        pltpu.VMEM((2, n_chunks, 1, SSD_INNER), F32),
                        pltpu.VMEM((2, ts, SSD_GROUPS * SSD_STATE), BF16),
                        pltpu.VMEM((2, ts, SSD_INNER), F32)],
        compiler_params=_cparams(("parallel", "arbitrary")),
        name="ssd_scan",
    )(xbc, xbc, xbc, xbc, xbc, xbc, dt, dt, dtt, dtt, *consts)


def _pad_cols(w, width, at=0):
    out = jnp.zeros(w.shape[:-1] + (width,), w.dtype)
    return out.at[..., at:at + w.shape[-1]].set(w)


def _prep_layer(i, ffn1_norm, ffn1_w_gu, ffn1_w_down, mix_norm, w_in, mla_q_norm, mla_w_uq, mla_kv_norm,
                mla_w_ukv, mla_q_gain, mla_k_gain, gla_w_gate, gla_b_gate, gla_o_norm, ssd_conv_w, ssd_conv_b,
                ssd_a_log, ssd_dt_bias, ssd_d, ssd_norm, w_out, ffn2_norm, ffn2_w_gu, ffn2_w_down, final_norm):
    def ffn_w(w_gu, w_down):
        wg = w_gu[:, :D_FF].reshape(D_MODEL, N_FF_CHUNKS, FF_CHUNK).transpose(1, 0, 2).astype(BF16)
        wu = w_gu[:, D_FF:].reshape(D_MODEL, N_FF_CHUNKS, FF_CHUNK).transpose(1, 0, 2).astype(BF16)
        wd = w_down.astype(BF16)
        return wg, wu, wd

    w = {}
    w["n1"] = ffn1_norm[i][None, :]
    w["wg1"], w["wu1"], w["wd1"] = ffn_w(ffn1_w_gu[i], ffn1_w_down[i])
    w["nmix"] = mix_norm[i][None, :]

    wi = w_in[i]
    o_gla = MLA_IN
    o_ssd = MLA_IN + GLA_IN
    o_mla_kr = MLA_Q_LORA + MLA_KV_LORA
    gla_qkv_w = 2 * GLA_QK + GLA_V
    cols = [
        wi[:, :o_mla_kr],
        _pad_cols(wi[:, o_mla_kr:MLA_IN], HEAD_PAD, at=MLA_NOPE),
        wi[:, o_gla:o_gla + gla_qkv_w],
        _pad_cols(wi[:, o_gla + gla_qkv_w + GLA_V:o_ssd], P_LR),
        wi[:, o_gla + gla_qkv_w:o_gla + gla_qkv_w + GLA_V],
        wi[:, o_ssd:o_ssd + SSD_INNER],
        wi[:, o_ssd + SSD_INNER:o_ssd + SSD_INNER + SSD_CONV_DIM],
        _pad_cols(wi[:, o_ssd + SSD_INNER + SSD_CONV_DIM:], P_DT),
    ]
    w["win"] = jnp.concatenate(cols, axis=1).astype(BF16)

    w["qn"] = mla_q_norm[i][None, :]
    w["kvn"] = mla_kv_norm[i][None, :]
    wuq = _pad_cols(mla_w_uq[i].reshape(MLA_Q_LORA, MLA_HEADS, MLA_QK), HEAD_PAD)
    w["wuqt"] = wuq.reshape(MLA_Q_LORA, MLA_HEADS * HEAD_PAD).T.astype(BF16)
    wukv = mla_w_ukv[i].reshape(MLA_KV_LORA, MLA_HEADS, MLA_NOPE + MLA_V)
    w["wk"] = _pad_cols(wukv[..., :MLA_NOPE], HEAD_PAD).reshape(MLA_KV_LORA, MLA_HEADS * HEAD_PAD).astype(BF16)
    w["wvt"] = wukv[..., MLA_NOPE:].reshape(MLA_KV_LORA, MLA_HEADS * MLA_V).T.astype(BF16)
    w["gq"] = (_pad_cols(mla_q_gain[i], HEAD_PAD) * (MLA_QK ** -0.5 * LOG2E))[:, None]
    w["gk"] = _pad_cols(mla_k_gain[i], HEAD_PAD)[None, :]

    wgate = jnp.zeros((2, LANES, GLA_QK), F32)
    for zdir in range(2):
        wgate = wgate.at[zdir, zdir * GLA_GATE_RANK:(zdir + 1) * GLA_GATE_RANK, :].set(gla_w_gate[i, zdir])
    w["wgh"] = wgate.astype(BF16)
    w["wgl"] = (wgate - w["wgh"].astype(F32)).astype(BF16)
    w["bg"] = gla_b_gate[i][:, None, :]
    w["onorm"] = jnp.tile(gla_o_norm[i], GLA_HEADS)[None, :]

    w["cw"] = jnp.zeros((SUBLANES, SSD_CONV_DIM), F32).at[:SSD_CONV].set(ssd_conv_w[i])
    w["cb"] = ssd_conv_b[i][None, :]
    flat_bias = ssd_dt_bias[i].reshape(2 * SSD_HEADS)
    flat_alog = ssd_a_log[i].reshape(2 * SSD_HEADS)
    w["bias_r"] = _pad_cols(flat_bias, P_DT)[None, :]
    w["bias_c"] = flat_bias[:, None]
    w["alog_r"] = _pad_cols(flat_alog, P_DT)[None, :]
    w["alog_c"] = flat_alog[:, None]
    w["dskip"] = jnp.repeat(ssd_d[i], SSD_HEADDIM)[None, :]
    w["snorm"] = ssd_norm[i][None, :]

    w["wout"] = w_out[i].astype(BF16)
    w["n2"] = ffn2_norm[i][None, :]
    w["wg2"], w["wu2"], w["wd2"] = ffn_w(ffn2_w_gu[i], ffn2_w_down[i])
    w["fn"] = final_norm[i][None, :]
    return w


def _block_ones(n, blk):
    idx = np.arange(n) // blk
    return jnp.asarray(idx[:, None] == idx[None, :], BF16)


def _seq_consts(l, tg):
    half = MLA_ROPE // 2
    pos = jnp.arange(l, dtype=F32)
    inv_freq = 1.0 / (ROPE_BASE ** (jnp.arange(0, MLA_ROPE, 2, dtype=F32) / MLA_ROPE))
    ang = pos[:, None] * inv_freq[None, :]
    cos, sin = jnp.cos(ang), jnp.sin(ang)
    c = {}
    c["cos_t"], c["sin_t"] = cos.T, sin.T
    ones = jnp.ones((l, HEAD_PAD), F32)
    c["c_n"] = ones.at[:, MLA_NOPE:MLA_NOPE + half].set(cos).at[:, MLA_NOPE + half:MLA_QK].set(cos)
    zeros = jnp.zeros((l, HEAD_PAD), F32)
    c["s1_n"] = zeros.at[:, MLA_NOPE:MLA_NOPE + half].set(-sin)
    c["s2_n"] = zeros.at[:, MLA_NOPE + half:MLA_QK].set(sin)

    r = np.arange(tg)
    same = (r[:, None] // GLA_CHUNK) == (r[None, :] // GLA_CHUNK)
    c["gla_cum"] = jnp.asarray(np.stack([same & (r[None, :] <= r[:, None]), same & (r[None, :] >= r[:, None])]), BF16)
    c["gla_tot"] = jnp.asarray(same, BF16)
    qk_head = np.arange(GLA_QK) // GLA_DK
    v_head = np.arange(GLA_V) // GLA_DV
    p_col = np.arange(GLA_CHUNK * GLA_QK)
    a_col = np.arange(GLA_HEADS * GLA_CHUNK)
    c["gla_ind"] = jnp.asarray(((p_col % GLA_QK) // GLA_DK * GLA_CHUNK + p_col // GLA_QK)[:, None] == a_col[None, :], BF16)
    c["gla_vmask"] = jnp.asarray((a_col // GLA_CHUNK)[:, None] == v_head[None, :], BF16)
    c["gla_hmask"] = jnp.asarray(v_head[:, None] == qk_head[None, :], F32)

    q = np.arange(SSD_CHUNK)
    c["ssd_tri"] = jnp.asarray(np.stack([q[None, :] <= q[:, None], q[None, :] >= q[:, None]]), BF16)
    xe_row = np.arange(SSD_CHUNK + 2 * CONV_HALO)
    offs = [k - SSD_CONV // 2 for k in range(SSD_CONV) if k != SSD_CONV // 2]
    c["ssd_shift"] = jnp.asarray(np.stack([xe_row[None, :] == (q[:, None] + CONV_HALO + o) for o in offs]), BF16)
    src_lane = np.arange(P_DT)
    c["ssd_rep"] = jnp.asarray(np.stack(
        [src_lane[:, None] == (zdir * SSD_HEADS + np.arange(SSD_INNER) // SSD_HEADDIM)[None, :] for zdir in range(2)]), BF16)
    return c


def _pick_tile(n, pref):
    t = min(n, pref)
    while n % t:
        t //= 2
    return t


def _layer(x, w, c, tg):
    b, l, _ = x.shape
    t = b * l
    tm = _pick_tile(t, 512)
    x1, p_mla, p_qkv, p_lr, p_g, p_z, p_xbc, p_dt = _ffn_in(
        x.reshape(t, D_MODEL), w["n1"], w["wg1"], w["wu1"], w["wd1"], w["nmix"], w["win"], tm)

    qt, k, vt = _mla_prep(p_mla.reshape(b, l, P_MLA), w["qn"], w["wuqt"], w["kvn"], w["wk"], w["wvt"],
                          w["gq"], w["gk"], c["cos_t"], c["sin_t"], c["c_n"], c["s1_n"], c["s2_n"],
                          _pick_tile(l, 512))
    o_mla = _mla_attn(qt, k, vt, _pick_tile(l, 512), _pick_tile(l, 256))

    og_f, og_b = _gla_scan(p_qkv.reshape(b, l, P_QKV), p_lr.reshape(b, l, P_LR), w["wgh"], w["wgl"], w["bg"],
                           c["gla_cum"], c["gla_tot"], c["gla_ind"], c["gla_vmask"], c["gla_hmask"], tg)

    dt3 = p_dt.reshape(b, l, P_DT)
    dtt = jnp.swapaxes(dt3[:, :, :2 * SSD_HEADS], 1, 2)
    ys_f, ys_b = _ssd_scan(p_xbc.reshape(b, l, P_XBC), dt3, dtt, w["cw"], w["cb"], w["bias_r"], w["bias_c"],
                      w["alog_r"], w["alog_c"], w["dskip"], c["ssd_tri"], c["ssd_shift"], c["ssd_rep"],
                      _pick_tile(l // SSD_CHUNK, SSD_TILE_CHUNKS))

    y = _ffn_out(x1, o_mla.reshape(t, MLA_HEADS * MLA_V), og_f.reshape(t, GLA_V), og_b.reshape(t, GLA_V), p_g,
                 ys_f.reshape(t, SSD_INNER), ys_b.reshape(t, SSD_INNER), p_z, w["onorm"], w["snorm"],
                 _block_ones(GLA_V, GLA_DV), _block_ones(SSD_INNER, SSD_INNER // SSD_GROUPS),
                 w["wout"], w["n2"], w["wg2"], w["wu2"], w["wd2"], w["fn"], tm)
    return y.reshape(b, l, D_MODEL)


def kernel(x_prompt, x_sample, ffn1_norm, ffn1_w_gu, ffn1_w_down, mix_norm, w_in, mla_q_norm, mla_w_uq,
           mla_kv_norm, mla_w_ukv, mla_q_gain, mla_k_gain, gla_w_gate, gla_b_gate, gla_o_norm, ssd_conv_w,
           ssd_conv_b, ssd_a_log, ssd_dt_bias, ssd_d, ssd_norm, w_out, ffn2_norm, ffn2_w_gu, ffn2_w_down,
           final_norm):
    params = (ffn1_norm, ffn1_w_gu, ffn1_w_down, mix_norm, w_in, mla_q_norm, mla_w_uq, mla_kv_norm,
              mla_w_ukv, mla_q_gain, mla_k_gain, gla_w_gate, gla_b_gate, gla_o_norm, ssd_conv_w, ssd_conv_b,
              ssd_a_log, ssd_dt_bias, ssd_d, ssd_norm, w_out, ffn2_norm, ffn2_w_gu, ffn2_w_down, final_norm)
    depth = ffn1_norm.shape[0]
    streams = [x_prompt, x_sample]
    tgs = [_pick_tile(s.shape[1], 256) for s in streams]
    consts = [_seq_consts(s.shape[1], tg) for s, tg in zip(streams, tgs)]
    for i in range(depth):
        w = _prep_layer(i, *params)
        streams = [_layer(s, w, c, tg) for s, c, tg in zip(streams, consts, tgs)]
    return tuple(streams)
```

```python
import functools
import itertools

import jax
import jax.numpy as jnp
import numpy as np
from jax import lax
from jax.experimental import pallas as pl
from jax.experimental.pallas import tpu as pltpu

F32 = jnp.float32
BF16 = jnp.bfloat16

D_MODEL = 1024
D_FF = 2816
EPS = 1e-6
MLA_HEADS = 8
MLA_Q_LORA = 384
MLA_KV_LORA = 256
MLA_NOPE = 64
MLA_ROPE = 32
MLA_QK = MLA_NOPE + MLA_ROPE
MLA_V = 64
ROPE_BASE = 10000.0
GLA_HEADS = 4
GLA_DK = 32
GLA_DV = 64
GLA_GATE_RANK = 16
GLA_GATE_NORM = 16.0
GLA_CHUNK = 16
SSD_HEADS = 4
SSD_HEADDIM = 64
SSD_INNER = SSD_HEADS * SSD_HEADDIM
SSD_GROUPS = 2
SSD_STATE = 128
SSD_CONV = 5
SSD_CHUNK = 128
SSD_CONV_DIM = SSD_INNER + 2 * SSD_GROUPS * SSD_STATE
MLA_IN = MLA_Q_LORA + MLA_KV_LORA + MLA_ROPE
GLA_IN = 2 * GLA_HEADS * GLA_DK + 2 * GLA_HEADS * GLA_DV + 2 * GLA_GATE_RANK
SSD_IN = SSD_INNER + SSD_CONV_DIM + 2 * SSD_HEADS

LANES = 128
SUBLANES = 8
MXU_DIM = 256
VMEM_LIMIT = 56 * 1024 * 1024

HEAD_PAD = LANES
V_ROWS = MLA_V + 16
GLA_QK = GLA_HEADS * GLA_DK
GLA_V = GLA_HEADS * GLA_DV
FF_CHUNK = MXU_DIM
N_FF_CHUNKS = D_FF // FF_CHUNK

P_MLA = MLA_Q_LORA + MLA_KV_LORA + HEAD_PAD
P_QKV = 2 * GLA_QK + GLA_V
P_LR = LANES
P_G = GLA_V
P_Z = SSD_INNER
P_XBC = SSD_CONV_DIM
P_DT = LANES
P_TOTAL = P_MLA + P_QKV + P_LR + P_G + P_Z + P_XBC + P_DT

NEG_BIG = -1e30
LOG2E = 1.4426950408889634


def _rms(x, gain):
    return x * lax.rsqrt(jnp.mean(x * x, axis=-1, keepdims=True) + EPS) * gain


def _split_hi_lo(x):
    hi = x.astype(BF16)
    lo = (x - hi.astype(F32)).astype(BF16)
    return hi, lo


def _dot(a, b):
    return jnp.dot(a, b, preferred_element_type=F32)


def _dot_nt(a, b):
    return lax.dot_general(a, b, (((1,), (1,)), ((), ())), preferred_element_type=F32)


def _dot_tn(a, b):
    return lax.dot_general(a, b, (((0,), (0,)), ((), ())), preferred_element_type=F32)


def _cparams(sem):
    return pltpu.CompilerParams(dimension_semantics=sem, vmem_limit_bytes=VMEM_LIMIT)


def _const_spec(shape):
    nd = len(shape)
    return pl.BlockSpec(shape, lambda *_: (0,) * nd, pipeline_mode=pl.Buffered(1))


def _swiglu(h_ref, wg_ref, wu_ref, wd_ref, a_ref):
    for c in range(N_FF_CHUNKS):
        h = h_ref[...]
        g = _dot(h, wg_ref[c])
        u = _dot(h, wu_ref[c])
        a_ref[:, c * FF_CHUNK:(c + 1) * FF_CHUNK] = (g * jax.nn.sigmoid(g) * u).astype(BF16)
    return _dot(a_ref[...], wd_ref[...])


def _ffn_in_kernel(x_ref, n1_ref, wg_ref, wu_ref, wd_ref, n2_ref, win_ref,
                   x1_ref, mla_ref, qkv_ref, lr_ref, g_ref, z_ref, xbc_ref, dt_ref,
                   h_ref, a_ref):
    x = x_ref[...]
    h_ref[...] = _rms(x, n1_ref[...]).astype(BF16)
    x1 = x + 0.5 * _swiglu(h_ref, wg_ref, wu_ref, wd_ref, a_ref)
    x1_ref[...] = x1
    h2 = _rms(x1, n2_ref[...]).astype(BF16)
    off = 0
    for ref, width in ((mla_ref, P_MLA), (qkv_ref, P_QKV), (lr_ref, P_LR), (g_ref, P_G),
                       (z_ref, P_Z), (xbc_ref, P_XBC), (dt_ref, P_DT)):
        ref[...] = _dot(h2, win_ref[:, off:off + width])
        off += width


def _ffn_in(x, n1, wg, wu, wd, n2, win, tm):
    t = x.shape[0]
    widths = (D_MODEL, P_MLA, P_QKV, P_LR, P_G, P_Z, P_XBC, P_DT)
    tok = lambda w: pl.BlockSpec((tm, w), lambda i: (i, 0))
    return pl.pallas_call(
        _ffn_in_kernel,
        grid=(t // tm,),
        in_specs=[tok(D_MODEL), _const_spec(n1.shape), _const_spec(wg.shape), _const_spec(wu.shape),
                  _const_spec(wd.shape), _const_spec(n2.shape), _const_spec(win.shape)],
        out_specs=[tok(w) for w in widths],
        out_shape=[jax.ShapeDtypeStruct((t, w), F32) for w in widths],
        scratch_shapes=[pltpu.VMEM((tm, D_MODEL), BF16), pltpu.VMEM((tm, D_FF), BF16)],
        compiler_params=_cparams(("parallel",)),
        name="ffn_in",
    )(x, n1, wg, wu, wd, n2, win)


def _ffn_out_kernel(x1_ref, omla_ref, ogf_ref, ogb_ref, g_ref, ysf_ref, ysb_ref, z_ref,
                    onorm_ref, snorm_ref, blk64_ref, blk128_ref, wout_ref,
                    n_ref, wg_ref, wu_ref, wd_ref, fn_ref,
                    y_ref, h_ref, a_ref):
    og = ogf_ref[...] + ogb_ref[...]
    ss = _dot((og * og).astype(BF16), blk64_ref[...])
    g = g_ref[...]
    m_gla = og * lax.rsqrt(ss * (1.0 / GLA_DV) + EPS) * onorm_ref[...] * (g * jax.nn.sigmoid(g))
    z = z_ref[...]
    ys = (ysf_ref[...] + ysb_ref[...]) * (z * jax.nn.sigmoid(z))
    ss2 = _dot((ys * ys).astype(BF16), blk128_ref[...])
    m_ssd = ys * lax.rsqrt(ss2 * (1.0 / (SSD_INNER // SSD_GROUPS)) + EPS) * snorm_ref[...]
    m = jnp.concatenate([omla_ref[...], m_gla.astype(BF16), m_ssd.astype(BF16)], axis=-1)
    x2 = x1_ref[...] + _dot(m, wout_ref[...])
    h_ref[...] = _rms(x2, n_ref[...]).astype(BF16)
    x3 = x2 + 0.5 * _swiglu(h_ref, wg_ref, wu_ref, wd_ref, a_ref)
    y_ref[...] = _rms(x3, fn_ref[...])


def _ffn_out(x1, omla, ogf, ogb, g, ysf, ysb, z, onorm, snorm, blk64, blk128, wout, n, wg, wu, wd, fn, tm):
    t = x1.shape[0]
    tok = lambda w: pl.BlockSpec((tm, w), lambda i: (i, 0))
    consts = (onorm, snorm, blk64, blk128, wout, n, wg, wu, wd, fn)
    return pl.pallas_call(
        _ffn_out_kernel,
        grid=(t // tm,),
        in_specs=[tok(D_MODEL), tok(MLA_HEADS * MLA_V), tok(GLA_V), tok(GLA_V), tok(GLA_V), tok(SSD_INNER),
                  tok(SSD_INNER), tok(SSD_INNER)]
        + [_const_spec(c.shape) for c in consts],
        out_specs=tok(D_MODEL),
        out_shape=jax.ShapeDtypeStruct((t, D_MODEL), F32),
        scratch_shapes=[pltpu.VMEM((tm, D_MODEL), BF16), pltpu.VMEM((tm, D_FF), BF16)],
        compiler_params=_cparams(("parallel",)),
        name="ffn_out",
    )(x1, omla, ogf, ogb, g, ysf, ysb, z, *consts)


def _mla_prep_kernel(p_ref, qn_ref, wuqt_ref, kvn_ref, wk_ref, wvt_ref, gq_ref, gk_ref,
                     cost_ref, sint_ref, cn_ref, s1_ref, s2_ref,
                     qt_ref, k_ref, vt_ref):
    p = p_ref[0]
    cq = p[:, :MLA_Q_LORA]
    ckv = p[:, MLA_Q_LORA:MLA_Q_LORA + MLA_KV_LORA]
    kr = p[:, MLA_Q_LORA + MLA_KV_LORA:]
    hq = _rms(cq, qn_ref[...]).astype(BF16)
    hkv = _rms(ckv, kvn_ref[...]).astype(BF16)

    qt = _dot_nt(wuqt_ref[...], hq)
    cos_t = cost_ref[...]
    sin_t = sint_ref[...]
    gq = gq_ref[...]
    half = MLA_ROPE // 2
    for h in range(MLA_HEADS):
        x = qt[h * HEAD_PAD:(h + 1) * HEAD_PAD]
        ss = jnp.sum(x * x, axis=0, keepdims=True)
        x = x * lax.rsqrt(ss * (1.0 / MLA_QK) + EPS) * gq
        x1 = x[MLA_NOPE:MLA_NOPE + half]
        x2 = x[MLA_NOPE + half:MLA_QK]
        qt_ref[0, h, 0:MLA_NOPE, :] = x[0:MLA_NOPE].astype(BF16)
        qt_ref[0, h, MLA_NOPE:MLA_NOPE + half, :] = (x1 * cos_t - x2 * sin_t).astype(BF16)
        qt_ref[0, h, MLA_NOPE + half:MLA_QK, :] = (x1 * sin_t + x2 * cos_t).astype(BF16)
        qt_ref[0, h, MLA_QK:HEAD_PAD, :] = x[MLA_QK:HEAD_PAD].astype(BF16)

    gk = gk_ref[...]
    krg = kr * gk
    k_rot = (krg * cn_ref[...] + pltpu.roll(krg, HEAD_PAD - half, axis=1) * s1_ref[...]
             + pltpu.roll(krg, half, axis=1) * s2_ref[...])
    ss_rope = jnp.sum(kr * kr, axis=-1, keepdims=True)
    kn = _dot(hkv, wk_ref[...])
    for h in range(MLA_HEADS):
        x = kn[:, h * HEAD_PAD:(h + 1) * HEAD_PAD]
        ss = jnp.sum(x * x, axis=-1, keepdims=True) + ss_rope
        k_ref[0, h] = ((x * gk + k_rot) * lax.rsqrt(ss * (1.0 / MLA_QK) + EPS)).astype(BF16)

    vt = _dot_nt(wvt_ref[...], hkv)
    ones_tile = jnp.where(lax.broadcasted_iota(jnp.int32, (V_ROWS - MLA_V, vt.shape[1]), 0) == 0, 1.0, 0.0).astype(BF16)
    for h in range(MLA_HEADS):
        vt_ref[0, h, 0:MLA_V, :] = vt[h * MLA_V:(h + 1) * MLA_V].astype(BF16)
        vt_ref[0, h, MLA_V:V_ROWS, :] = ones_tile


def _mla_prep(p_mla, qn, wuqt, kvn, wk, wvt, gq, gk, cos_t, sin_t, c_n, s1_n, s2_n, tm):
    b, l, _ = p_mla.shape
    consts = (qn, wuqt, kvn, wk, wvt, gq, gk)
    half = MLA_ROPE // 2
    return pl.pallas_call(
        _mla_prep_kernel,
        grid=(b, l // tm),
        in_specs=[pl.BlockSpec((1, tm, P_MLA), lambda i, j: (i, j, 0))]
        + [_const_spec(c.shape) for c in consts]
        + [pl.BlockSpec((half, tm), lambda i, j: (0, j)), pl.BlockSpec((half, tm), lambda i, j: (0, j)),
           pl.BlockSpec((tm, HEAD_PAD), lambda i, j: (j, 0)), pl.BlockSpec((tm, HEAD_PAD), lambda i, j: (j, 0)),
           pl.BlockSpec((tm, HEAD_PAD), lambda i, j: (j, 0))],
        out_specs=[pl.BlockSpec((1, MLA_HEADS, HEAD_PAD, tm), lambda i, j: (i, 0, 0, j)),
                   pl.BlockSpec((1, MLA_HEADS, tm, HEAD_PAD), lambda i, j: (i, 0, j, 0)),
                   pl.BlockSpec((1, MLA_HEADS, V_ROWS, tm), lambda i, j: (i, 0, 0, j))],
        out_shape=[jax.ShapeDtypeStruct((b, MLA_HEADS, HEAD_PAD, l), BF16),
                   jax.ShapeDtypeStruct((b, MLA_HEADS, l, HEAD_PAD), BF16),
                   jax.ShapeDtypeStruct((b, MLA_HEADS, V_ROWS, l), BF16)],
        compiler_params=_cparams(("parallel", "parallel")),
        name="mla_prep",
    )(p_mla, *consts, cos_t, sin_t, c_n, s1_n, s2_n)


ATTN_HEADS_PER_STEP = 2
ATTN_Q_SPLIT = 2
ATTN_LOOKAHEAD = 2


def _mla_attn_kernel(qt_ref, k_ref, vt_ref, o_ref, ot_ref, s_ref, *, tk, lookahead):
    n_kb = k_ref.shape[2] // tk
    tq = qt_ref.shape[3]
    n_slots = lookahead + 1
    total = ATTN_HEADS_PER_STEP * n_kb
    qts = [qt_ref[0, hh] for hh in range(ATTN_HEADS_PER_STEP)]

    def scores(g):
        hh, kb = divmod(g, n_kb)
        return _dot(k_ref[0, hh, kb * tk:(kb + 1) * tk, :], qts[hh])

    for g in range(min(lookahead, total)):
        s_ref[g % n_slots] = scores(g)
    halves = [slice(c, c + tq // ATTN_Q_SPLIT) for c in range(0, tq, tq // ATTN_Q_SPLIT)]
    m = acc = None
    for g in range(total):
        hh, kb = divmod(g, n_kb)
        if kb == 0:
            m = [jnp.full((1, tq // ATTN_Q_SPLIT), -jnp.inf, F32) for _ in halves]
            acc = [jnp.zeros((V_ROWS, tq // ATTN_Q_SPLIT), F32) for _ in halves]
        if g + lookahead < total:
            s_ref[(g + lookahead) % n_slots] = scores(g + lookahead)
        vt = vt_ref[0, hh, :, kb * tk:(kb + 1) * tk]
        for i, cols in enumerate(halves):
            s = s_ref[g % n_slots, :, cols]
            m_new = jnp.maximum(m[i], jnp.max(s, axis=0, keepdims=True))
            alpha = jnp.exp2(m[i] - m_new)
            p = jnp.exp2(s - m_new)
            acc[i] = alpha * acc[i] + _dot(vt, p.astype(BF16))
            m[i] = m_new
            if kb == n_kb - 1:
                ot_ref[hh * MLA_V:(hh + 1) * MLA_V, cols] = acc[i][0:MLA_V] * (1.0 / acc[i][MLA_V:MLA_V + 1])
    o_ref[0] = ot_ref[...].T.astype(BF16)


def _mla_attn(qt, k, vt, tq, tk):
    b, nh, _, l = qt.shape
    hp = ATTN_HEADS_PER_STEP
    lookahead = min(ATTN_LOOKAHEAD, hp * (l // tk) - 1)
    return pl.pallas_call(
        functools.partial(_mla_attn_kernel, tk=tk, lookahead=lookahead),
        grid=(b, nh // hp, l // tq),
        in_specs=[pl.BlockSpec((1, hp, HEAD_PAD, tq), lambda i, h, j: (i, h, 0, j)),
                  pl.BlockSpec((1, hp, l, HEAD_PAD), lambda i, h, j: (i, h, 0, 0)),
                  pl.BlockSpec((1, hp, V_ROWS, l), lambda i, h, j: (i, h, 0, 0))],
        out_specs=pl.BlockSpec((1, tq, hp * MLA_V), lambda i, h, j: (i, j, h)),
        out_shape=jax.ShapeDtypeStruct((b, l, nh * MLA_V), BF16),
        scratch_shapes=[pltpu.VMEM((hp * MLA_V, tq), F32),
                        pltpu.VMEM((lookahead + 1, tk, tq), F32)],
        compiler_params=_cparams(("parallel", "parallel", "arbitrary")),
        name="mla_attn",
    )(qt, k, vt)


def _gla_kernel(qkv_f_ref, qkv_b_ref, lr_f_ref, lr_b_ref, wgh_ref, wgl_ref, bg_ref, cum_ref, tot_ref, ind_ref,
                vmask_ref, hmask_ref, o_f_ref, o_b_ref, st_ref, upd_ref, p_ref, *, tg):
    n_chunks = tg // GLA_CHUNK
    chunk_rows = [slice(c * GLA_CHUNK, (c + 1) * GLA_CHUNK) for c in range(n_chunks)]
    hmask = hmask_ref[...]
    vkeep = vmask_ref[...] > 0

    @pl.when(pl.program_id(1) == 0)
    def _():
        st_ref[...] = jnp.zeros_like(st_ref)

    def head(d, qkv_ref, lr_ref):
        qkv = qkv_ref[0]
        q = qkv[:, :GLA_QK] * (GLA_DK ** -0.5)
        k = qkv[:, GLA_QK:2 * GLA_QK]
        v = qkv[:, 2 * GLA_QK:]
        lr_hi, lr_lo = _split_hi_lo(lr_ref[0])
        zg = _dot(lr_hi, wgh_ref[d]) + _dot(lr_lo, wgh_ref[d]) + _dot(lr_hi, wgl_ref[d]) + bg_ref[d]
        log_a = (jnp.minimum(zg, 0.0) - jnp.log1p(jnp.exp(-jnp.abs(zg)))) * (1.0 / GLA_GATE_NORM)
        la_hi, la_lo = _split_hi_lo(log_a)
        bcum = _dot(cum_ref[d], la_hi) + _dot(cum_ref[d], la_lo)
        btot = _dot(tot_ref[...], la_hi) + _dot(tot_ref[...], la_lo)
        return dict(q=q, k=k, bcum=bcum, qd=(q * jnp.exp(bcum)).astype(BF16),
                    kd=(k * jnp.exp(btot - bcum)).astype(BF16), vb=v.astype(BF16), dec=jnp.exp(btot))

    def products(d, h):
        q3 = h["q"].reshape(n_chunks, GLA_CHUNK, GLA_QK)
        k3 = h["k"].reshape(n_chunks, GLA_CHUNK, GLA_QK)
        b3 = h["bcum"].reshape(n_chunks, GLA_CHUNK, GLA_QK)
        tin = lax.broadcasted_iota(jnp.int32, (n_chunks, GLA_CHUNK, GLA_QK), 1)
        pending = list(enumerate(chunk_rows))
        for j in range(GLA_CHUNK):
            for c, rows in pending[j::GLA_CHUNK]:
                upd_ref[d, c] = _dot_tn(h["vb"][rows], h["kd"][rows]) * hmask
            valid = (tin >= j) if d == 0 else (tin <= j)
            e = jnp.exp(jnp.where(valid, b3 - b3[:, j:j + 1, :], NEG_BIG))
            p_ref[d, :, j * GLA_QK:(j + 1) * GLA_QK] = (
                q3 * k3[:, j:j + 1, :] * e).reshape(tg, GLA_QK).astype(BF16)

    def outputs(d, h, o_ref):
        a_intra = _dot(p_ref[d], ind_ref[...]).astype(BF16)
        st = st_ref[d]
        for c in (range(n_chunks) if d == 0 else reversed(range(n_chunks))):
            rows = chunk_rows[c]
            v_tiled = jnp.concatenate([h["vb"][rows]] * GLA_HEADS, axis=0)
            v_rows = jnp.where(vkeep, v_tiled, jnp.zeros_like(v_tiled))
            lhs = jnp.concatenate([h["qd"][rows], a_intra[rows]], axis=1)
            o_ref[0, rows, :] = _dot(lhs, jnp.concatenate([st.T.astype(BF16), v_rows], axis=0))
            st = st * h["dec"][c * GLA_CHUNK:c * GLA_CHUNK + 1, :] + upd_ref[d, c]
            yield
        st_ref[d] = st

    h_f = head(0, qkv_f_ref, lr_f_ref)
    h_b = head(1, qkv_b_ref, lr_b_ref)
    products(0, h_f)
    products(1, h_b)
    for _ in itertools.zip_longest(outputs(0, h_f, o_f_ref), outputs(1, h_b, o_b_ref)):
        pass


def _gla_scan(qkv, lr, wgh, wgl, bg, cum, tot, ind, vmask, hmask, tg):
    b, l, _ = qkv.shape
    nt = l // tg
    n_chunks = tg // GLA_CHUNK
    fwd = lambda i, t: (i, t, 0)
    bwd = lambda i, t: (i, nt - 1 - t, 0)
    consts = (wgh, wgl, bg, cum, tot, ind, vmask, hmask)
    return pl.pallas_call(
        functools.partial(_gla_kernel, tg=tg),
        grid=(b, nt),
        in_specs=[pl.BlockSpec((1, tg, P_QKV), fwd), pl.BlockSpec((1, tg, P_QKV), bwd),
                  pl.BlockSpec((1, tg, P_LR), fwd), pl.BlockSpec((1, tg, P_LR), bwd)]
        + [_const_spec(c.shape) for c in consts],
        out_specs=[pl.BlockSpec((1, tg, GLA_V), fwd), pl.BlockSpec((1, tg, GLA_V), bwd)],
        out_shape=[jax.ShapeDtypeStruct((b, l, GLA_V), F32)] * 2,
        scratch_shapes=[pltpu.VMEM((2, GLA_V, GLA_QK), F32),
                        pltpu.VMEM((2, n_chunks, GLA_V, GLA_QK), F32),
                        pltpu.VMEM((2, tg, GLA_CHUNK * GLA_QK), BF16)],
        compiler_params=_cparams(("parallel", "arbitrary")),
        name="gla_scan",
    )(qkv, qkv, lr, lr, *consts)


CONV_HALO = SUBLANES
SSD_TILE_CHUNKS = 8


def _softplus(x):
    return jnp.maximum(x, 0.0) + jnp.log1p(jnp.exp(-jnp.abs(x)))


def _ssd_kernel(xc_f_ref, xp_f_ref, xn_f_ref, xc_b_ref, xp_b_ref, xn_b_ref, dt_f_ref, dt_b_ref, dtt_f_ref, dtt_b_ref,
                cw_ref, cb_ref, bias_r_ref, bias_c_ref, alog_r_ref, alog_c_ref, dskip_ref, tri_ref, shift_ref, rep_ref,
                y_f_ref, y_b_ref, xe_ref, st_ref, upd_ref, dec_ref, cgb_ref, e64_ref, *, n_tiles, n_chunks):
    t = pl.program_id(1)
    cs = SSD_CHUNK
    ts = n_chunks * cs
    hpg = SSD_HEADS // SSD_GROUPS
    gw = hpg * SSD_HEADDIM
    pad = SSD_CONV // 2
    neg_a_r = -jnp.exp(alog_r_ref[...])
    neg_a_c = -jnp.exp(alog_c_ref[...])
    lane_head = lax.broadcasted_iota(jnp.int32, (cs, gw), 1) // SSD_HEADDIM

    @pl.when(t == 0)
    def _():
        st_ref[...] = jnp.zeros_like(st_ref)

    def direction(d, xc_ref, xp_ref, xn_ref, dt_ref, dtt_ref, y_ref):
        tt = t if d == 0 else n_tiles - 1 - t
        xe_ref[d, 0:CONV_HALO, :] = jnp.where(tt == 0, 0.0, xp_ref[0])
        xe_ref[d, CONV_HALO:CONV_HALO + ts, :] = xc_ref[0]
        xe_ref[d, CONV_HALO + ts:, :] = jnp.where(tt == n_tiles - 1, 0.0, xn_ref[0])
        tri_d = tri_ref[d]
        tri_o = tri_ref[1 - d]
        mask = tri_d.astype(F32) > 0.5
        rep64 = rep_ref[d]

        def front(c):
            rows = slice(c * cs, (c + 1) * cs)
            xe_b = xe_ref[d, c * cs:(c + 1) * cs + 2 * CONV_HALO, :].astype(BF16)
            acc = cb_ref[...] + cw_ref[pad:pad + 1, :] * xe_ref[d, CONV_HALO + c * cs:CONV_HALO + (c + 1) * cs, :]
            for i, kk in enumerate(k for k in range(SSD_CONV) if k != pad):
                acc = acc + cw_ref[kk:kk + 1, :] * _dot(shift_ref[i], xe_b)
            act = acc * jax.nn.sigmoid(acc)
            xs = act[:, :SSD_INNER]
            bm = act[:, SSD_INNER:SSD_INNER + SSD_GROUPS * SSD_STATE]
            cmb = act[:, SSD_INNER + SSD_GROUPS * SSD_STATE:].astype(BF16)
            dt_c = _softplus(dt_ref[0, rows, :] + bias_r_ref[...])
            la_c = dt_c * neg_a_r
            dt_r = _softplus(dtt_ref[0, :, rows] + bias_c_ref[...])
            la_r = dt_r * neg_a_c
            lc_hi, lc_lo = _split_hi_lo(la_c)
            cum_c = _dot(tri_d, lc_hi) + _dot(tri_d, lc_lo)
            lr_hi, lr_lo = _split_hi_lo(la_r)
            cum_r = _dot(lr_hi, tri_o) + _dot(lr_lo, tri_o)
            dt_hi, dt_lo = _split_hi_lo(dt_c)
            dt64 = _dot(dt_hi, rep64) + _dot(dt_lo, rep64)
            cu_hi, cu_lo = _split_hi_lo(cum_c)
            cum64 = _dot(cu_hi, rep64) + _dot(cu_lo, rep64)
            return xs, bm, cmb, cum_r, dt64, cum64

        def back(c, vals):
            xs, bm, cmb, cum_r, dt64, cum64 = vals
            rows = slice(c * cs, (c + 1) * cs)
            tot64 = cum64[cs - 1:cs, :] if d == 0 else cum64[0:1, :]
            xc = xs * dt64
            xcb = xc.astype(BF16)
            xdec = (xc * jnp.exp(tot64 - cum64)).astype(BF16)
            e64_ref[d, rows, :] = jnp.exp(cum64)
            dec_ref[d, c] = jnp.exp(tot64)
            cgb_ref[d, rows, :] = cmb
            ys = []
            for g in range(SSD_GROUPS):
                gl = slice(g * gw, (g + 1) * gw)
                bg = bm[:, g * SSD_STATE:(g + 1) * SSD_STATE]
                cb = _dot_nt(cmb[:, g * SSD_STATE:(g + 1) * SSD_STATE], bg.astype(BF16))
                y_g = jnp.zeros((cs, gw), F32)
                cum_g = cum64[:, gl]
                cum_swapped = pltpu.roll(cum_g, SSD_HEADDIM, axis=1)
                for hg in range(hpg):
                    h = d * SSD_HEADS + g * hpg + hg
                    cum_h = jnp.where(lane_head == hg, cum_g, cum_swapped)
                    seg = jnp.exp(jnp.where(mask, cum_h - cum_r[h:h + 1, :], NEG_BIG))
                    yd = _dot((cb * seg).astype(BF16), xcb[:, gl])
                    y_g = y_g + jnp.where(lane_head == hg, yd, 0.0)
                ys.append(y_g)
                upd_ref[d, c, g] = _dot(bg.T.astype(BF16), xdec[:, gl])
            y = jnp.concatenate(ys, axis=1)
            y_ref[0, rows, :] = y + dskip_ref[...] * xs if d == 0 else y

        vals = front(0)
        for c in range(n_chunks):
            nxt = front(c + 1) if c + 1 < n_chunks else None
            back(c, vals)
            vals = nxt
            yield

        st = [st_ref[d, g] for g in range(SSD_GROUPS)]
        for c in (range(n_chunks) if d == 0 else reversed(range(n_chunks))):
            rows = slice(c * cs, (c + 1) * cs)
            y_off = jnp.concatenate(
                [_dot(cgb_ref[d, rows, g * SSD_STATE:(g + 1) * SSD_STATE], st[g].astype(BF16))
                 for g in range(SSD_GROUPS)], axis=1)
            y_ref[0, rows, :] = y_ref[0, rows, :] + e64_ref[d, rows, :] * y_off
            dec = dec_ref[d, c]
            st = [st[g] * dec[:, g * gw:(g + 1) * gw] + upd_ref[d, c, g] for g in range(SSD_GROUPS)]
            yield
        for g in range(SSD_GROUPS):
            st_ref[d, g] = st[g]

    for _ in itertools.zip_longest(direction(0, xc_f_ref, xp_f_ref, xn_f_ref, dt_f_ref, dtt_f_ref, y_f_ref),
                                   direction(1, xc_b_ref, xp_b_ref, xn_b_ref, dt_b_ref, dtt_b_ref, y_b_ref)):
        pass


def _ssd_scan(xbc, dt, dtt, cw, cb, bias_r, bias_c, alog_r, alog_c, dskip, tri, shift, rep, n_chunks):
    b, l, _ = xbc.shape
    ts = n_chunks * SSD_CHUNK
    nt = l // ts
    hb = ts // CONV_HALO
    n_hb = l // CONV_HALO
    gw = SSD_INNER // SSD_GROUPS
    tiles = (lambda t: t, lambda t: nt - 1 - t)
    cur = [lambda i, t, f=f: (i, f(t), 0) for f in tiles]
    prev = [lambda i, t, f=f: (i, jnp.maximum(f(t) * hb - 1, 0), 0) for f in tiles]
    nxt = [lambda i, t, f=f: (i, jnp.minimum((f(t) + 1) * hb, n_hb - 1), 0) for f in tiles]
    tok_rows = [lambda i, t, f=f: (i, 0, f(t)) for f in tiles]
    consts = (cw, cb, bias_r, bias_c, alog_r, alog_c, dskip, tri, shift, rep)
    halo = lambda m: pl.BlockSpec((1, CONV_HALO, P_XBC), m)
    return pl.pallas_call(
        functools.partial(_ssd_kernel, n_tiles=nt, n_chunks=n_chunks),
        grid=(b, nt),
        in_specs=[pl.BlockSpec((1, ts, P_XBC), cur[0]), halo(prev[0]), halo(nxt[0]),
                  pl.BlockSpec((1, ts, P_XBC), cur[1]), halo(prev[1]), halo(nxt[1]),
                  pl.BlockSpec((1, ts, P_DT), cur[0]), pl.BlockSpec((1, ts, P_DT), cur[1]),
                  pl.BlockSpec((1, 2 * SSD_HEADS, ts), tok_rows[0]), pl.BlockSpec((1, 2 * SSD_HEADS, ts), tok_rows[1])]
        + [_const_spec(c.shape) for c in consts],
        out_specs=[pl.BlockSpec((1, ts, SSD_INNER), cur[0]), pl.BlockSpec((1, ts, SSD_INNER), cur[1])],
        out_shape=[jax.ShapeDtypeStruct((b, l, SSD_INNER), F32)] * 2,
        scratch_shapes=[pltpu.VMEM((2, ts + 2 * CONV_HALO, P_XBC), F32),
                        pltpu.VMEM((2, SSD_GROUPS, SSD_STATE, gw), F32),
                        pltpu.VMEM((2, n_chunks, SSD_GROUPS, SSD_STATE, gw), F32),
                        pltpu.VMEM((2, n_chunks, 1, SSD_INNER), F32),
                        pltpu.VMEM((2, ts, SSD_GROUPS * SSD_STATE), BF16),
                        pltpu.VMEM((2, ts, SSD_INNER), F32)],
        compiler_params=_cparams(("parallel", "arbitrary")),
        name="ssd_scan",
    )(xbc, xbc, xbc, xbc, xbc, xbc, dt, dt, dtt, dtt, *consts)


def _pad_cols(w, width, at=0):
    out = jnp.zeros(w.shape[:-1] + (width,), w.dtype)
    return out.at[..., at:at + w.shape[-1]].set(w)


def _prep_layer(i, ffn1_norm, ffn1_w_gu, ffn1_w_down, mix_norm, w_in, mla_q_norm, mla_w_uq, mla_kv_norm,
                mla_w_ukv, mla_q_gain, mla_k_gain, gla_w_gate, gla_b_gate, gla_o_norm, ssd_conv_w, ssd_conv_b,
                ssd_a_log, ssd_dt_bias, ssd_d, ssd_norm, w_out, ffn2_norm, ffn2_w_gu, ffn2_w_down, final_norm):
    def ffn_w(w_gu, w_down):
        wg = w_gu[:, :D_FF].reshape(D_MODEL, N_FF_CHUNKS, FF_CHUNK).transpose(1, 0, 2).astype(BF16)
        wu = w_gu[:, D_FF:].reshape(D_MODEL, N_FF_CHUNKS, FF_CHUNK).transpose(1, 0, 2).astype(BF16)
        wd = w_down.astype(BF16)
        return wg, wu, wd

    w = {}
    w["n1"] = ffn1_norm[i][None, :]
    w["wg1"], w["wu1"], w["wd1"] = ffn_w(ffn1_w_gu[i], ffn1_w_down[i])
    w["nmix"] = mix_norm[i][None, :]

    wi = w_in[i]
    o_gla = MLA_IN
    o_ssd = MLA_IN + GLA_IN
    o_mla_kr = MLA_Q_LORA + MLA_KV_LORA
    gla_qkv_w = 2 * GLA_QK + GLA_V
    cols = [
        wi[:, :o_mla_kr],
        _pad_cols(wi[:, o_mla_kr:MLA_IN], HEAD_PAD, at=MLA_NOPE),
        wi[:, o_gla:o_gla + gla_qkv_w],
        _pad_cols(wi[:, o_gla + gla_qkv_w + GLA_V:o_ssd], P_LR),
        wi[:, o_gla + gla_qkv_w:o_gla + gla_qkv_w + GLA_V],
        wi[:, o_ssd:o_ssd + SSD_INNER],
        wi[:, o_ssd + SSD_INNER:o_ssd + SSD_INNER + SSD_CONV_DIM],
        _pad_cols(wi[:, o_ssd + SSD_INNER + SSD_CONV_DIM:], P_DT),
    ]
    w["win"] = jnp.concatenate(cols, axis=1).astype(BF16)

    w["qn"] = mla_q_norm[i][None, :]
    w["kvn"] = mla_kv_norm[i][None, :]
    wuq = _pad_cols(mla_w_uq[i].reshape(MLA_Q_LORA, MLA_HEADS, MLA_QK), HEAD_PAD)
    w["wuqt"] = wuq.reshape(MLA_Q_LORA, MLA_HEADS * HEAD_PAD).T.astype(BF16)
    wukv = mla_w_ukv[i].reshape(MLA_KV_LORA, MLA_HEADS, MLA_NOPE + MLA_V)
    w["wk"] = _pad_cols(wukv[..., :MLA_NOPE], HEAD_PAD).reshape(MLA_KV_LORA, MLA_HEADS * HEAD_PAD).astype(BF16)
    w["wvt"] = wukv[..., MLA_NOPE:].reshape(MLA_KV_LORA, MLA_HEADS * MLA_V).T.astype(BF16)
    w["gq"] = (_pad_cols(mla_q_gain[i], HEAD_PAD) * (MLA_QK ** -0.5 * LOG2E))[:, None]
    w["gk"] = _pad_cols(mla_k_gain[i], HEAD_PAD)[None, :]

    wgate = jnp.zeros((2, LANES, GLA_QK), F32)
    for zdir in range(2):
        wgate = wgate.at[zdir, zdir * GLA_GATE_RANK:(zdir + 1) * GLA_GATE_RANK, :].set(gla_w_gate[i, zdir])
    w["wgh"] = wgate.astype(BF16)
    w["wgl"] = (wgate - w["wgh"].astype(F32)).astype(BF16)
    w["bg"] = gla_b_gate[i][:, None, :]
    w["onorm"] = jnp.tile(gla_o_norm[i], GLA_HEADS)[None, :]

    w["cw"] = jnp.zeros((SUBLANES, SSD_CONV_DIM), F32).at[:SSD_CONV].set(ssd_conv_w[i])
    w["cb"] = ssd_conv_b[i][None, :]
    flat_bias = ssd_dt_bias[i].reshape(2 * SSD_HEADS)
    flat_alog = ssd_a_log[i].reshape(2 * SSD_HEADS)
    w["bias_r"] = _pad_cols(flat_bias, P_DT)[None, :]
    w["bias_c"] = flat_bias[:, None]
    w["alog_r"] = _pad_cols(flat_alog, P_DT)[None, :]
    w["alog_c"] = flat_alog[:, None]
    w["dskip"] = jnp.repeat(ssd_d[i], SSD_HEADDIM)[None, :]
    w["snorm"] = ssd_norm[i][None, :]

    w["wout"] = w_out[i].astype(BF16)
    w["n2"] = ffn2_norm[i][None, :]
    w["wg2"], w["wu2"], w["wd2"] = ffn_w(ffn2_w_gu[i], ffn2_w_down[i])
    w["fn"] = final_norm[i][None, :]
    return w


def _block_ones(n, blk):
    idx = np.arange(n) // blk
    return jnp.asarray(idx[:, None] == idx[None, :], BF16)


def _seq_consts(l, tg):
    half = MLA_ROPE // 2
    pos = jnp.arange(l, dtype=F32)
    inv_freq = 1.0 / (ROPE_BASE ** (jnp.arange(0, MLA_ROPE, 2, dtype=F32) / MLA_ROPE))
    ang = pos[:, None] * inv_freq[None, :]
    cos, sin = jnp.cos(ang), jnp.sin(ang)
    c = {}
    c["cos_t"], c["sin_t"] = cos.T, sin.T
    ones = jnp.ones((l, HEAD_PAD), F32)
    c["c_n"] = ones.at[:, MLA_NOPE:MLA_NOPE + half].set(cos).at[:, MLA_NOPE + half:MLA_QK].set(cos)
    zeros = jnp.zeros((l, HEAD_PAD), F32)
    c["s1_n"] = zeros.at[:, MLA_NOPE:MLA_NOPE + half].set(-sin)
    c["s2_n"] = zeros.at[:, MLA_NOPE + half:MLA_QK].set(sin)

    r = np.arange(tg)
    same = (r[:, None] // GLA_CHUNK) == (r[None, :] // GLA_CHUNK)
    c["gla_cum"] = jnp.asarray(np.stack([same & (r[None, :] <= r[:, None]), same & (r[None, :] >= r[:, None])]), BF16)
    c["gla_tot"] = jnp.asarray(same, BF16)
    qk_head = np.arange(GLA_QK) // GLA_DK
    v_head = np.arange(GLA_V) // GLA_DV
    p_col = np.arange(GLA_CHUNK * GLA_QK)
    a_col = np.arange(GLA_HEADS * GLA_CHUNK)
    c["gla_ind"] = jnp.asarray(((p_col % GLA_QK) // GLA_DK * GLA_CHUNK + p_col // GLA_QK)[:, None] == a_col[None, :], BF16)
    c["gla_vmask"] = jnp.asarray((a_col // GLA_CHUNK)[:, None] == v_head[None, :], BF16)
    c["gla_hmask"] = jnp.asarray(v_head[:, None] == qk_head[None, :], F32)

    q = np.arange(SSD_CHUNK)
    c["ssd_tri"] = jnp.asarray(np.stack([q[None, :] <= q[:, None], q[None, :] >= q[:, None]]), BF16)
    xe_row = np.arange(SSD_CHUNK + 2 * CONV_HALO)
    offs = [k - SSD_CONV // 2 for k in range(SSD_CONV) if k != SSD_CONV // 2]
    c["ssd_shift"] = jnp.asarray(np.stack([xe_row[None, :] == (q[:, None] + CONV_HALO + o) for o in offs]), BF16)
    src_lane = np.arange(P_DT)
    c["ssd_rep"] = jnp.asarray(np.stack(
        [src_lane[:, None] == (zdir * SSD_HEADS + np.arange(SSD_INNER) // SSD_HEADDIM)[None, :] for zdir in range(2)]), BF16)
    return c


def _pick_tile(n, pref):
    t = min(n, pref)
    while n % t:
        t //= 2
    return t


def _layer(x, w, c, tg):
    b, l, _ = x.shape
    t = b * l
    tm = _pick_tile(t, 512)
    x1, p_mla, p_qkv, p_lr, p_g, p_z, p_xbc, p_dt = _ffn_in(
        x.reshape(t, D_MODEL), w["n1"], w["wg1"], w["wu1"], w["wd1"], w["nmix"], w["win"], tm)

    qt, k, vt = _mla_prep(p_mla.reshape(b, l, P_MLA), w["qn"], w["wuqt"], w["kvn"], w["wk"], w["wvt"],
                          w["gq"], w["gk"], c["cos_t"], c["sin_t"], c["c_n"], c["s1_n"], c["s2_n"],
                          _pick_tile(l, 512))
    o_mla = _mla_attn(qt, k, vt, _pick_tile(l, 512), _pick_tile(l, 256))

    og_f, og_b = _gla_scan(p_qkv.reshape(b, l, P_QKV), p_lr.reshape(b, l, P_LR), w["wgh"], w["wgl"], w["bg"],
                           c["gla_cum"], c["gla_tot"], c["gla_ind"], c["gla_vmask"], c["gla_hmask"], tg)

    dt3 = p_dt.reshape(b, l, P_DT)
    dtt = jnp.swapaxes(dt3[:, :, :2 * SSD_HEADS], 1, 2)
    ys_f, ys_b = _ssd_scan(p_xbc.reshape(b, l, P_XBC), dt3, dtt, w["cw"], w["cb"], w["bias_r"], w["bias_c"],
                      w["alog_r"], w["alog_c"], w["dskip"], c["ssd_tri"], c["ssd_shift"], c["ssd_rep"],
                      _pick_tile(l // SSD_CHUNK, SSD_TILE_CHUNKS))

    y = _ffn_out(x1, o_mla.reshape(t, MLA_HEADS * MLA_V), og_f.reshape(t, GLA_V), og_b.reshape(t, GLA_V), p_g,
                 ys_f.reshape(t, SSD_INNER), ys_b.reshape(t, SSD_INNER), p_z, w["onorm"], w["snorm"],
                 _block_ones(GLA_V, GLA_DV), _block_ones(SSD_INNER, SSD_INNER // SSD_GROUPS),
                 w["wout"], w["n2"], w["wg2"], w["wu2"], w["wd2"], w["fn"], tm)
    return y.reshape(b, l, D_MODEL)


def kernel(x_prompt, x_sample, ffn1_norm, ffn1_w_gu, ffn1_w_down, mix_norm, w_in, mla_q_norm, mla_w_uq,
           mla_kv_norm, mla_w_ukv, mla_q_gain, mla_k_gain, gla_w_gate, gla_b_gate, gla_o_norm, ssd_conv_w,
           ssd_conv_b, ssd_a_log, ssd_dt_bias, ssd_d, ssd_norm, w_out, ffn2_norm, ffn2_w_gu, ffn2_w_down,
           final_norm):
    params = (ffn1_norm, ffn1_w_gu, ffn1_w_down, mix_norm, w_in, mla_q_norm, mla_w_uq, mla_kv_norm,
              mla_w_ukv, mla_q_gain, mla_k_gain, gla_w_gate, gla_b_gate, gla_o_norm, ssd_conv_w, ssd_conv_b,
              ssd_a_log, ssd_dt_bias, ssd_d, ssd_norm, w_out, ffn2_norm, ffn2_w_gu, ffn2_w_down, final_norm)
    depth = ffn1_norm.shape[0]
    streams = [x_prompt, x_sample]
    tgs = [_pick_tile(s.shape[1], 512) for s in streams]
    consts = [_seq_consts(s.shape[1], tg) for s, tg in zip(streams, tgs)]
    for i in range(depth):
        w = _prep_layer(i, *params)
        streams = [_layer(s, w, c, tg) for s, c, tg in zip(streams, consts, tgs)]
    return tuple(streams)
```
